```python
import math
import jax, jax.numpy as jnp
from jax import lax
import numpy as np

D_MODEL = 1024
BATCH = 4
SEQ = 4096
DEPTH = 2

GRID_W = 64
CTX_LEN = 256
GLA_HEADS = 4
GLA_DK = 32
GLA_DV = 64
GLA_GATE_RANK = 16
GLA_GATE_NORM = 16.0
GLA_CHUNK = 64
GQA_HEADS = 8
GQA_KV_HEADS = 2
GQA_GROUP = GQA_HEADS // GQA_KV_HEADS
GQA_HD = 64
DIFF_HEADS = 4
DIFF_QK = 32
DIFF_V = 64

ROPE_THETA = 10000.0
Q_BLOCK = 128
NORM_EPS = 1e-6
MIX_WIDTH = GLA_HEADS * GLA_DV + GQA_HEADS * GQA_HD + DIFF_HEADS * DIFF_V
IN_WIDTHS = (GLA_HEADS * GLA_DK, GLA_HEADS * GLA_DK, GLA_HEADS * GLA_DV, GLA_HEADS * GLA_DV,
             GLA_GATE_RANK, GLA_GATE_RANK,
             GQA_HEADS * GQA_HD, GQA_KV_HEADS * GQA_HD, GQA_KV_HEADS * GQA_HD,
             DIFF_HEADS * 2 * DIFF_QK, DIFF_HEADS * 2 * DIFF_QK, DIFF_HEADS * DIFF_V)
IN_WIDTH = sum(IN_WIDTHS)
D_FF = 2816
N_EXPERTS = 8
TOP_K = 2
D_FF_EXPERT = 3584
N_DENSE = (DEPTH + 1) // 2
N_MOE = DEPTH // 2

kernel_name = "hybrid_gla_gqa_diffattn_moe_dit_block"


def rms_norm(x, w):
    xf = x.astype(jnp.float32)
    y = xf * lax.rsqrt(jnp.mean(xf * xf, axis=-1, keepdims=True) + NORM_EPS)
    return (y * w.astype(jnp.float32)).astype(x.dtype)


def modulate(h, shift, scale):
    return h * (1.0 + scale) + shift


def rope_1d(x, pos):
    L, d = x.shape[1], x.shape[-1]
    half = d // 2
    freqs = ROPE_THETA ** (-jnp.arange(half, dtype=jnp.float32) / half)
    ang = pos[:, None] * freqs[None, :]
    bshape = (L,) + (1,) * (x.ndim - 3) + (half,)
    cos = jnp.cos(ang).reshape(bshape)
    sin = jnp.sin(ang).reshape(bshape)
    x1 = x[..., :half].astype(jnp.float32)
    x2 = x[..., half:].astype(jnp.float32)
    return jnp.concatenate([x1 * cos - x2 * sin, x1 * sin + x2 * cos], axis=-1).astype(x.dtype)


def axial_rope(x, rows, cols):
    d = x.shape[-1]
    return jnp.concatenate([rope_1d(x[..., : d // 2], rows), rope_1d(x[..., d // 2:], cols)], axis=-1)


def gla_chunk_scan(q, k, v, log_a, s0):
    B, L, H, dk = q.shape
    dv = v.shape[-1]
    n = L // GLA_CHUNK

    def chunked(t):
        return t.reshape(B, n, GLA_CHUNK, H, t.shape[-1]).transpose(1, 0, 3, 2, 4).astype(jnp.float32)

    qc, kc, vc, gc = chunked(q * GLA_DK ** -0.5), chunked(k), chunked(v), chunked(log_a)
    b = jnp.cumsum(gc, axis=-2)
    b_last = b[..., -1:, :]
    q_dec = qc * jnp.exp(b)
    causal = jnp.tril(jnp.ones((GLA_CHUNK, GLA_CHUNK), dtype=bool))
    A = jnp.einsum("nbhcd,nbhsd->nbhcs", q_dec, kc * jnp.exp(-b))
    A = jnp.where(causal, A, 0.0)
    o_intra = jnp.einsum("nbhcs,nbhse->nbhce", A, vc)
    kv = jnp.einsum("nbhcd,nbhce->nbhde", kc * jnp.exp(b_last - b), vc)
    decay = jnp.exp(b[..., -1, :])

    def step(S, inp):
        dec, kv_n = inp
        return dec[..., None] * S + kv_n, S

    s_final, s_in = lax.scan(step, s0.astype(jnp.float32), (decay, kv))
    o_inter = jnp.einsum("nbhcd,nbhde->nbhce", q_dec, s_in)
    o = (o_intra + o_inter).transpose(1, 0, 3, 2, 4).reshape(B, L, H, dv)
    return o.astype(v.dtype), s_final


def gla_bidirectional(q, k, v, la_f, la_b, s0_f, s0_b):
    o_f, s_f = gla_chunk_scan(q, k, v, la_f, s0_f)
    flip = lambda t: jnp.flip(t, axis=1)
    o_b, s_b = gla_chunk_scan(flip(q), flip(k), flip(v), flip(la_b), s0_b)
    return o_f + flip(o_b), s_f, s_b


def gqa_attend(q, k, v):
    s = jnp.einsum("bqkgd,bskd->bkgqs", q, k).astype(jnp.float32) * q.shape[-1] ** -0.5
    p = jax.nn.softmax(s, axis=-1).astype(v.dtype)
    return jnp.einsum("bkgqs,bskd->bqkgd", p, v)


def diff_attend(q, k, v, lam):
    s = jnp.einsum("bqhmd,bshmd->bhmqs", q, k).astype(jnp.float32) * q.shape[-1] ** -0.5
    p = jax.nn.softmax(s, axis=-1)
    p = p[:, :, 0] - lam * p[:, :, 1]
    return jnp.einsum("bhqs,bshd->bqhd", p.astype(v.dtype), v)


def sweep_query_blocks(attend, q):
    B, L = q.shape[:2]
    nb = L // Q_BLOCK
    qb = jnp.moveaxis(q.reshape((B, nb, Q_BLOCK) + q.shape[2:]), 1, 0)
    ob = lax.map(attend, qb)
    return jnp.moveaxis(ob, 0, 1).reshape((B, L) + ob.shape[3:])


def swiglu(h, w_gate, w_up, w_down):
    return (jax.nn.silu(h @ w_gate) * (h @ w_up)) @ w_down


def moe_swiglu(h, w_router, w_gate, w_up, w_down):
    logits = (h @ w_router).astype(jnp.float32)
    top_v, top_i = lax.top_k(logits, TOP_K)
    top_w = jax.nn.softmax(top_v, axis=-1)
    combine = jnp.sum(jax.nn.one_hot(top_i, N_EXPERTS, dtype=jnp.float32) * top_w[..., None], axis=-2)
    combine = combine.astype(h.dtype)
    out = jnp.zeros_like(h)
    for e in range(N_EXPERTS):
        out = out + combine[..., e:e + 1] * swiglu(h, w_gate[e], w_up[e], w_down[e])
    return out


def hybrid_mixer(h, hc, rows, cols, w_in, w_out, gla_a2_f, gla_ab_f, gla_a2_b, gla_ab_b, gla_norm,
                 q_norm, k_norm, lq1, lk1, lq2, lk2, diff_norm, lambda_init, need_ctx_out):
    splits = [int(s) for s in np.cumsum(IN_WIDTHS)[:-1]]
    P = jnp.split(h @ w_in, splits, axis=-1)
    Pc = jnp.split(hc @ w_in, splits, axis=-1)
    heads = lambda t, *shape: t.reshape(t.shape[:2] + shape)

    def gla_in(p):
        q = heads(p[0], GLA_HEADS, GLA_DK)
        k = heads(p[1], GLA_HEADS, GLA_DK)
        v = heads(p[2], GLA_HEADS, GLA_DV)
        la_f = heads(jax.nn.log_sigmoid(p[4] @ gla_a2_f + gla_ab_f) / GLA_GATE_NORM, GLA_HEADS, GLA_DK)
        la_b = heads(jax.nn.log_sigmoid(p[5] @ gla_a2_b + gla_ab_b) / GLA_GATE_NORM, GLA_HEADS, GLA_DK)
        return q, k, v, la_f, la_b

    zero_state = jnp.zeros((hc.shape[0], GLA_HEADS, GLA_DK, GLA_DV), jnp.float32)
    gla_c, s_f, s_b = gla_bidirectional(*gla_in(Pc), zero_state, zero_state)
    gla_l, _, _ = gla_bidirectional(*gla_in(P), s_f, s_b)
    gla_out = lambda o, r: rms_norm(o, gla_norm).reshape(r.shape) * jax.nn.silu(r)

    def gqa_in(p, rope):
        q = rms_norm(heads(p[6], GQA_KV_HEADS, GQA_GROUP, GQA_HD), q_norm)
        k = rms_norm(heads(p[7], GQA_KV_HEADS, GQA_HD), k_norm)
        v = heads(p[8], GQA_KV_HEADS, GQA_HD)
        if rope:
            q, k = axial_rope(q, rows, cols), axial_rope(k, rows, cols)
        return q, k, v

    gq, gk, gv = gqa_in(P, True)
    gqc, gkc, gvc = gqa_in(Pc, False)
    gk_all = jnp.concatenate([gk, gkc], axis=1)
    gv_all = jnp.concatenate([gv, gvc], axis=1)
    gqa_l = sweep_query_blocks(lambda qb: gqa_attend(qb, gk_all, gv_all), gq)
    gqa_l = gqa_l.reshape(gqa_l.shape[:2] + (-1,))

    lam = (jnp.exp(jnp.sum((lq1 * lk1).astype(jnp.float32)))
           - jnp.exp(jnp.sum((lq2 * lk2).astype(jnp.float32))) + lambda_init)

    def diff_in(p, rope):
        q = heads(p[9], DIFF_HEADS, 2, DIFF_QK)
        k = heads(p[10], DIFF_HEADS, 2, DIFF_QK)
        v = heads(p[11], DIFF_HEADS, DIFF_V)
        if rope:
            q, k = axial_rope(q, rows, cols), axial_rope(k, rows, cols)
        return q, k, v

    dq, dk, dv = diff_in(P, True)
    dqc, dkc, dvc = diff_in(Pc, False)
    dk_all = jnp.concatenate([dk, dkc], axis=1)
    dv_all = jnp.concatenate([dv, dvc], axis=1)
    diff_l = sweep_query_blocks(lambda qb: diff_attend(qb, dk_all, dv_all, lam), dq)
    diff_out = lambda o: (rms_norm(o, diff_norm) * (1.0 - lambda_init)).reshape(o.shape[:2] + (-1,))

    y = jnp.concatenate([gla_out(gla_l, P[3]), gqa_l, diff_out(diff_l)], axis=-1) @ w_out
    yc = None
    if need_ctx_out:
        gqa_c = gqa_attend(gqc, gkc, gvc)
        gqa_c = gqa_c.reshape(gqa_c.shape[:2] + (-1,))
        diff_c = diff_attend(dqc, dkc, dvc, lam)
        yc = jnp.concatenate([gla_out(gla_c, Pc[3]), gqa_c, diff_out(diff_c)], axis=-1) @ w_out
    return y, yc


def setup_inputs(seed: int = 0) -> dict:
    key = jax.random.key(seed)
    ks = iter(jax.random.split(key, 40))
    nrm = lambda shape, scale: jax.random.normal(next(ks), shape, jnp.float32) * scale
    gain = lambda shape: 1.0 + nrm(shape, 0.02)
    D = D_MODEL
    return {
        "x": nrm((BATCH, SEQ, D), 1.0),
        "c": nrm((BATCH, D), 1.0),
        "ctx": nrm((BATCH, CTX_LEN, D), 1.0),
        "c_ctx": nrm((D,), 1.0),
        "w_mod": nrm((DEPTH, D, 6 * D), 0.5 * D ** -0.5),
        "b_mod": nrm((DEPTH, 6 * D), 0.02),
        "norm_mix": gain((DEPTH, D)),
        "norm_ffn": gain((DEPTH, D)),
        "w_in": nrm((DEPTH, D, IN_WIDTH), D ** -0.5),
        "w_out": nrm((DEPTH, MIX_WIDTH, D), MIX_WIDTH ** -0.5),
        "gla_a2_f": nrm((DEPTH, GLA_GATE_RANK, GLA_HEADS * GLA_DK), GLA_GATE_RANK ** -0.5),
        "gla_ab_f": nrm((DEPTH, GLA_HEADS * GLA_DK), 0.1),
        "gla_a2_b": nrm((DEPTH, GLA_GATE_RANK, GLA_HEADS * GLA_DK), GLA_GATE_RANK ** -0.5),
        "gla_ab_b": nrm((DEPTH, GLA_HEADS * GLA_DK), 0.1),
        "gla_norm": gain((DEPTH, GLA_DV)),
        "gqa_q_norm": gain((DEPTH, GQA_HD)),
        "gqa_k_norm": gain((DEPTH, GQA_HD)),
        "diff_lam_q1": nrm((DEPTH, DIFF_QK), 0.1),
        "diff_lam_k1": nrm((DEPTH, DIFF_QK), 0.1),
        "diff_lam_q2": nrm((DEPTH, DIFF_QK), 0.1),
        "diff_lam_k2": nrm((DEPTH, DIFF_QK), 0.1),
        "diff_norm": gain((DEPTH, DIFF_V)),
        "ffn_gate": nrm((N_DENSE, D, D_FF), D ** -0.5),
        "ffn_up": nrm((N_DENSE, D, D_FF), D ** -0.5),
        "ffn_down": nrm((N_DENSE, D_FF, D), D_FF ** -0.5),
        "moe_router": nrm((N_MOE, D, N_EXPERTS), D ** -0.5),
        "moe_gate": nrm((N_MOE, N_EXPERTS, D, D_FF_EXPERT), D ** -0.5),
        "moe_up": nrm((N_MOE, N_EXPERTS, D, D_FF_EXPERT), D ** -0.5),
        "moe_down": nrm((N_MOE, N_EXPERTS, D_FF_EXPERT, D), D_FF_EXPERT ** -0.5),
        "norm_f": gain((D,)),
    }


def reference(x, c, ctx, c_ctx, w_mod, b_mod, norm_mix, norm_ffn, w_in, w_out,
              gla_a2_f, gla_ab_f, gla_a2_b, gla_ab_b, gla_norm, gqa_q_norm, gqa_k_norm,
              diff_lam_q1, diff_lam_k1, diff_lam_q2, diff_lam_k2, diff_norm,
              ffn_gate, ffn_up, ffn_down, moe_router, moe_gate, moe_up, moe_down, norm_f):
    L = x.shape[1]
    ROWS = L // GRID_W
    rows = jnp.repeat(jnp.arange(ROWS, dtype=jnp.float32), GRID_W)
    cols = jnp.tile(jnp.arange(GRID_W, dtype=jnp.float32), ROWS)
    xc = ctx
    for l in range(DEPTH):
        need_ctx = l < DEPTH - 1
        mod = jax.nn.silu(c) @ w_mod[l] + b_mod[l]
        mod_c = jax.nn.silu(c_ctx) @ w_mod[l] + b_mod[l]
        sh_a, sc_a, g_a, sh_f, sc_f, g_f = jnp.split(mod[:, None, :], 6, axis=-1)
        csh_a, csc_a, cg_a, csh_f, csc_f, cg_f = jnp.split(mod_c, 6, axis=-1)

        h = modulate(rms_norm(x, norm_mix[l]), sh_a, sc_a)
        hc = modulate(rms_norm(xc, norm_mix[l]), csh_a, csc_a)
        y, yc = hybrid_mixer(h, hc, rows, cols, w_in[l], w_out[l],
                             gla_a2_f[l], gla_ab_f[l], gla_a2_b[l], gla_ab_b[l], gla_norm[l],
                             gqa_q_norm[l], gqa_k_norm[l],
                             diff_lam_q1[l], diff_lam_k1[l], diff_lam_q2[l], diff_lam_k2[l], diff_norm[l],
                             0.8 - 0.6 * math.exp(-0.3 * l), need_ctx)
        x = x + g_a * y

        i = l // 2
        if l % 2 == 0:
            ffn = lambda t: swiglu(t, ffn_gate[i], ffn_up[i], ffn_down[i])
        else:
            ffn = lambda t: moe_swiglu(t, moe_router[i], moe_gate[i], moe_up[i], moe_down[i])
        x = x + g_f * ffn(modulate(rms_norm(x, norm_ffn[l]), sh_f, sc_f))
        if need_ctx:
            xc = xc + cg_a * yc
            xc = xc + cg_f * ffn(modulate(rms_norm(xc, norm_ffn[l]), csh_f, csc_f))
    return rms_norm(x, norm_f)
```

```python
import functools
import math

import jax
import jax.numpy as jnp
import numpy as np
from jax import lax
from jax.experimental import pallas as pl
from jax.experimental.pallas import tpu as pltpu

F32 = jnp.float32
BF16 = jnp.bfloat16

GRID_W = 64
GLA_HEADS, GLA_DK, GLA_DV = 4, 32, 64
GLA_GATE_RANK = 16
GLA_GATE_NORM = 16.0
GLA_CHUNK = 64
GQA_HEADS, GQA_KV_HEADS, GQA_HD = 8, 2, 64
GQA_GROUP = GQA_HEADS // GQA_KV_HEADS
DIFF_HEADS, DIFF_QK, DIFF_V = 4, 32, 64
ROPE_THETA = 10000.0
NORM_EPS = 1e-6
N_EXPERTS = 8

LANES = 128
TOKEN_BLOCK = 256
VMEM_LIMIT = 56 * 1024 * 1024

_W_GLA_QK = 0
_W_GLA_V = 256
_W_GLA_R = 512
_W_GLA_LR = 768
_W_GQA_Q = 896
_W_GQA_K = 1408
_W_GQA_V = 1536
_W_DIFF_Q = 1664
_W_DIFF_K = 1920
_W_DIFF_V = 2176
_W_IN_PAD = 2432
_W_IN_LR_END = 800


def _cparams(*sem):
    return pltpu.CompilerParams(dimension_semantics=sem, vmem_limit_bytes=VMEM_LIMIT)


def _rms(x, w):
    ms = jnp.mean(x * x, axis=-1, keepdims=True)
    return x * lax.rsqrt(ms + NORM_EPS) * w


def _silu(x):
    return x * (1.0 / (1.0 + jnp.exp(-x)))


def _nt_dot(a, b):
    return lax.dot_general(a, b, (((1,), (1,)), ((), ())), preferred_element_type=F32)


def _tn_dot(a, b):
    return lax.dot_general(a, b, (((0,), (0,)), ((), ())), preferred_element_type=F32)


def _mod_kernel(c_ref, w_ref, b_ref, o_ref):
    s = _silu(c_ref[...])
    o_ref[...] = jnp.dot(s.astype(BF16), w_ref[...].astype(BF16),
                         preferred_element_type=F32) + b_ref[...]


def _modulation(c_rows, w_mod, b_mod):
    depth, d, n = w_mod.shape
    tn = 1536
    return pl.pallas_call(
        _mod_kernel,
        grid=(depth, n // tn),
        in_specs=[pl.BlockSpec((8, d), lambda l, j: (0, 0)),
                  pl.BlockSpec((None, d, tn), lambda l, j: (l, 0, j)),
                  pl.BlockSpec((None, 1, tn), lambda l, j: (l, 0, j))],
        out_specs=pl.BlockSpec((None, 8, tn), lambda l, j: (l, 0, j)),
        out_shape=jax.ShapeDtypeStruct((depth, 8, n), F32),
        compiler_params=_cparams("parallel", "parallel"),
        name="modulation",
    )(c_rows, w_mod, b_mod.reshape(depth, 1, n))


def _rope(x, cos, sin_signed, half):
    width = x.shape[1]
    lane = lax.broadcasted_iota(jnp.int32, x.shape, 1)
    first = (lane % (2 * half)) < half
    rot = jnp.where(first, pltpu.roll(x, width - half, 1), pltpu.roll(x, half, 1))
    return x * cos + rot * sin_signed


def _in_kernel(x_ref, mod_ref, nw_ref, w_ref, g_ref, qg_ref, kg_ref, cg_ref, sg_ref, cd_ref, sd_ref,
               glaqk_ref, glav_ref, glar_ref, glalr_ref, gq_ref, gkv_ref, dq_ref, dk_ref, dv_ref):
    mod = mod_ref[...]
    h = _rms(x_ref[...], nw_ref[...]) * (1.0 + mod[1:2]) + mod[0:1]
    hb = h.astype(BF16)

    def proj(a, b):
        return jnp.dot(hb, w_ref[:, a:b], preferred_element_type=F32)

    glaqk_ref[...] = proj(_W_GLA_QK, _W_GLA_V)
    glav_ref[...] = proj(_W_GLA_V, _W_GLA_R).astype(BF16)
    glar_ref[...] = proj(_W_GLA_R, _W_GLA_LR)
    glalr_ref[...] = proj(_W_GLA_LR, _W_GQA_Q)

    cos_g = cg_ref[...]
    sin_g = sg_ref[...]
    q = proj(_W_GQA_Q, _W_GQA_K)
    ms = jnp.dot((q * q).astype(BF16), g_ref[...], preferred_element_type=F32)
    qn = q * lax.rsqrt(ms + NORM_EPS) * qg_ref[...]
    gq_ref[...] = (_rope(qn, cos_g, sin_g, GQA_HD // 4) * GQA_HD ** -0.5).astype(BF16)

    kw = GQA_KV_HEADS * GQA_HD
    k = proj(_W_GQA_K, _W_GQA_V)
    ms = jnp.dot((k * k).astype(BF16), g_ref[0:kw, 0:kw], preferred_element_type=F32)
    kn = k * lax.rsqrt(ms + NORM_EPS) * kg_ref[...]
    gkv_ref[:, 0:kw] = _rope(kn, cos_g[:, 0:kw], sin_g[:, 0:kw], GQA_HD // 4).astype(BF16)
    gkv_ref[:, kw:2 * kw] = proj(_W_GQA_V, _W_DIFF_Q).astype(BF16)

    cos_d = cd_ref[...]
    sin_d = sd_ref[...]
    dq = proj(_W_DIFF_Q, _W_DIFF_K)
    dq_ref[...] = (_rope(dq, cos_d, sin_d, DIFF_QK // 4) * DIFF_QK ** -0.5).astype(BF16)
    dk = proj(_W_DIFF_K, _W_DIFF_V)
    dk_ref[...] = _rope(dk, cos_d, sin_d, DIFF_QK // 4).astype(BF16)
    dv_ref[...] = proj(_W_DIFF_V, _W_IN_PAD).astype(BF16)


def _in_projection(x_all, modsel, norm_w, w_in_p, gmat, q_gain, k_gain, tables):
    bsz, s, d = x_all.shape
    tm = TOKEN_BLOCK
    cos_g, sin_g, cos_d, sin_d = tables
    row = lambda w: pl.BlockSpec((None, tm, w), lambda b, j: (b, j, 0))
    const = lambda shape: pl.BlockSpec(shape, lambda b, j: (0,) * len(shape))
    tab = lambda w: pl.BlockSpec((tm, w), lambda b, j: (j, 0))
    widths = [(256, F32), (256, BF16), (256, F32), (128, F32), (512, BF16), (256, BF16),
              (256, BF16), (256, BF16), (256, BF16)]
    return pl.pallas_call(
        _in_kernel,
        grid=(bsz, s // tm),
        in_specs=[row(d),
                  pl.BlockSpec((None, None, 8, d), lambda b, j: (b, jnp.minimum(j, 1), 0, 0)),
                  const((1, d)), const((d, _W_IN_PAD)), const((512, 512)),
                  const((1, 512)), const((1, 128)),
                  tab(512), tab(512), tab(256), tab(256)],
        out_specs=[row(w) for w, _ in widths],
        out_shape=[jax.ShapeDtypeStruct((bsz, s, w), dt) for w, dt in widths],
        compiler_params=_cparams("parallel", "parallel"),
        name="in_projection",
    )(x_all, modsel, norm_w, w_in_p, gmat, q_gain, k_gain, cos_g, sin_g, cos_d, sin_d)


def _split3(g):
    g1 = g.astype(BF16)
    r1 = g - g1.astype(F32)
    g2 = r1.astype(BF16)
    g3 = (r1 - g2.astype(F32)).astype(BF16)
    return jnp.concatenate([g1, g2, g3], axis=1)


def _gla_kernel(qkf_ref, vf_ref, lrf_ref, qkb_ref, vb_ref, lrb_ref, a2_ref, ab_ref,
                of_ref, ob_ref, st_ref):
    @pl.when(pl.program_id(0) == 0)
    def _():
        st_ref[...] = jnp.zeros_like(st_ref)

    n_b = qkf_ref.shape[0]
    c = GLA_CHUNK
    hk = GLA_HEADS * GLA_DK
    r_i = lax.broadcasted_iota(jnp.int32, (c, c), 0)
    c_i = lax.broadcasted_iota(jnp.int32, (c, c), 1)
    dirs = ((qkf_ref, vf_ref, lrf_ref, of_ref, c_i <= r_i, c - 1),
            (qkb_ref, vb_ref, lrb_ref, ob_ref, c_i >= r_i, 0))
    for d, (qk_ref, v_ref, lr_ref, o_ref, tri, last) in enumerate(dirs):
        tri_b = jnp.where(tri, 1.0, 0.0).astype(BF16)
        for b in range(n_b):
            qk = qk_ref[b]
            pre = jnp.dot(lr_ref[b].astype(BF16), a2_ref[d], preferred_element_type=F32) + ab_ref[d]
            log_a = (jnp.minimum(pre, 0.0) - jnp.log(1.0 + jnp.exp(-jnp.abs(pre)))) * (1.0 / GLA_GATE_NORM)
            cs = jnp.dot(tri_b, _split3(log_a), preferred_element_type=F32)
            bc = cs[:, 0:hk] + cs[:, hk:2 * hk] + cs[:, 2 * hk:3 * hk]
            tot = bc[last:last + 1, :]
            q = qk[:, 0:hk] * GLA_DK ** -0.5
            k = qk[:, hk:2 * hk]
            qd = (q * jnp.exp(bc)).astype(BF16)
            kd = (k * jnp.exp(-bc)).astype(BF16)
            kl = (k * jnp.exp(tot - bc)).astype(BF16)
            v = v_ref[b]
            st = st_ref[d, b]
            outs, kvs = [], []
            for h in range(GLA_HEADS):
                ks = slice(h * GLA_DK, (h + 1) * GLA_DK)
                v_h = v[:, h * GLA_DV:(h + 1) * GLA_DV]
                a = jnp.where(tri, _nt_dot(qd[:, ks], kd[:, ks]), 0.0).astype(BF16)
                outs.append(jnp.dot(a, v_h, preferred_element_type=F32)
                            + _nt_dot(qd[:, ks], st[:, ks].astype(BF16)))
                kvs.append(_tn_dot(v_h, kl[:, ks]))
            o_ref[b] = jnp.concatenate(outs, axis=1)
            st_ref[d, b] = st * jnp.exp(tot) + jnp.concatenate(kvs, axis=1)


def _gla(gla_qk, gla_v, gla_lr, a2, ab, n_ctx_chunks):
    bsz, s, _ = gla_qk.shape
    c = GLA_CHUNK
    n = s // c
    fwd = lambda i: (0, i, 0)
    bwd = lambda i: (0, jnp.where(i < n_ctx_chunks, n_ctx_chunks - 1 - i, n + n_ctx_chunks - 1 - i), 0)
    blk = lambda w, m: pl.BlockSpec((bsz, c, w), m)
    hv = GLA_HEADS * GLA_DV
    return pl.pallas_call(
        _gla_kernel,
        grid=(n,),
        in_specs=[blk(256, fwd), blk(hv, fwd), blk(LANES, fwd),
                  blk(256, bwd), blk(hv, bwd), blk(LANES, bwd),
                  pl.BlockSpec((2, LANES, LANES), lambda i: (0, 0, 0)),
                  pl.BlockSpec((2, 1, LANES), lambda i: (0, 0, 0))],
        out_specs=[blk(hv, fwd), blk(hv, bwd)],
        out_shape=[jax.ShapeDtypeStruct((bsz, s, hv), F32)] * 2,
        scratch_shapes=[pltpu.VMEM((2, bsz, GLA_DV, GLA_HEADS * GLA_DK), F32)],
        compiler_params=_cparams("arbitrary"),
        name="gla_scan",
    )(gla_qk, gla_v, gla_lr, gla_qk, gla_v, gla_lr, a2, ab)


def _softmax_parts(q, k):
    s = _nt_dot(q, k)
    p = jnp.exp(s - jnp.max(s, axis=-1, keepdims=True))
    return p.astype(BF16), 1.0 / jnp.sum(p, axis=-1, keepdims=True)


def _attn_kernel(gq_ref, gkv_ref, dq_ref, dk_ref, dv_ref, lam_ref, dnw_ref, go_ref, do_ref,
                 *, n_ctx, out_scale, first_block):
    lam = (jnp.exp(jnp.sum(lam_ref[0:1, :] * lam_ref[1:2, :], axis=-1, keepdims=True))
           - jnp.exp(jnp.sum(lam_ref[2:3, :] * lam_ref[3:4, :], axis=-1, keepdims=True))
           + (1.0 - out_scale))
    kw = GQA_KV_HEADS * GQA_HD

    def attend(nk):
        for kh in range(GQA_KV_HEADS):
            k = gkv_ref[0:nk, kh * GQA_HD:(kh + 1) * GQA_HD]
            v = gkv_ref[0:nk, kw + kh * GQA_HD:kw + (kh + 1) * GQA_HD]
            for g in range(GQA_GROUP):
                hs = slice((kh * GQA_GROUP + g) * GQA_HD, (kh * GQA_GROUP + g + 1) * GQA_HD)
                p, inv = _softmax_parts(gq_ref[:, hs], k)
                go_ref[:, hs] = (jnp.dot(p, v, preferred_element_type=F32) * inv).astype(BF16)
        for h in range(DIFF_HEADS):
            base = h * 2 * DIFF_QK
            v = dv_ref[0:nk, h * DIFF_V:(h + 1) * DIFF_V]
            p1, inv1 = _softmax_parts(dq_ref[:, base:base + DIFF_QK], dk_ref[0:nk, base:base + DIFF_QK])
            p2, inv2 = _softmax_parts(dq_ref[:, base + DIFF_QK:base + 2 * DIFF_QK],
                                      dk_ref[0:nk, base + DIFF_QK:base + 2 * DIFF_QK])
            o = (jnp.dot(p1, v, preferred_element_type=F32) * inv1
                 - jnp.dot(p2, v, preferred_element_type=F32) * (inv2 * lam))
            do_ref[:, h * DIFF_V:(h + 1) * DIFF_V] = (_rms(o, dnw_ref[...]) * out_scale).astype(BF16)

    if first_block == 0:
        @pl.when(pl.program_id(1) == 0)
        def _():
            attend(n_ctx)

        @pl.when(pl.program_id(1) > 0)
        def _():
            attend(gkv_ref.shape[0])
    else:
        attend(gkv_ref.shape[0])


def _attention(gq, gkv, dq, dk, dv, lam_rows, diff_norm_w, n_ctx, lambda_init, ctx_out):
    bsz, s, _ = gq.shape
    tq = TOKEN_BLOCK
    assert n_ctx == tq
    first = 0 if ctx_out else 1
    s_out = s - first * tq
    qrow = lambda w: pl.BlockSpec((None, tq, w), lambda b, j: (b, j + first, 0))
    orow = lambda w: pl.BlockSpec((None, tq, w), lambda b, j: (b, j, 0))
    full = lambda w: pl.BlockSpec((None, s, w), lambda b, j: (b, 0, 0))
    return pl.pallas_call(
        functools.partial(_attn_kernel, n_ctx=n_ctx, out_scale=1.0 - lambda_init, first_block=first),
        grid=(bsz, s_out // tq),
        in_specs=[qrow(512), full(256), qrow(256), full(256), full(256),
                  pl.BlockSpec((4, DIFF_QK), lambda b, j: (0, 0)),
                  pl.BlockSpec((1, DIFF_V), lambda b, j: (0, 0))],
        out_specs=[orow(512), orow(256)],
        out_shape=[jax.ShapeDtypeStruct((bsz, s_out, 512), BF16),
                   jax.ShapeDtypeStruct((bsz, s_out, 256), BF16)],
        compiler_params=_cparams("parallel", "parallel"),
        name="attention",
    )(gq, gkv, dq, dk, dv, lam_rows, diff_norm_w)


def _route(h, wr_ref):
    logits = [jnp.sum(h * wr_ref[e:e + 1, :], axis=-1, keepdims=True) for e in range(N_EXPERTS)]

    def top(ls):
        m = functools.reduce(jnp.maximum, ls)
        idx = jnp.full_like(m, float(N_EXPERTS))
        for e in reversed(range(N_EXPERTS)):
            idx = jnp.where(ls[e] == m, float(e), idx)
        return m, idx

    m1, i1 = top(logits)
    m2, i2 = top([jnp.where(i1 == float(e), -jnp.inf, logits[e]) for e in range(N_EXPERTS)])
    e2 = jnp.exp(m2 - m1)
    w1 = 1.0 / (1.0 + e2)
    w2 = e2 * w1
    lane = lax.broadcasted_iota(jnp.int32, (h.shape[0], LANES), 1)
    return jnp.where(lane == 0, w1, jnp.where(lane == 1, w2, jnp.where(lane == 2, i1,
                     jnp.where(lane == 3, i2, 0.0))))


def _out_kernel(*refs, with_router):
    if with_router:
        (x_ref, mod_ref, of_ref, ob_ref, r_ref, go_ref, do_ref, w_ref, g_ref, gnw_ref, nw_ref, wr_ref,
         x1_ref, h2_ref, route_ref) = refs
    else:
        (x_ref, mod_ref, of_ref, ob_ref, r_ref, go_ref, do_ref, w_ref, g_ref, gnw_ref, nw_ref,
         x1_ref, h2_ref) = refs
    mod = mod_ref[...]
    hv = GLA_HEADS * GLA_DV
    gw = GQA_HEADS * GQA_HD
    o = of_ref[...] + ob_ref[...]
    ms = jnp.dot((o * o).astype(BF16), g_ref[...], preferred_element_type=F32)
    gla = o * lax.rsqrt(ms + NORM_EPS) * gnw_ref[...] * _silu(r_ref[...])
    y = (jnp.dot(gla.astype(BF16), w_ref[0:hv, :], preferred_element_type=F32)
         + jnp.dot(go_ref[...], w_ref[hv:hv + gw, :], preferred_element_type=F32)
         + jnp.dot(do_ref[...], w_ref[hv + gw:, :], preferred_element_type=F32))
    x1 = x_ref[...] + mod[2:3] * y
    x1_ref[...] = x1
    h2 = _rms(x1, nw_ref[...]) * (1.0 + mod[4:5]) + mod[3:4]
    h2_ref[...] = h2.astype(BF16)
    if with_router:
        route_ref[...] = _route(h2, wr_ref)


def _out_projection(x_all, modsel, o_f, o_b, gla_r, gqa_o, diff_o, w_out, gmat, gla_norm_w, norm_ffn_w,
                    router_t, ctx_out):
    bsz, s, d = x_all.shape
    tm = TOKEN_BLOCK
    first = 0 if ctx_out else 1
    s_out = s - first * tm
    arow = lambda w: pl.BlockSpec((None, tm, w), lambda b, j: (b, j + first, 0))
    orow = lambda w: pl.BlockSpec((None, tm, w), lambda b, j: (b, j, 0))
    const = lambda shape: pl.BlockSpec(shape, lambda b, j: (0,) * len(shape))
    with_router = router_t is not None
    in_specs = [arow(d),
                pl.BlockSpec((None, None, 8, d), lambda b, j: (b, jnp.minimum(j + first, 1), 0, 0)),
                arow(256), arow(256), arow(256), orow(512), orow(256),
                const((d, d)), const((256, 256)), const((1, 256)), const((1, d))]
    args = [x_all, modsel, o_f, o_b, gla_r, gqa_o, diff_o, w_out, gmat, gla_norm_w, norm_ffn_w]
    out_specs = [orow(d), orow(d)]
    out_shape = [jax.ShapeDtypeStruct((bsz, s_out, d), F32), jax.ShapeDtypeStruct((bsz, s_out, d), BF16)]
    if with_router:
        in_specs.append(const((N_EXPERTS, d)))
        args.append(router_t)
        out_specs.append(orow(LANES))
        out_shape.append(jax.ShapeDtypeStruct((bsz, s_out, LANES), F32))
    return pl.pallas_call(
        functools.partial(_out_kernel, with_router=with_router),
        grid=(bsz, s_out // tm),
        in_specs=in_specs, out_specs=out_specs, out_shape=out_shape,
        compiler_params=_cparams("parallel", "parallel"),
        name="out_projection",
    )(*args)


def _ffn_kernel(h_ref, x_ref, mod_ref, wg_ref, wu_ref, wd_ref, o_ref, acc_ref, *, n_ctx):
    f = pl.program_id(2)

    @pl.when(f == 0)
    def _():
        acc_ref[...] = jnp.zeros_like(acc_ref)

    h = h_ref[...]
    g = jnp.dot(h, wg_ref[...], preferred_element_type=F32)
    u = jnp.dot(h, wu_ref[...], preferred_element_type=F32)
    acc_ref[...] += jnp.dot((_silu(g) * u).astype(BF16), wd_ref[...], preferred_element_type=F32)

    @pl.when(f == pl.num_programs(2) - 1)
    def _():
        tm = h_ref.shape[0]
        row = pl.program_id(1) * tm + lax.broadcasted_iota(jnp.int32, (tm, 1), 0)
        gate = jnp.where(row < n_ctx, mod_ref[0, 5:6, :], mod_ref[1, 5:6, :])
        o_ref[...] = x_ref[...] + gate * acc_ref[...]


def _dense_ffn(h2, x1, modsel, wg, wu, wd, n_ctx):
    bsz, s, d = x1.shape
    ff = wg.shape[1]
    tm = s // 4
    tf = 256
    row = lambda: pl.BlockSpec((None, tm, d), lambda b, t, f: (b, t, 0))
    return pl.pallas_call(
        functools.partial(_ffn_kernel, n_ctx=n_ctx),
        grid=(bsz, s // tm, ff // tf),
        in_specs=[row(), row(),
                  pl.BlockSpec((None, 2, 8, d), lambda b, t, f: (b, 0, 0, 0)),
                  pl.BlockSpec((d, tf), lambda b, t, f: (0, f)),
                  pl.BlockSpec((d, tf), lambda b, t, f: (0, f)),
                  pl.BlockSpec((tf, d), lambda b, t, f: (f, 0))],
        out_specs=row(),
        out_shape=jax.ShapeDtypeStruct((bsz, s, d), F32),
        scratch_shapes=[pltpu.VMEM((tm, d), F32)],
        compiler_params=_cparams("parallel", "parallel", "arbitrary"),
        name="dense_ffn",
    )(h2, x1, modsel, wg, wu, wd)


MOE_TILE = 512


def _moe_kernel(te_ref, nv_ref, xs_ref, wg_ref, wu_ref, wd_ref, o_ref, acc_ref):
    i = pl.program_id(0)
    f = pl.program_id(1)

    @pl.when(i < nv_ref[0])
    def _():
        @pl.when(f == 0)
        def _():
            acc_ref[...] = jnp.zeros_like(acc_ref)

        h = xs_ref[...]
        g = jnp.dot(h, wg_ref[...], preferred_element_type=F32)
        u = jnp.dot(h, wu_ref[...], preferred_element_type=F32)
        acc_ref[...] += jnp.dot((_silu(g) * u).astype(BF16), wd_ref[...], preferred_element_type=F32)

        @pl.when(f == pl.num_programs(1) - 1)
        def _():
            o_ref[...] = acc_ref[...]


def _moe_experts(tile_expert, n_valid, xs, wg, wu, wd):
    rows, d = xs.shape
    ff = wg.shape[2]
    tm = MOE_TILE
    tf = 512
    nf = ff // tf

    def tile(i, nv):
        return jnp.minimum(i, nv[0] - 1)

    def fidx(i, f, nv):
        return jnp.where(i < nv[0], f, nf - 1)

    return pl.pallas_call(
        _moe_kernel,
        grid_spec=pltpu.PrefetchScalarGridSpec(
            num_scalar_prefetch=2,
            grid=(rows // tm, nf),
            in_specs=[pl.BlockSpec((tm, d), lambda i, f, te, nv: (tile(i, nv), 0)),
                      pl.BlockSpec((None, d, tf), lambda i, f, te, nv: (te[tile(i, nv)], 0, fidx(i, f, nv))),
                      pl.BlockSpec((None, d, tf), lambda i, f, te, nv: (te[tile(i, nv)], 0, fidx(i, f, nv))),
                      pl.BlockSpec((None, tf, d), lambda i, f, te, nv: (te[tile(i, nv)], fidx(i, f, nv), 0))],
            out_specs=pl.BlockSpec((tm, d), lambda i, f, te, nv: (tile(i, nv), 0)),
            scratch_shapes=[pltpu.VMEM((tm, d), F32)]),
        out_shape=jax.ShapeDtypeStruct((rows, d), F32),
        compiler_params=_cparams("arbitrary", "arbitrary"),
        name="moe_experts",
    )(tile_expert, n_valid, xs, wg, wu, wd)


def _combine_kernel(x_ref, mod_ref, route_ref, ya_ref, yb_ref, nf_ref, o_ref):
    route = route_ref[...]
    moe = route[:, 0:1] * ya_ref[...] + route[:, 1:2] * yb_ref[...]
    x2 = x_ref[...] + mod_ref[5:6, :] * moe
    o_ref[...] = _rms(x2, nf_ref[...])


def _moe_combine(x1, modsel, route, ya, yb, norm_f_w):
    bsz, s, d = x1.shape
    tm = TOKEN_BLOCK
    row = lambda w: pl.BlockSpec((None, tm, w), lambda b, j: (b, j, 0))
    return pl.pallas_call(
        _combine_kernel,
        grid=(bsz, s // tm),
        in_specs=[row(d), pl.BlockSpec((None, None, 8, d), lambda b, j: (b, 1, 0, 0)),
                  row(LANES), row(d), row(d), pl.BlockSpec((1, d), lambda b, j: (0, 0))],
        out_specs=row(d),
        out_shape=jax.ShapeDtypeStruct((bsz, s, d), F32),
        compiler_params=_cparams("parallel", "parallel"),
        name="moe_combine",
    )(x1, modsel, route, ya, yb, norm_f_w)


def _final_norm_kernel(x_ref, nf_ref, o_ref):
    o_ref[...] = _rms(x_ref[...], nf_ref[...])


def _final_norm(x, norm_f_w, first_row):
    bsz, s, d = x.shape
    tm = TOKEN_BLOCK
    first = first_row // tm
    return pl.pallas_call(
        _final_norm_kernel,
        grid=(bsz, (s - first_row) // tm),
        in_specs=[pl.BlockSpec((None, tm, d), lambda b, j: (b, j + first, 0)),
                  pl.BlockSpec((1, d), lambda b, j: (0, 0))],
        out_specs=pl.BlockSpec((None, tm, d), lambda b, j: (b, j, 0)),
        out_shape=jax.ShapeDtypeStruct((bsz, s - first_row, d), F32),
        compiler_params=_cparams("parallel", "parallel"),
        name="final_norm",
    )(x, norm_f_w)


def _moe_plan(route, tm):
    n = route.shape[0]
    e_flat = route[:, 2:4].astype(jnp.int32).reshape(-1)
    onehot = (e_flat[:, None] == jnp.arange(N_EXPERTS, dtype=jnp.int32)[None, :]).astype(jnp.int32)
    csum = jnp.cumsum(onehot, axis=0)
    rank = jnp.take_along_axis(csum, e_flat[:, None], axis=1)[:, 0] - 1
    counts = csum[-1]
    padded = ((counts + tm - 1) // tm) * tm
    ends = jnp.cumsum(padded)
    pos = (ends - padded)[e_flat] + rank
    rows = 2 * n + N_EXPERTS * tm
    n_tiles = rows // tm
    src = jnp.zeros((rows,), jnp.int32).at[pos].set(jnp.arange(2 * n, dtype=jnp.int32) // 2)
    tile_expert = jnp.minimum(
        jnp.searchsorted(ends, jnp.arange(n_tiles, dtype=jnp.int32) * tm, side="right"),
        N_EXPERTS - 1).astype(jnp.int32)
    n_valid = (ends[-1] // tm).astype(jnp.int32).reshape(1)
    return src, pos.reshape(n, 2), tile_expert, n_valid


def _rope_tables(n_ctx, seq, head_dim, copies):
    half = head_dim // 4
    t = np.arange(seq)
    freqs = ROPE_THETA ** (-jnp.arange(half, dtype=F32) / half)
    rows = jnp.asarray(t // GRID_W, F32)[:, None] * freqs[None, :]
    cols = jnp.asarray(t % GRID_W, F32)[:, None] * freqs[None, :]
    cos = jnp.concatenate([jnp.cos(rows)] * 2 + [jnp.cos(cols)] * 2, axis=1)
    sin = jnp.concatenate([-jnp.sin(rows), jnp.sin(rows), -jnp.sin(cols), jnp.sin(cols)], axis=1)
    cos = jnp.concatenate([jnp.ones((n_ctx, head_dim), F32), cos], axis=0)
    sin = jnp.concatenate([jnp.zeros((n_ctx, head_dim), F32), sin], axis=0)
    return jnp.tile(cos, (1, copies)), jnp.tile(sin, (1, copies))


def _group_mean_matrix(width, group):
    g = np.kron(np.eye(width // group, dtype=np.float32), np.ones((group, group), np.float32)) / group
    return jnp.asarray(g, BF16)


def kernel(x, c, ctx, c_ctx, w_mod, b_mod, norm_mix, norm_ffn, w_in, w_out, gla_a2_f, gla_ab_f, gla_a2_b,
           gla_ab_b, gla_norm, gqa_q_norm, gqa_k_norm, diff_lam_q1, diff_lam_k1, diff_lam_q2, diff_lam_k2,
           diff_norm, ffn_gate, ffn_up, ffn_down, moe_router, moe_gate, moe_up, moe_down, norm_f):
    bsz, seq, d = x.shape
    n_ctx = ctx.shape[1]
    depth = w_mod.shape[0]

    c_rows = jnp.concatenate([c, c_ctx[None, :], jnp.zeros((8 - bsz - 1, d), F32)], axis=0)
    mod = _modulation(c_rows, w_mod, b_mod).reshape(depth, 8, 6, d)
    mod = jnp.pad(mod, ((0, 0), (0, 0), (0, 2), (0, 0)))
    modsel = jnp.stack([jnp.broadcast_to(mod[:, bsz][:, None], (depth, bsz, 8, d)), mod[:, :bsz]], axis=2)

    tables = (_rope_tables(n_ctx, seq, GQA_HD, GQA_HEADS) + _rope_tables(n_ctx, seq, DIFF_QK, 2 * DIFF_HEADS))
    gmat = _group_mean_matrix(GQA_HEADS * GQA_HD, GQA_HD)
    gmat_v = _group_mean_matrix(GLA_HEADS * GLA_DV, GLA_DV)

    w_in_p = jnp.concatenate([w_in[:, :, :_W_IN_LR_END],
                              jnp.zeros((depth, d, _W_GQA_Q - _W_IN_LR_END), F32),
                              w_in[:, :, _W_IN_LR_END:]], axis=2).astype(BF16)
    w_out_b = w_out.astype(BF16)
    a2 = jnp.zeros((depth, 2, LANES, LANES), F32)
    a2 = a2.at[:, 0, 0:GLA_GATE_RANK].set(gla_a2_f).at[:, 1, GLA_GATE_RANK:2 * GLA_GATE_RANK].set(gla_a2_b)
    a2 = a2.astype(BF16)
    ab = jnp.stack([gla_ab_f, gla_ab_b], axis=1)[:, :, None, :]

    x_all = jnp.concatenate([ctx, x], axis=1)
    out = None
    for l in range(depth):
        ctx_out = l < depth - 1
        lambda_init = 0.8 - 0.6 * math.exp(-0.3 * l)
        gla_qk, gla_v, gla_r, gla_lr, gq, gkv, dq, dk, dv = _in_projection(
            x_all, modsel[l], norm_mix[l][None, :], w_in_p[l], gmat,
            jnp.tile(gqa_q_norm[l], GQA_HEADS)[None, :], jnp.tile(gqa_k_norm[l], GQA_KV_HEADS)[None, :], tables)
        o_f, o_b = _gla(gla_qk, gla_v, gla_lr, a2[l], ab[l], n_ctx // GLA_CHUNK)
        lam_rows = jnp.stack([diff_lam_q1[l], diff_lam_k1[l], diff_lam_q2[l], diff_lam_k2[l]], axis=0)
        gqa_o, diff_o = _attention(gq, gkv, dq, dk, dv, lam_rows, diff_norm[l][None, :], n_ctx, lambda_init,
                                   ctx_out)
        moe_layer = l % 2 == 1
        i = l // 2
        res = _out_projection(x_all, modsel[l], o_f, o_b, gla_r, gqa_o, diff_o, w_out_b[l], gmat_v,
                              jnp.tile(gla_norm[l], GLA_HEADS)[None, :], norm_ffn[l][None, :],
                              moe_router[i].T if moe_layer else None, ctx_out)
        if not moe_layer:
            x1, h2 = res
            if not ctx_out:
                raise NotImplementedError("dense feed-forward on the last layer")
            x_all = _dense_ffn(h2, x1, modsel[l], ffn_gate[i].astype(BF16), ffn_up[i].astype(BF16),
                               ffn_down[i].astype(BF16), n_ctx)
            if l == depth - 1:
                out = _final_norm(x_all, norm_f[None, :], n_ctx)
        else:
            x1, h2, route = res
            if ctx_out or l != depth - 1:
                raise NotImplementedError("expert feed-forward before the last layer")
            n = bsz * seq
            src, pos, tile_expert, n_valid = _moe_plan(route.reshape(n, LANES), MOE_TILE)
            xs = jnp.take(h2.reshape(n, d), src, axis=0)
            ys = _moe_experts(tile_expert, n_valid, xs, moe_gate[i].astype(BF16), moe_up[i].astype(BF16),
                              moe_down[i].astype(BF16))
            ya = jnp.take(ys, pos[:, 0], axis=0).reshape(bsz, seq, d)
            yb = jnp.take(ys, pos[:, 1], axis=0).reshape(bsz, seq, d)
            out = _moe_combine(x1, modsel[l], route, ya, yb, norm_f[None, :])
    return out
```

```python
import functools
import math

import jax
import jax.numpy as jnp
import numpy as np
from jax import lax
from jax.experimental import pallas as pl
from jax.experimental.pallas import tpu as pltpu

F32 = jnp.float32
BF16 = jnp.bfloat16

GRID_W = 64
GLA_HEADS, GLA_DK, GLA_DV = 4, 32, 64
GLA_GATE_RANK = 16
GLA_GATE_NORM = 16.0
GLA_CHUNK = 64
GQA_HEADS, GQA_KV_HEADS, GQA_HD = 8, 2, 64
GQA_GROUP = GQA_HEADS // GQA_KV_HEADS
DIFF_HEADS, DIFF_QK, DIFF_V = 4, 32, 64
ROPE_THETA = 10000.0
NORM_EPS = 1e-6
N_EXPERTS = 8

LANES = 128
TOKEN_BLOCK = 256
VMEM_LIMIT = 56 * 1024 * 1024

_W_GLA_QK = 0
_W_GLA_V = 256
_W_GLA_R = 512
_W_GLA_LR = 768
_W_GQA_Q = 896
_W_GQA_K = 1408
_W_GQA_V = 1536
_W_DIFF_Q = 1664
_W_DIFF_K = 1920
_W_DIFF_V = 2176
_W_IN_PAD = 2432
_W_IN_LR_END = 800


def _cparams(*sem):
    return pltpu.CompilerParams(dimension_semantics=sem, vmem_limit_bytes=VMEM_LIMIT)


def _rms(x, w):
    ms = jnp.mean(x * x, axis=-1, keepdims=True)
    return x * lax.rsqrt(ms + NORM_EPS) * w


def _silu(x):
    return x * (1.0 / (1.0 + jnp.exp(-x)))


def _nt_dot(a, b):
    return lax.dot_general(a, b, (((1,), (1,)), ((), ())), preferred_element_type=F32)


def _tn_dot(a, b):
    return lax.dot_general(a, b, (((0,), (0,)), ((), ())), preferred_element_type=F32)


def _mod_kernel(c_ref, w_ref, b_ref, o_ref):
    s = _silu(c_ref[...])
    o_ref[...] = jnp.dot(s.astype(BF16), w_ref[...].astype(BF16),
                         preferred_element_type=F32) + b_ref[...]


def _modulation(c_rows, w_mod, b_mod):
    depth, d, n = w_mod.shape
    tn = 1536
    return pl.pallas_call(
        _mod_kernel,
        grid=(depth, n // tn),
        in_specs=[pl.BlockSpec((8, d), lambda l, j: (0, 0)),
                  pl.BlockSpec((None, d, tn), lambda l, j: (l, 0, j)),
                  pl.BlockSpec((None, 1, tn), lambda l, j: (l, 0, j))],
        out_specs=pl.BlockSpec((None, 8, tn), lambda l, j: (l, 0, j)),
        out_shape=jax.ShapeDtypeStruct((depth, 8, n), F32),
        compiler_params=_cparams("parallel", "parallel"),
        name="modulation",
    )(c_rows, w_mod, b_mod.reshape(depth, 1, n))


def _rope(x, cos, sin_signed, half):
    width = x.shape[1]
    lane = lax.broadcasted_iota(jnp.int32, x.shape, 1)
    first = (lane % (2 * half)) < half
    rot = jnp.where(first, pltpu.roll(x, width - half, 1), pltpu.roll(x, half, 1))
    return x * cos + rot * sin_signed


def _in_kernel(x_ref, mod_ref, nw_ref, w_ref, g_ref, qg_ref, kg_ref, cg_ref, sg_ref, cd_ref, sd_ref,
               glaqk_ref, glav_ref, glar_ref, glalr_ref, gq_ref, gkv_ref, dq_ref, dk_ref, dv_ref):
    mod = mod_ref[...]
    h = _rms(x_ref[...], nw_ref[...]) * (1.0 + mod[1:2]) + mod[0:1]
    hb = h.astype(BF16)

    def proj(a, b):
        return jnp.dot(hb, w_ref[:, a:b], preferred_element_type=F32)

    glaqk_ref[...] = proj(_W_GLA_QK, _W_GLA_V)
    glav_ref[...] = proj(_W_GLA_V, _W_GLA_R).astype(BF16)
    glar_ref[...] = proj(_W_GLA_R, _W_GLA_LR)
    glalr_ref[...] = proj(_W_GLA_LR, _W_GQA_Q)

    cos_g = cg_ref[...]
    sin_g = sg_ref[...]
    q = proj(_W_GQA_Q, _W_GQA_K)
    ms = jnp.dot((q * q).astype(BF16), g_ref[...], preferred_element_type=F32)
    qn = q * lax.rsqrt(ms + NORM_EPS) * qg_ref[...]
    gq_ref[...] = (_rope(qn, cos_g, sin_g, GQA_HD // 4) * GQA_HD ** -0.5).astype(BF16)

    kw = GQA_KV_HEADS * GQA_HD
    k = proj(_W_GQA_K, _W_GQA_V)
    ms = jnp.dot((k * k).astype(BF16), g_ref[0:kw, 0:kw], preferred_element_type=F32)
    kn = k * lax.rsqrt(ms + NORM_EPS) * kg_ref[...]
    gkv_ref[:, 0:kw] = _rope(kn, cos_g[:, 0:kw], sin_g[:, 0:kw], GQA_HD // 4).astype(BF16)
    gkv_ref[:, kw:2 * kw] = proj(_W_GQA_V, _W_DIFF_Q).astype(BF16)

    cos_d = cd_ref[...]
    sin_d = sd_ref[...]
    dq = proj(_W_DIFF_Q, _W_DIFF_K)
    dq_ref[...] = (_rope(dq, cos_d, sin_d, DIFF_QK // 4) * DIFF_QK ** -0.5).astype(BF16)
    dk = proj(_W_DIFF_K, _W_DIFF_V)
    dk_ref[...] = _rope(dk, cos_d, sin_d, DIFF_QK // 4).astype(BF16)
    dv_ref[...] = proj(_W_DIFF_V, _W_IN_PAD).astype(BF16)


def _in_projection(x_all, modsel, norm_w, w_in_p, gmat, q_gain, k_gain, tables):
    bsz, s, d = x_all.shape
    tm = TOKEN_BLOCK
    cos_g, sin_g, cos_d, sin_d = tables
    row = lambda w: pl.BlockSpec((None, tm, w), lambda b, j: (b, j, 0))
    const = lambda shape: pl.BlockSpec(shape, lambda b, j: (0,) * len(shape))
    tab = lambda w: pl.BlockSpec((tm, w), lambda b, j: (j, 0))
    widths = [(256, F32), (256, BF16), (256, F32), (128, F32), (512, BF16), (256, BF16),
              (256, BF16), (256, BF16), (256, BF16)]
    return pl.pallas_call(
        _in_kernel,
        grid=(bsz, s // tm),
        in_specs=[row(d),
                  pl.BlockSpec((None, None, 8, d), lambda b, j: (b, jnp.minimum(j, 1), 0, 0)),
                  const((1, d)), const((d, _W_IN_PAD)), const((512, 512)),
                  const((1, 512)), const((1, 128)),
                  tab(512), tab(512), tab(256), tab(256)],
        out_specs=[row(w) for w, _ in widths],
        out_shape=[jax.ShapeDtypeStruct((bsz, s, w), dt) for w, dt in widths],
        compiler_params=_cparams("parallel", "parallel"),
        name="in_projection",
    )(x_all, modsel, norm_w, w_in_p, gmat, q_gain, k_gain, cos_g, sin_g, cos_d, sin_d)


def _split3(g):
    g1 = g.astype(BF16)
    r1 = g - g1.astype(F32)
    g2 = r1.astype(BF16)
    g3 = (r1 - g2.astype(F32)).astype(BF16)
    return jnp.concatenate([g1, g2, g3], axis=1)


def _gla_kernel(qkf_ref, vf_ref, lrf_ref, qkb_ref, vb_ref, lrb_ref, a2_ref, ab_ref,
                of_ref, ob_ref, st_ref):
    @pl.when(pl.program_id(0) == 0)
    def _():
        st_ref[...] = jnp.zeros_like(st_ref)

    n_b = qkf_ref.shape[0]
    c = GLA_CHUNK
    hk = GLA_HEADS * GLA_DK
    r_i = lax.broadcasted_iota(jnp.int32, (c, c), 0)
    c_i = lax.broadcasted_iota(jnp.int32, (c, c), 1)
    dirs = ((qkf_ref, vf_ref, lrf_ref, of_ref, c_i <= r_i, c - 1),
            (qkb_ref, vb_ref, lrb_ref, ob_ref, c_i >= r_i, 0))
    for d, (qk_ref, v_ref, lr_ref, o_ref, tri, last) in enumerate(dirs):
        tri_b = jnp.where(tri, 1.0, 0.0).astype(BF16)
        for b in range(n_b):
            qk = qk_ref[b]
            pre = jnp.dot(lr_ref[b].astype(BF16), a2_ref[d], preferred_element_type=F32) + ab_ref[d]
            log_a = (jnp.minimum(pre, 0.0) - jnp.log(1.0 + jnp.exp(-jnp.abs(pre)))) * (1.0 / GLA_GATE_NORM)
            cs = jnp.dot(tri_b, _split3(log_a), preferred_element_type=F32)
            bc = cs[:, 0:hk] + cs[:, hk:2 * hk] + cs[:, 2 * hk:3 * hk]
            tot = bc[last:last + 1, :]
            q = qk[:, 0:hk] * GLA_DK ** -0.5
            k = qk[:, hk:2 * hk]
            qd = (q * jnp.exp(bc)).astype(BF16)
            kd = (k * jnp.exp(-bc)).astype(BF16)
            kl = (k * jnp.exp(tot - bc)).astype(BF16)
            v = v_ref[b]
            st = st_ref[d, b]
            outs, kvs = [], []
            for h in range(GLA_HEADS):
                ks = slice(h * GLA_DK, (h + 1) * GLA_DK)
                v_h = v[:, h * GLA_DV:(h + 1) * GLA_DV]
                a = jnp.where(tri, _nt_dot(qd[:, ks], kd[:, ks]), 0.0).astype(BF16)
                outs.append(jnp.dot(a, v_h, preferred_element_type=F32)
                            + _nt_dot(qd[:, ks], st[:, ks].astype(BF16)))
                kvs.append(_tn_dot(v_h, kl[:, ks]))
            o_ref[b] = jnp.concatenate(outs, axis=1)
            st_ref[d, b] = st * jnp.exp(tot) + jnp.concatenate(kvs, axis=1)


def _gla(gla_qk, gla_v, gla_lr, a2, ab, n_ctx_chunks):
    bsz, s, _ = gla_qk.shape
    c = GLA_CHUNK
    n = s // c
    fwd = lambda i: (0, i, 0)
    bwd = lambda i: (0, jnp.where(i < n_ctx_chunks, n_ctx_chunks - 1 - i, n + n_ctx_chunks - 1 - i), 0)
    blk = lambda w, m: pl.BlockSpec((bsz, c, w), m)
    hv = GLA_HEADS * GLA_DV
    return pl.pallas_call(
        _gla_kernel,
        grid=(n,),
        in_specs=[blk(256, fwd), blk(hv, fwd), blk(LANES, fwd),
                  blk(256, bwd), blk(hv, bwd), blk(LANES, bwd),
                  pl.BlockSpec((2, LANES, LANES), lambda i: (0, 0, 0)),
                  pl.BlockSpec((2, 1, LANES), lambda i: (0, 0, 0))],
        out_specs=[blk(hv, fwd), blk(hv, bwd)],
        out_shape=[jax.ShapeDtypeStruct((bsz, s, hv), F32)] * 2,
        scratch_shapes=[pltpu.VMEM((2, bsz, GLA_DV, GLA_HEADS * GLA_DK), F32)],
        compiler_params=_cparams("arbitrary"),
        name="gla_scan",
    )(gla_qk, gla_v, gla_lr, gla_qk, gla_v, gla_lr, a2, ab)


def _softmax_parts(q, k):
    s = _nt_dot(q, k)
    p = jnp.exp(s - jnp.max(s, axis=-1, keepdims=True))
    return p.astype(BF16), 1.0 / jnp.sum(p, axis=-1, keepdims=True)


def _attn_kernel(gq_ref, gkv_ref, dq_ref, dk_ref, dv_ref, lam_ref, dnw_ref, go_ref, do_ref,
                 *, n_ctx, out_scale, first_block):
    lam = (jnp.exp(jnp.sum(lam_ref[0:1, :] * lam_ref[1:2, :], axis=-1, keepdims=True))
           - jnp.exp(jnp.sum(lam_ref[2:3, :] * lam_ref[3:4, :], axis=-1, keepdims=True))
           + (1.0 - out_scale))
    kw = GQA_KV_HEADS * GQA_HD

    def attend(nk):
        for kh in range(GQA_KV_HEADS):
            k = gkv_ref[0:nk, kh * GQA_HD:(kh + 1) * GQA_HD]
            v = gkv_ref[0:nk, kw + kh * GQA_HD:kw + (kh + 1) * GQA_HD]
            for g in range(GQA_GROUP):
                hs = slice((kh * GQA_GROUP + g) * GQA_HD, (kh * GQA_GROUP + g + 1) * GQA_HD)
                p, inv = _softmax_parts(gq_ref[:, hs], k)
                go_ref[:, hs] = (jnp.dot(p, v, preferred_element_type=F32) * inv).astype(BF16)
        for h in range(DIFF_HEADS):
            base = h * 2 * DIFF_QK
            v = dv_ref[0:nk, h * DIFF_V:(h + 1) * DIFF_V]
            p1, inv1 = _softmax_parts(dq_ref[:, base:base + DIFF_QK], dk_ref[0:nk, base:base + DIFF_QK])
            p2, inv2 = _softmax_parts(dq_ref[:, base + DIFF_QK:base + 2 * DIFF_QK],
                                      dk_ref[0:nk, base + DIFF_QK:base + 2 * DIFF_QK])
            o = (jnp.dot(p1, v, preferred_element_type=F32) * inv1
                 - jnp.dot(p2, v, preferred_element_type=F32) * (inv2 * lam))
            do_ref[:, h * DIFF_V:(h + 1) * DIFF_V] = (_rms(o, dnw_ref[...]) * out_scale).astype(BF16)

    if first_block == 0:
        @pl.when(pl.program_id(1) == 0)
        def _():
            attend(n_ctx)

        @pl.when(pl.program_id(1) > 0)
        def _():
            attend(gkv_ref.shape[0])
    else:
        attend(gkv_ref.shape[0])


def _attention(gq, gkv, dq, dk, dv, lam_rows, diff_norm_w, n_ctx, lambda_init, ctx_out):
    bsz, s, _ = gq.shape
    tq = TOKEN_BLOCK
    assert n_ctx == tq
    first = 0 if ctx_out else 1
    s_out = s - first * tq
    qrow = lambda w: pl.BlockSpec((None, tq, w), lambda b, j: (b, j + first, 0))
    orow = lambda w: pl.BlockSpec((None, tq, w), lambda b, j: (b, j, 0))
    full = lambda w: pl.BlockSpec((None, s, w), lambda b, j: (b, 0, 0))
    return pl.pallas_call(
        functools.partial(_attn_kernel, n_ctx=n_ctx, out_scale=1.0 - lambda_init, first_block=first),
        grid=(bsz, s_out // tq),
        in_specs=[qrow(512), full(256), qrow(256), full(256), full(256),
                  pl.BlockSpec((4, DIFF_QK), lambda b, j: (0, 0)),
                  pl.BlockSpec((1, DIFF_V), lambda b, j: (0, 0))],
        out_specs=[orow(512), orow(256)],
        out_shape=[jax.ShapeDtypeStruct((bsz, s_out, 512), BF16),
                   jax.ShapeDtypeStruct((bsz, s_out, 256), BF16)],
        compiler_params=_cparams("parallel", "parallel"),
        name="attention",
    )(gq, gkv, dq, dk, dv, lam_rows, diff_norm_w)


def _route(h, wr_ref):
    logits = [jnp.sum(h * wr_ref[e:e + 1, :], axis=-1, keepdims=True) for e in range(N_EXPERTS)]

    def top(ls):
        m = functools.reduce(jnp.maximum, ls)
        idx = jnp.full_like(m, float(N_EXPERTS))
        for e in reversed(range(N_EXPERTS)):
            idx = jnp.where(ls[e] == m, float(e), idx)
        return m, idx

    m1, i1 = top(logits)
    m2, i2 = top([jnp.where(i1 == float(e), -jnp.inf, logits[e]) for e in range(N_EXPERTS)])
    e2 = jnp.exp(m2 - m1)
    w1 = 1.0 / (1.0 + e2)
    w2 = e2 * w1
    lane = lax.broadcasted_iota(jnp.int32, (h.shape[0], LANES), 1)
    return jnp.where(lane == 0, w1, jnp.where(lane == 1, w2, jnp.where(lane == 2, i1,
                     jnp.where(lane == 3, i2, 0.0))))


def _out_kernel(*refs, with_router):
    if with_router:
        (x_ref, mod_ref, of_ref, ob_ref, r_ref, go_ref, do_ref, w_ref, g_ref, gnw_ref, nw_ref, wr_ref,
         x1_ref, h2_ref, route_ref) = refs
    else:
        (x_ref, mod_ref, of_ref, ob_ref, r_ref, go_ref, do_ref, w_ref, g_ref, gnw_ref, nw_ref,
         x1_ref, h2_ref) = refs
    mod = mod_ref[...]
    hv = GLA_HEADS * GLA_DV
    gw = GQA_HEADS * GQA_HD
    o = of_ref[...] + ob_ref[...]
    ms = jnp.dot((o * o).astype(BF16), g_ref[...], preferred_element_type=F32)
    gla = o * lax.rsqrt(ms + NORM_EPS) * gnw_ref[...] * _silu(r_ref[...])
    y = (jnp.dot(gla.astype(BF16), w_ref[0:hv, :], preferred_element_type=F32)
         + jnp.dot(go_ref[...], w_ref[hv:hv + gw, :], preferred_element_type=F32)
         + jnp.dot(do_ref[...], w_ref[hv + gw:, :], preferred_element_type=F32))
    x1 = x_ref[...] + mod[2:3] * y
    x1_ref[...] = x1
    h2 = _rms(x1, nw_ref[...]) * (1.0 + mod[4:5]) + mod[3:4]
    h2_ref[...] = h2.astype(h2_ref.dtype)
    if with_router:
        route_ref[...] = _route(h2, wr_ref)


def _out_projection(x_all, modsel, o_f, o_b, gla_r, gqa_o, diff_o, w_out, gmat, gla_norm_w, norm_ffn_w,
                    router_t, ctx_out):
    bsz, s, d = x_all.shape
    tm = TOKEN_BLOCK
    first = 0 if ctx_out else 1
    s_out = s - first * tm
    arow = lambda w: pl.BlockSpec((None, tm, w), lambda b, j: (b, j + first, 0))
    orow = lambda w: pl.BlockSpec((None, tm, w), lambda b, j: (b, j, 0))
    const = lambda shape: pl.BlockSpec(shape, lambda b, j: (0,) * len(shape))
    with_router = router_t is not None
    in_specs = [arow(d),
                pl.BlockSpec((None, None, 8, d), lambda b, j: (b, jnp.minimum(j + first, 1), 0, 0)),
                arow(256), arow(256), arow(256), orow(512), orow(256),
                const((d, d)), const((256, 256)), const((1, 256)), const((1, d))]
    args = [x_all, modsel, o_f, o_b, gla_r, gqa_o, diff_o, w_out, gmat, gla_norm_w, norm_ffn_w]
    out_specs = [orow(d), orow(d)]
    out_shape = [jax.ShapeDtypeStruct((bsz, s_out, d), F32),
                 jax.ShapeDtypeStruct((bsz, s_out, d), F32 if with_router else BF16)]
    if with_router:
        in_specs.append(const((N_EXPERTS, d)))
        args.append(router_t)
        out_specs.append(orow(LANES))
        out_shape.append(jax.ShapeDtypeStruct((bsz, s_out, LANES), F32))
    return pl.pallas_call(
        functools.partial(_out_kernel, with_router=with_router),
        grid=(bsz, s_out // tm),
        in_specs=in_specs, out_specs=out_specs, out_shape=out_shape,
        compiler_params=_cparams("parallel", "parallel"),
        name="out_projection",
    )(*args)


def _ffn_kernel(h_ref, x_ref, mod_ref, wg_ref, wu_ref, wd_ref, o_ref, acc_ref, *, n_ctx):
    f = pl.program_id(2)

    @pl.when(f == 0)
    def _():
        acc_ref[...] = jnp.zeros_like(acc_ref)

    h = h_ref[...]
    g = jnp.dot(h, wg_ref[...], preferred_element_type=F32)
    u = jnp.dot(h, wu_ref[...], preferred_element_type=F32)
    acc_ref[...] += jnp.dot((_silu(g) * u).astype(BF16), wd_ref[...], preferred_element_type=F32)

    @pl.when(f == pl.num_programs(2) - 1)
    def _():
        tm = h_ref.shape[0]
        row = pl.program_id(1) * tm + lax.broadcasted_iota(jnp.int32, (tm, 1), 0)
        gate = jnp.where(row < n_ctx, mod_ref[0, 5:6, :], mod_ref[1, 5:6, :])
        o_ref[...] = x_ref[...] + gate * acc_ref[...]


def _dense_ffn(h2, x1, modsel, wg, wu, wd, n_ctx):
    bsz, s, d = x1.shape
    ff = wg.shape[1]
    tm = s // 4
    tf = 256
    row = lambda: pl.BlockSpec((None, tm, d), lambda b, t, f: (b, t, 0))
    return pl.pallas_call(
        functools.partial(_ffn_kernel, n_ctx=n_ctx),
        grid=(bsz, s // tm, ff // tf),
        in_specs=[row(), row(),
                  pl.BlockSpec((None, 2, 8, d), lambda b, t, f: (b, 0, 0, 0)),
                  pl.BlockSpec((d, tf), lambda b, t, f: (0, f)),
                  pl.BlockSpec((d, tf), lambda b, t, f: (0, f)),
                  pl.BlockSpec((tf, d), lambda b, t, f: (f, 0))],
        out_specs=row(),
        out_shape=jax.ShapeDtypeStruct((bsz, s, d), F32),
        scratch_shapes=[pltpu.VMEM((tm, d), F32)],
        compiler_params=_cparams("parallel", "parallel", "arbitrary"),
        name="dense_ffn",
    )(h2, x1, modsel, wg, wu, wd)


MOE_TILE = 512


def _moe_kernel(te_ref, nv_ref, xs_ref, wg_ref, wu_ref, wd_ref, o_ref, acc_ref, xb_ref):
    i = pl.program_id(0)
    f = pl.program_id(1)

    @pl.when(i < nv_ref[0])
    def _():
        @pl.when(f == 0)
        def _():
            acc_ref[...] = jnp.zeros_like(acc_ref)
            xb_ref[...] = xs_ref[...].astype(BF16)

        h = xb_ref[...]
        g = jnp.dot(h, wg_ref[...], preferred_element_type=F32)
        u = jnp.dot(h, wu_ref[...], preferred_element_type=F32)
        acc_ref[...] += jnp.dot((_silu(g) * u).astype(BF16), wd_ref[...], preferred_element_type=F32)

        @pl.when(f == pl.num_programs(1) - 1)
        def _():
            o_ref[...] = acc_ref[...]


def _moe_experts(tile_expert, n_valid, xs, wg, wu, wd):
    rows, d = xs.shape
    ff = wg.shape[2]
    tm = MOE_TILE
    tf = 512
    nf = ff // tf

    def tile(i, nv):
        return jnp.minimum(i, nv[0] - 1)

    def fidx(i, f, nv):
        return jnp.where(i < nv[0], f, nf - 1)

    return pl.pallas_call(
        _moe_kernel,
        grid_spec=pltpu.PrefetchScalarGridSpec(
            num_scalar_prefetch=2,
            grid=(rows // tm, nf),
            in_specs=[pl.BlockSpec((tm, d), lambda i, f, te, nv: (tile(i, nv), 0)),
                      pl.BlockSpec((None, d, tf), lambda i, f, te, nv: (te[tile(i, nv)], 0, fidx(i, f, nv))),
                      pl.BlockSpec((None, d, tf), lambda i, f, te, nv: (te[tile(i, nv)], 0, fidx(i, f, nv))),
                      pl.BlockSpec((None, tf, d), lambda i, f, te, nv: (te[tile(i, nv)], fidx(i, f, nv), 0))],
            out_specs=pl.BlockSpec((tm, d), lambda i, f, te, nv: (tile(i, nv), 0)),
            scratch_shapes=[pltpu.VMEM((tm, d), F32), pltpu.VMEM((tm, d), BF16)]),
        out_shape=jax.ShapeDtypeStruct((rows, d), F32),
        compiler_params=_cparams("arbitrary", "arbitrary"),
        name="moe_experts",
    )(tile_expert, n_valid, xs, wg, wu, wd)


def _combine_kernel(x_ref, mod_ref, route_ref, ya_ref, yb_ref, nf_ref, o_ref):
    route = route_ref[...]
    moe = route[:, 0:1] * ya_ref[...] + route[:, 1:2] * yb_ref[...]
    x2 = x_ref[...] + mod_ref[5:6, :] * moe
    o_ref[...] = _rms(x2, nf_ref[...])


def _moe_combine(x1, modsel, route, ya, yb, norm_f_w):
    bsz, s, d = x1.shape
    tm = TOKEN_BLOCK
    row = lambda w: pl.BlockSpec((None, tm, w), lambda b, j: (b, j, 0))
    return pl.pallas_call(
        _combine_kernel,
        grid=(bsz, s // tm),
        in_specs=[row(d), pl.BlockSpec((None, None, 8, d), lambda b, j: (b, 1, 0, 0)),
                  row(LANES), row(d), row(d), pl.BlockSpec((1, d), lambda b, j: (0, 0))],
        out_specs=row(d),
        out_shape=jax.ShapeDtypeStruct((bsz, s, d), F32),
        compiler_params=_cparams("parallel", "parallel"),
        name="moe_combine",
    )(x1, modsel, route, ya, yb, norm_f_w)


def _final_norm_kernel(x_ref, nf_ref, o_ref):
    o_ref[...] = _rms(x_ref[...], nf_ref[...])


def _final_norm(x, norm_f_w, first_row):
    bsz, s, d = x.shape
    tm = TOKEN_BLOCK
    first = first_row // tm
    return pl.pallas_call(
        _final_norm_kernel,
        grid=(bsz, (s - first_row) // tm),
        in_specs=[pl.BlockSpec((None, tm, d), lambda b, j: (b, j + first, 0)),
                  pl.BlockSpec((1, d), lambda b, j: (0, 0))],
        out_specs=pl.BlockSpec((None, tm, d), lambda b, j: (b, j, 0)),
        out_shape=jax.ShapeDtypeStruct((bsz, s - first_row, d), F32),
        compiler_params=_cparams("parallel", "parallel"),
        name="final_norm",
    )(x, norm_f_w)


def _moe_plan(route, tm):
    n = route.shape[0]
    e_flat = route[:, 2:4].astype(jnp.int32).reshape(-1)
    onehot = (e_flat[:, None] == jnp.arange(N_EXPERTS, dtype=jnp.int32)[None, :]).astype(jnp.int32)
    csum = jnp.cumsum(onehot, axis=0)
    counts = csum[-1]
    padded = ((counts + tm - 1) // tm) * tm
    ends = jnp.cumsum(padded)
    pos = jnp.sum(onehot * (csum - 1 + (ends - padded)[None, :]), axis=1)
    rows = 2 * n + N_EXPERTS * tm
    n_tiles = rows // tm
    src = jnp.zeros((rows,), jnp.int32).at[pos].set(jnp.arange(2 * n, dtype=jnp.int32) // 2)
    starts = jnp.arange(n_tiles, dtype=jnp.int32) * tm
    tile_expert = jnp.minimum(jnp.sum((ends[None, :] <= starts[:, None]).astype(jnp.int32), axis=1),
                              N_EXPERTS - 1)
    n_valid = (ends[-1] // tm).astype(jnp.int32).reshape(1)
    return src, pos.reshape(n, 2), tile_expert, n_valid


def _rope_tables(n_ctx, seq, head_dim, copies):
    half = head_dim // 4
    t = np.arange(seq)
    freqs = ROPE_THETA ** (-jnp.arange(half, dtype=F32) / half)
    rows = jnp.asarray(t // GRID_W, F32)[:, None] * freqs[None, :]
    cols = jnp.asarray(t % GRID_W, F32)[:, None] * freqs[None, :]
    cos = jnp.concatenate([jnp.cos(rows)] * 2 + [jnp.cos(cols)] * 2, axis=1)
    sin = jnp.concatenate([-jnp.sin(rows), jnp.sin(rows), -jnp.sin(cols), jnp.sin(cols)], axis=1)
    cos = jnp.concatenate([jnp.ones((n_ctx, head_dim), F32), cos], axis=0)
    sin = jnp.concatenate([jnp.zeros((n_ctx, head_dim), F32), sin], axis=0)
    return jnp.tile(cos, (1, copies)), jnp.tile(sin, (1, copies))


def _group_mean_matrix(width, group):
    g = np.kron(np.eye(width // group, dtype=np.float32), np.ones((group, group), np.float32)) / group
    return jnp.asarray(g, BF16)


def kernel(x, c, ctx, c_ctx, w_mod, b_mod, norm_mix, norm_ffn, w_in, w_out, gla_a2_f, gla_ab_f, gla_a2_b,
           gla_ab_b, gla_norm, gqa_q_norm, gqa_k_norm, diff_lam_q1, diff_lam_k1, diff_lam_q2, diff_lam_k2,
           diff_norm, ffn_gate, ffn_up, ffn_down, moe_router, moe_gate, moe_up, moe_down, norm_f):
    bsz, seq, d = x.shape
    n_ctx = ctx.shape[1]
    depth = w_mod.shape[0]

    c_rows = jnp.concatenate([c, c_ctx[None, :], jnp.zeros((8 - bsz - 1, d), F32)], axis=0)
    mod = _modulation(c_rows, w_mod, b_mod).reshape(depth, 8, 6, d)
    mod = jnp.pad(mod, ((0, 0), (0, 0), (0, 2), (0, 0)))
    modsel = jnp.stack([jnp.broadcast_to(mod[:, bsz][:, None], (depth, bsz, 8, d)), mod[:, :bsz]], axis=2)

    tables = (_rope_tables(n_ctx, seq, GQA_HD, GQA_HEADS) + _rope_tables(n_ctx, seq, DIFF_QK, 2 * DIFF_HEADS))
    gmat = _group_mean_matrix(GQA_HEADS * GQA_HD, GQA_HD)
    gmat_v = _group_mean_matrix(GLA_HEADS * GLA_DV, GLA_DV)

    w_in_p = jnp.concatenate([w_in[:, :, :_W_IN_LR_END],
                              jnp.zeros((depth, d, _W_GQA_Q - _W_IN_LR_END), F32),
                              w_in[:, :, _W_IN_LR_END:]], axis=2).astype(BF16)
    w_out_b = w_out.astype(BF16)
    a2 = jnp.zeros((depth, 2, LANES, LANES), F32)
    a2 = a2.at[:, 0, 0:GLA_GATE_RANK].set(gla_a2_f).at[:, 1, GLA_GATE_RANK:2 * GLA_GATE_RANK].set(gla_a2_b)
    a2 = a2.astype(BF16)
    ab = jnp.stack([gla_ab_f, gla_ab_b], axis=1)[:, :, None, :]

    x_all = jnp.concatenate([ctx, x], axis=1)
    out = None
    for l in range(depth):
        ctx_out = l < depth - 1
        lambda_init = 0.8 - 0.6 * math.exp(-0.3 * l)
        gla_qk, gla_v, gla_r, gla_lr, gq, gkv, dq, dk, dv = _in_projection(
            x_all, modsel[l], norm_mix[l][None, :], w_in_p[l], gmat,
            jnp.tile(gqa_q_norm[l], GQA_HEADS)[None, :], jnp.tile(gqa_k_norm[l], GQA_KV_HEADS)[None, :], tables)
        o_f, o_b = _gla(gla_qk, gla_v, gla_lr, a2[l], ab[l], n_ctx // GLA_CHUNK)
        lam_rows = jnp.stack([diff_lam_q1[l], diff_lam_k1[l], diff_lam_q2[l], diff_lam_k2[l]], axis=0)
        gqa_o, diff_o = _attention(gq, gkv, dq, dk, dv, lam_rows, diff_norm[l][None, :], n_ctx, lambda_init,
                                   ctx_out)
        moe_layer = l % 2 == 1
        i = l // 2
        res = _out_projection(x_all, modsel[l], o_f, o_b, gla_r, gqa_o, diff_o, w_out_b[l], gmat_v,
                              jnp.tile(gla_norm[l], GLA_HEADS)[None, :], norm_ffn[l][None, :],
                              moe_router[i].T if moe_layer else None, ctx_out)
        if not moe_layer:
            x1, h2 = res
            if not ctx_out:
                raise NotImplementedError("dense feed-forward on the last layer")
            x_all = _dense_ffn(h2, x1, modsel[l], ffn_gate[i].astype(BF16), ffn_up[i].astype(BF16),
                               ffn_down[i].astype(BF16), n_ctx)
            if l == depth - 1:
                out = _final_norm(x_all, norm_f[None, :], n_ctx)
        else:
            x1, h2, route = res
            if ctx_out or l != depth - 1:
                raise NotImplementedError("expert feed-forward before the last layer")
            n = bsz * seq
            src, pos, tile_expert, n_valid = _moe_plan(route.reshape(n, LANES), MOE_TILE)
            xs = h2.reshape(n, d).at[src].get(mode="promise_in_bounds")
            ys = _moe_experts(tile_expert, n_valid, xs, moe_gate[i].astype(BF16), moe_up[i].astype(BF16),
                              moe_down[i].astype(BF16))
            ya = ys.at[pos[:, 0]].get(mode="promise_in_bounds").reshape(bsz, seq, d)
            yb = ys.at[pos[:, 1]].get(mode="promise_in_bounds").reshape(bsz, seq, d)
            out = _moe_combine(x1, modsel[l], route, ya, yb, norm_f[None, :])
    return out
```

```python
import functools
import math

import jax
import jax.numpy as jnp
import numpy as np
from jax import lax
from jax.experimental import pallas as pl
from jax.experimental.pallas import tpu as pltpu

F32 = jnp.float32
BF16 = jnp.bfloat16

GRID_W = 64
GLA_HEADS, GLA_DK, GLA_DV = 4, 32, 64
GLA_GATE_RANK = 16
GLA_GATE_NORM = 16.0
GLA_CHUNK = 64
GQA_HEADS, GQA_KV_HEADS, GQA_HD = 8, 2, 64
GQA_GROUP = GQA_HEADS // GQA_KV_HEADS
DIFF_HEADS, DIFF_QK, DIFF_V = 4, 32, 64
ROPE_THETA = 10000.0
NORM_EPS = 1e-6
N_EXPERTS = 8

LANES = 128
TOKEN_BLOCK = 256
VMEM_LIMIT = 56 * 1024 * 1024

_W_GLA_QK = 0
_W_GLA_V = 256
_W_GLA_R = 512
_W_GLA_LR = 768
_W_GQA_Q = 896
_W_GQA_K = 1408
_W_GQA_V = 1536
_W_DIFF_Q = 1664
_W_DIFF_K = 1920
_W_DIFF_V = 2176
_W_IN_PAD = 2432
_W_IN_LR_END = 800


def _cparams(*sem):
    return pltpu.CompilerParams(dimension_semantics=sem, vmem_limit_bytes=VMEM_LIMIT)


def _rms(x, w):
    ms = jnp.mean(x * x, axis=-1, keepdims=True)
    return x * lax.rsqrt(ms + NORM_EPS) * w


def _silu(x):
    return x * (1.0 / (1.0 + jnp.exp(-x)))


def _nt_dot(a, b):
    return lax.dot_general(a, b, (((1,), (1,)), ((), ())), preferred_element_type=F32)


def _tn_dot(a, b):
    return lax.dot_general(a, b, (((0,), (0,)), ((), ())), preferred_element_type=F32)


def _mod_kernel(c_ref, w_ref, b_ref, o_ref):
    s = _silu(c_ref[...])
    o_ref[...] = jnp.dot(s.astype(BF16), w_ref[...].astype(BF16),
                         preferred_element_type=F32) + b_ref[...]


def _modulation(c_rows, w_mod, b_mod):
    depth, d, n = w_mod.shape
    tn = 1536
    return pl.pallas_call(
        _mod_kernel,
        grid=(depth, n // tn),
        in_specs=[pl.BlockSpec((8, d), lambda l, j: (0, 0)),
                  pl.BlockSpec((None, d, tn), lambda l, j: (l, 0, j)),
                  pl.BlockSpec((None, 1, tn), lambda l, j: (l, 0, j))],
        out_specs=pl.BlockSpec((None, 8, tn), lambda l, j: (l, 0, j)),
        out_shape=jax.ShapeDtypeStruct((depth, 8, n), F32),
        compiler_params=_cparams("parallel", "parallel"),
        name="modulation",
    )(c_rows, w_mod, b_mod.reshape(depth, 1, n))


def _rope(x, cos, sin_signed, half):
    width = x.shape[1]
    lane = lax.broadcasted_iota(jnp.int32, x.shape, 1)
    first = (lane % (2 * half)) < half
    rot = jnp.where(first, pltpu.roll(x, width - half, 1), pltpu.roll(x, half, 1))
    return x * cos + rot * sin_signed


def _in_kernel(x_ref, mod_ref, nw_ref, w_ref, g_ref, qg_ref, kg_ref, cg_ref, sg_ref, cd_ref, sd_ref,
               glaqk_ref, glav_ref, glar_ref, glalr_ref, gq_ref, gkv_ref, dq_ref, dk_ref, dv_ref):
    mod = mod_ref[...]
    h = _rms(x_ref[...], nw_ref[...]) * (1.0 + mod[1:2]) + mod[0:1]
    hb = h.astype(BF16)

    def proj(a, b):
        return jnp.dot(hb, w_ref[:, a:b], preferred_element_type=F32)

    glaqk_ref[...] = proj(_W_GLA_QK, _W_GLA_V)
    glav_ref[...] = proj(_W_GLA_V, _W_GLA_R).astype(BF16)
    glar_ref[...] = proj(_W_GLA_R, _W_GLA_LR)
    glalr_ref[...] = proj(_W_GLA_LR, _W_GQA_Q)

    cos_g = cg_ref[...]
    sin_g = sg_ref[...]
    q = proj(_W_GQA_Q, _W_GQA_K)
    ms = jnp.dot((q * q).astype(BF16), g_ref[...], preferred_element_type=F32)
    qn = q * lax.rsqrt(ms + NORM_EPS) * qg_ref[...]
    gq_ref[...] = (_rope(qn, cos_g, sin_g, GQA_HD // 4) * GQA_HD ** -0.5).astype(BF16)

    kw = GQA_KV_HEADS * GQA_HD
    k = proj(_W_GQA_K, _W_GQA_V)
    ms = jnp.dot((k * k).astype(BF16), g_ref[0:kw, 0:kw], preferred_element_type=F32)
    kn = k * lax.rsqrt(ms + NORM_EPS) * kg_ref[...]
    gkv_ref[:, 0:kw] = _rope(kn, cos_g[:, 0:kw], sin_g[:, 0:kw], GQA_HD // 4).astype(BF16)
    gkv_ref[:, kw:2 * kw] = proj(_W_GQA_V, _W_DIFF_Q).astype(BF16)

    cos_d = cd_ref[...]
    sin_d = sd_ref[...]
    dq = proj(_W_DIFF_Q, _W_DIFF_K)
    dq_ref[...] = (_rope(dq, cos_d, sin_d, DIFF_QK // 4) * DIFF_QK ** -0.5).astype(BF16)
    dk = proj(_W_DIFF_K, _W_DIFF_V)
    dk_ref[...] = _rope(dk, cos_d, sin_d, DIFF_QK // 4).astype(BF16)
    dv_ref[...] = proj(_W_DIFF_V, _W_IN_PAD).astype(BF16)


def _in_projection(x_all, modsel, norm_w, w_in_p, gmat, q_gain, k_gain, tables):
    bsz, s, d = x_all.shape
    tm = TOKEN_BLOCK
    cos_g, sin_g, cos_d, sin_d = tables
    row = lambda w: pl.BlockSpec((None, tm, w), lambda b, j: (b, j, 0))
    const = lambda shape: pl.BlockSpec(shape, lambda b, j: (0,) * len(shape))
    tab = lambda w: pl.BlockSpec((tm, w), lambda b, j: (j, 0))
    widths = [(256, F32), (256, BF16), (256, F32), (128, F32), (512, BF16), (256, BF16),
              (256, BF16), (256, BF16), (256, BF16)]
    return pl.pallas_call(
        _in_kernel,
        grid=(bsz, s // tm),
        in_specs=[row(d),
                  pl.BlockSpec((None, None, 8, d), lambda b, j: (b, jnp.minimum(j, 1), 0, 0)),
                  const((1, d)), const((d, _W_IN_PAD)), const((512, 512)),
                  const((1, 512)), const((1, 128)),
                  tab(512), tab(512), tab(256), tab(256)],
        out_specs=[row(w) for w, _ in widths],
        out_shape=[jax.ShapeDtypeStruct((bsz, s, w), dt) for w, dt in widths],
        compiler_params=_cparams("parallel", "parallel"),
        name="in_projection",
    )(x_all, modsel, norm_w, w_in_p, gmat, q_gain, k_gain, cos_g, sin_g, cos_d, sin_d)


def _split3(g):
    g1 = g.astype(BF16)
    r1 = g - g1.astype(F32)
    g2 = r1.astype(BF16)
    g3 = (r1 - g2.astype(F32)).astype(BF16)
    return jnp.concatenate([g1, g2, g3], axis=1)


def _gla_kernel(qkf_ref, vf_ref, lrf_ref, qkb_ref, vb_ref, lrb_ref, a2_ref, ab_ref,
                of_ref, ob_ref, st_ref):
    @pl.when(pl.program_id(0) == 0)
    def _():
        st_ref[...] = jnp.zeros_like(st_ref)

    n_b = qkf_ref.shape[0]
    c = GLA_CHUNK
    hk = GLA_HEADS * GLA_DK
    r_i = lax.broadcasted_iota(jnp.int32, (c, c), 0)
    c_i = lax.broadcasted_iota(jnp.int32, (c, c), 1)
    dirs = ((qkf_ref, vf_ref, lrf_ref, of_ref, c_i <= r_i, c - 1),
            (qkb_ref, vb_ref, lrb_ref, ob_ref, c_i >= r_i, 0))
    loaded = [[(qk_ref[b], lr_ref[b], v_ref[b], st_ref[d, b]) for b in range(n_b)]
              for d, (qk_ref, v_ref, lr_ref, _, _, _) in enumerate(dirs)]
    results = []
    for d, (_, _, _, o_ref, tri, last) in enumerate(dirs):
        tri_b = jnp.where(tri, 1.0, 0.0).astype(BF16)
        for b in range(n_b):
            qk, lr, v, st = loaded[d][b]
            pre = jnp.dot(lr.astype(BF16), a2_ref[d], preferred_element_type=F32) + ab_ref[d]
            log_a = (jnp.minimum(pre, 0.0) - jnp.log(1.0 + jnp.exp(-jnp.abs(pre)))) * (1.0 / GLA_GATE_NORM)
            cs = jnp.dot(tri_b, _split3(log_a), preferred_element_type=F32)
            bc = cs[:, 0:hk] + cs[:, hk:2 * hk] + cs[:, 2 * hk:3 * hk]
            tot = bc[last:last + 1, :]
            q = qk[:, 0:hk] * GLA_DK ** -0.5
            k = qk[:, hk:2 * hk]
            qd = (q * jnp.exp(bc)).astype(BF16)
            kd = (k * jnp.exp(-bc)).astype(BF16)
            kl = (k * jnp.exp(tot - bc)).astype(BF16)
            outs, kvs = [], []
            for h in range(GLA_HEADS):
                ks = slice(h * GLA_DK, (h + 1) * GLA_DK)
                v_h = v[:, h * GLA_DV:(h + 1) * GLA_DV]
                a = jnp.where(tri, _nt_dot(qd[:, ks], kd[:, ks]), 0.0).astype(BF16)
                outs.append(jnp.dot(a, v_h, preferred_element_type=F32)
                            + _nt_dot(qd[:, ks], st[:, ks].astype(BF16)))
                kvs.append(_tn_dot(v_h, kl[:, ks]))
            results.append((o_ref, d, b, jnp.concatenate(outs, axis=1),
                            st * jnp.exp(tot) + jnp.concatenate(kvs, axis=1)))
    for o_ref, d, b, o, st_new in results:
        o_ref[b] = o
        st_ref[d, b] = st_new


def _gla(gla_qk, gla_v, gla_lr, a2, ab, n_ctx_chunks):
    bsz, s, _ = gla_qk.shape
    c = GLA_CHUNK
    n = s // c
    fwd = lambda i: (0, i, 0)
    bwd = lambda i: (0, jnp.where(i < n_ctx_chunks, n_ctx_chunks - 1 - i, n + n_ctx_chunks - 1 - i), 0)
    blk = lambda w, m: pl.BlockSpec((bsz, c, w), m)
    hv = GLA_HEADS * GLA_DV
    return pl.pallas_call(
        _gla_kernel,
        grid=(n,),
        in_specs=[blk(256, fwd), blk(hv, fwd), blk(LANES, fwd),
                  blk(256, bwd), blk(hv, bwd), blk(LANES, bwd),
                  pl.BlockSpec((2, LANES, LANES), lambda i: (0, 0, 0)),
                  pl.BlockSpec((2, 1, LANES), lambda i: (0, 0, 0))],
        out_specs=[blk(hv, fwd), blk(hv, bwd)],
        out_shape=[jax.ShapeDtypeStruct((bsz, s, hv), F32)] * 2,
        scratch_shapes=[pltpu.VMEM((2, bsz, GLA_DV, GLA_HEADS * GLA_DK), F32)],
        compiler_params=_cparams("arbitrary"),
        name="gla_scan",
    )(gla_qk, gla_v, gla_lr, gla_qk, gla_v, gla_lr, a2, ab)


def _softmax_parts(q, k):
    s = _nt_dot(q, k)
    p = jnp.exp(s - jnp.max(s, axis=-1, keepdims=True))
    return p.astype(BF16), 1.0 / jnp.sum(p, axis=-1, keepdims=True)


def _attn_kernel(gq_ref, gkv_ref, dq_ref, dk_ref, dv_ref, lam_ref, dnw_ref, go_ref, do_ref,
                 *, n_ctx, out_scale, first_block):
    lam = (jnp.exp(jnp.sum(lam_ref[0:1, :] * lam_ref[1:2, :], axis=-1, keepdims=True))
           - jnp.exp(jnp.sum(lam_ref[2:3, :] * lam_ref[3:4, :], axis=-1, keepdims=True))
           + (1.0 - out_scale))
    kw = GQA_KV_HEADS * GQA_HD

    def attend(nk):
        for kh in range(GQA_KV_HEADS):
            k = gkv_ref[0:nk, kh * GQA_HD:(kh + 1) * GQA_HD]
            v = gkv_ref[0:nk, kw + kh * GQA_HD:kw + (kh + 1) * GQA_HD]
            for g in range(GQA_GROUP):
                hs = slice((kh * GQA_GROUP + g) * GQA_HD, (kh * GQA_GROUP + g + 1) * GQA_HD)
                p, inv = _softmax_parts(gq_ref[:, hs], k)
                go_ref[:, hs] = (jnp.dot(p, v, preferred_element_type=F32) * inv).astype(BF16)
        for h in range(DIFF_HEADS):
            base = h * 2 * DIFF_QK
            v = dv_ref[0:nk, h * DIFF_V:(h + 1) * DIFF_V]
            p1, inv1 = _softmax_parts(dq_ref[:, base:base + DIFF_QK], dk_ref[0:nk, base:base + DIFF_QK])
            p2, inv2 = _softmax_parts(dq_ref[:, base + DIFF_QK:base + 2 * DIFF_QK],
                                      dk_ref[0:nk, base + DIFF_QK:base + 2 * DIFF_QK])
            o = (jnp.dot(p1, v, preferred_element_type=F32) * inv1
                 - jnp.dot(p2, v, preferred_element_type=F32) * (inv2 * lam))
            do_ref[:, h * DIFF_V:(h + 1) * DIFF_V] = (_rms(o, dnw_ref[...]) * out_scale).astype(BF16)

    if first_block == 0:
        @pl.when(pl.program_id(1) == 0)
        def _():
            attend(n_ctx)

        @pl.when(pl.program_id(1) > 0)
        def _():
            attend(gkv_ref.shape[0])
    else:
        attend(gkv_ref.shape[0])


def _attention(gq, gkv, dq, dk, dv, lam_rows, diff_norm_w, n_ctx, lambda_init, ctx_out):
    bsz, s, _ = gq.shape
    tq = TOKEN_BLOCK
    assert n_ctx == tq
    first = 0 if ctx_out else 1
    s_out = s - first * tq
    qrow = lambda w: pl.BlockSpec((None, tq, w), lambda b, j: (b, j + first, 0))
    orow = lambda w: pl.BlockSpec((None, tq, w), lambda b, j: (b, j, 0))
    full = lambda w: pl.BlockSpec((None, s, w), lambda b, j: (b, 0, 0))
    return pl.pallas_call(
        functools.partial(_attn_kernel, n_ctx=n_ctx, out_scale=1.0 - lambda_init, first_block=first),
        grid=(bsz, s_out // tq),
        in_specs=[qrow(512), full(256), qrow(256), full(256), full(256),
                  pl.BlockSpec((4, DIFF_QK), lambda b, j: (0, 0)),
                  pl.BlockSpec((1, DIFF_V), lambda b, j: (0, 0))],
        out_specs=[orow(512), orow(256)],
        out_shape=[jax.ShapeDtypeStruct((bsz, s_out, 512), BF16),
                   jax.ShapeDtypeStruct((bsz, s_out, 256), BF16)],
        compiler_params=_cparams("parallel", "parallel"),
        name="attention",
    )(gq, gkv, dq, dk, dv, lam_rows, diff_norm_w)


def _route(h, wr_ref):
    logits = [jnp.sum(h * wr_ref[e:e + 1, :], axis=-1, keepdims=True) for e in range(N_EXPERTS)]

    def top(ls):
        m = functools.reduce(jnp.maximum, ls)
        idx = jnp.full_like(m, float(N_EXPERTS))
        for e in reversed(range(N_EXPERTS)):
            idx = jnp.where(ls[e] == m, float(e), idx)
        return m, idx

    m1, i1 = top(logits)
    m2, i2 = top([jnp.where(i1 == float(e), -jnp.inf, logits[e]) for e in range(N_EXPERTS)])
    e2 = jnp.exp(m2 - m1)
    w1 = 1.0 / (1.0 + e2)
    w2 = e2 * w1
    lane = lax.broadcasted_iota(jnp.int32, (h.shape[0], LANES), 1)
    return jnp.where(lane == 0, w1, jnp.where(lane == 1, w2, jnp.where(lane == 2, i1,
                     jnp.where(lane == 3, i2, 0.0))))


def _out_kernel(*refs, with_router):
    if with_router:
        (x_ref, mod_ref, of_ref, ob_ref, r_ref, go_ref, do_ref, w_ref, g_ref, gnw_ref, nw_ref, wr_ref,
         x1_ref, h2_ref, route_ref) = refs
    else:
        (x_ref, mod_ref, of_ref, ob_ref, r_ref, go_ref, do_ref, w_ref, g_ref, gnw_ref, nw_ref,
         x1_ref, h2_ref) = refs
    mod = mod_ref[...]
    hv = GLA_HEADS * GLA_DV
    gw = GQA_HEADS * GQA_HD
    o = of_ref[...] + ob_ref[...]
    ms = jnp.dot((o * o).astype(BF16), g_ref[...], preferred_element_type=F32)
    gla = o * lax.rsqrt(ms + NORM_EPS) * gnw_ref[...] * _silu(r_ref[...])
    y = (jnp.dot(gla.astype(BF16), w_ref[0:hv, :], preferred_element_type=F32)
         + jnp.dot(go_ref[...], w_ref[hv:hv + gw, :], preferred_element_type=F32)
         + jnp.dot(do_ref[...], w_ref[hv + gw:, :], preferred_element_type=F32))
    x1 = x_ref[...] + mod[2:3] * y
    x1_ref[...] = x1
    h2 = _rms(x1, nw_ref[...]) * (1.0 + mod[4:5]) + mod[3:4]
    h2_ref[...] = h2.astype(h2_ref.dtype)
    if with_router:
        route_ref[...] = _route(h2, wr_ref)


def _out_projection(x_all, modsel, o_f, o_b, gla_r, gqa_o, diff_o, w_out, gmat, gla_norm_w, norm_ffn_w,
                    router_t, ctx_out):
    bsz, s, d = x_all.shape
    tm = TOKEN_BLOCK
    first = 0 if ctx_out else 1
    s_out = s - first * tm
    arow = lambda w: pl.BlockSpec((None, tm, w), lambda b, j: (b, j + first, 0))
    orow = lambda w: pl.BlockSpec((None, tm, w), lambda b, j: (b, j, 0))
    const = lambda shape: pl.BlockSpec(shape, lambda b, j: (0,) * len(shape))
    with_router = router_t is not None
    in_specs = [arow(d),
                pl.BlockSpec((None, None, 8, d), lambda b, j: (b, jnp.minimum(j + first, 1), 0, 0)),
                arow(256), arow(256), arow(256), orow(512), orow(256),
                const((d, d)), const((256, 256)), const((1, 256)), const((1, d))]
    args = [x_all, modsel, o_f, o_b, gla_r, gqa_o, diff_o, w_out, gmat, gla_norm_w, norm_ffn_w]
    out_specs = [orow(d), orow(d)]
    out_shape = [jax.ShapeDtypeStruct((bsz, s_out, d), F32),
                 jax.ShapeDtypeStruct((bsz, s_out, d), F32 if with_router else BF16)]
    if with_router:
        in_specs.append(const((N_EXPERTS, d)))
        args.append(router_t)
        out_specs.append(orow(LANES))
        out_shape.append(jax.ShapeDtypeStruct((bsz, s_out, LANES), F32))
    return pl.pallas_call(
        functools.partial(_out_kernel, with_router=with_router),
        grid=(bsz, s_out // tm),
        in_specs=in_specs, out_specs=out_specs, out_shape=out_shape,
        compiler_params=_cparams("parallel", "parallel"),
        name="out_projection",
    )(*args)


def _ffn_kernel(h_ref, x_ref, mod_ref, wg_ref, wu_ref, wd_ref, o_ref, acc_ref, *, n_ctx):
    f = pl.program_id(2)

    @pl.when(f == 0)
    def _():
        acc_ref[...] = jnp.zeros_like(acc_ref)

    h = h_ref[...]
    g = jnp.dot(h, wg_ref[...], preferred_element_type=F32)
    u = jnp.dot(h, wu_ref[...], preferred_element_type=F32)
    acc_ref[...] += jnp.dot((_silu(g) * u).astype(BF16), wd_ref[...], preferred_element_type=F32)

    @pl.when(f == pl.num_programs(2) - 1)
    def _():
        tm = h_ref.shape[0]
        row = pl.program_id(1) * tm + lax.broadcasted_iota(jnp.int32, (tm, 1), 0)
        gate = jnp.where(row < n_ctx, mod_ref[0, 5:6, :], mod_ref[1, 5:6, :])
        o_ref[...] = x_ref[...] + gate * acc_ref[...]


def _dense_ffn(h2, x1, modsel, wg, wu, wd, n_ctx):
    bsz, s, d = x1.shape
    ff = wg.shape[1]
    tm = s // 4
    tf = 256
    row = lambda: pl.BlockSpec((None, tm, d), lambda b, t, f: (b, t, 0))
    return pl.pallas_call(
        functools.partial(_ffn_kernel, n_ctx=n_ctx),
        grid=(bsz, s // tm, ff // tf),
        in_specs=[row(), row(),
                  pl.BlockSpec((None, 2, 8, d), lambda b, t, f: (b, 0, 0, 0)),
                  pl.BlockSpec((d, tf), lambda b, t, f: (0, f)),
                  pl.BlockSpec((d, tf), lambda b, t, f: (0, f)),
                  pl.BlockSpec((tf, d), lambda b, t, f: (f, 0))],
        out_specs=row(),
        out_shape=jax.ShapeDtypeStruct((bsz, s, d), F32),
        scratch_shapes=[pltpu.VMEM((tm, d), F32)],
        compiler_params=_cparams("parallel", "parallel", "arbitrary"),
        name="dense_ffn",
    )(h2, x1, modsel, wg, wu, wd)


MOE_TILE = 1024
MOE_SUB = 256
MOE_FF_BLOCK = 512


def _moe_kernel(te_ref, nv_ref, vr_ref, xs_ref, wg_ref, wu_ref, wd_ref, o_ref, acc_ref, xb_ref):
    f = pl.program_id(1)
    last_f = pl.num_programs(1) - 1
    valid = vr_ref[pl.program_id(0)]

    @pl.when(valid > 0)
    def _():
        wg = wg_ref[...].astype(BF16)
        wu = wu_ref[...].astype(BF16)
        wd = wd_ref[...].astype(BF16)
        for sb in range(MOE_TILE // MOE_SUB):
            rows = pl.ds(sb * MOE_SUB, MOE_SUB)

            @pl.when(sb * MOE_SUB < valid)
            def _():
                @pl.when(f == 0)
                def _():
                    acc_ref[rows, :] = jnp.zeros((MOE_SUB, acc_ref.shape[1]), F32)
                    xb_ref[rows, :] = xs_ref[rows, :].astype(BF16)

                h = xb_ref[rows, :]
                g = jnp.dot(h, wg, preferred_element_type=F32)
                u = jnp.dot(h, wu, preferred_element_type=F32)
                acc_ref[rows, :] += jnp.dot((_silu(g) * u).astype(BF16), wd, preferred_element_type=F32)

                @pl.when(f == last_f)
                def _():
                    o_ref[rows, :] = acc_ref[rows, :]

            @pl.when(jnp.logical_and(sb * MOE_SUB >= valid, f == last_f))
            def _():
                o_ref[rows, :] = jnp.zeros((MOE_SUB, o_ref.shape[1]), F32)


def _moe_experts(tile_expert, n_valid, valid_rows, xs, wg, wu, wd):
    rows, d = xs.shape
    ff = wg.shape[2]
    tm = MOE_TILE
    tf = MOE_FF_BLOCK
    nf = ff // tf

    def tile(i, nv):
        return jnp.minimum(i, nv[0] - 1)

    def fidx(i, f, nv):
        return jnp.where(i < nv[0], f, nf - 1)

    return pl.pallas_call(
        _moe_kernel,
        grid_spec=pltpu.PrefetchScalarGridSpec(
            num_scalar_prefetch=3,
            grid=(rows // tm, nf),
            in_specs=[pl.BlockSpec((tm, d), lambda i, f, te, nv, vr: (tile(i, nv), 0)),
                      pl.BlockSpec((None, d, tf), lambda i, f, te, nv, vr: (te[tile(i, nv)], 0, fidx(i, f, nv))),
                      pl.BlockSpec((None, d, tf), lambda i, f, te, nv, vr: (te[tile(i, nv)], 0, fidx(i, f, nv))),
                      pl.BlockSpec((None, tf, d), lambda i, f, te, nv, vr: (te[tile(i, nv)], fidx(i, f, nv), 0))],
            out_specs=pl.BlockSpec((tm, d), lambda i, f, te, nv, vr: (tile(i, nv), 0)),
            scratch_shapes=[pltpu.VMEM((tm, d), F32), pltpu.VMEM((tm, d), BF16)]),
        out_shape=jax.ShapeDtypeStruct((rows, d), F32),
        compiler_params=_cparams("arbitrary", "arbitrary"),
        name="moe_experts",
    )(tile_expert, n_valid, valid_rows, xs, wg, wu, wd)


def _combine_kernel(x_ref, mod_ref, route_ref, ya_ref, yb_ref, nf_ref, o_ref):
    route = route_ref[...]
    moe = route[:, 0:1] * ya_ref[...] + route[:, 1:2] * yb_ref[...]
    x2 = x_ref[...] + mod_ref[5:6, :] * moe
    o_ref[...] = _rms(x2, nf_ref[...])


def _moe_combine(x1, modsel, route, ya, yb, norm_f_w):
    bsz, s, d = x1.shape
    tm = TOKEN_BLOCK
    row = lambda w: pl.BlockSpec((None, tm, w), lambda b, j: (b, j, 0))
    return pl.pallas_call(
        _combine_kernel,
        grid=(bsz, s // tm),
        in_specs=[row(d), pl.BlockSpec((None, None, 8, d), lambda b, j: (b, 1, 0, 0)),
                  row(LANES), row(d), row(d), pl.BlockSpec((1, d), lambda b, j: (0, 0))],
        out_specs=row(d),
        out_shape=jax.ShapeDtypeStruct((bsz, s, d), F32),
        compiler_params=_cparams("parallel", "parallel"),
        name="moe_combine",
    )(x1, modsel, route, ya, yb, norm_f_w)


def _final_norm_kernel(x_ref, nf_ref, o_ref):
    o_ref[...] = _rms(x_ref[...], nf_ref[...])


def _final_norm(x, norm_f_w, first_row):
    bsz, s, d = x.shape
    tm = TOKEN_BLOCK
    first = first_row // tm
    return pl.pallas_call(
        _final_norm_kernel,
        grid=(bsz, (s - first_row) // tm),
        in_specs=[pl.BlockSpec((None, tm, d), lambda b, j: (b, j + first, 0)),
                  pl.BlockSpec((1, d), lambda b, j: (0, 0))],
        out_specs=pl.BlockSpec((None, tm, d), lambda b, j: (b, j, 0)),
        out_shape=jax.ShapeDtypeStruct((bsz, s - first_row, d), F32),
        compiler_params=_cparams("parallel", "parallel"),
        name="final_norm",
    )(x, norm_f_w)


def _moe_plan(route, tm):
    n = route.shape[0]
    e_flat = route[:, 2:4].astype(jnp.int32).reshape(-1)
    onehot = (e_flat[:, None] == jnp.arange(N_EXPERTS, dtype=jnp.int32)[None, :]).astype(jnp.int32)
    csum = jnp.cumsum(onehot, axis=0)
    counts = csum[-1]
    padded = ((counts + tm - 1) // tm) * tm
    ends = jnp.cumsum(padded)
    pos = jnp.sum(onehot * (csum - 1 + (ends - padded)[None, :]), axis=1)
    rows = 2 * n + N_EXPERTS * tm
    n_tiles = rows // tm
    src = jnp.zeros((rows,), jnp.int32).at[pos].set(jnp.arange(2 * n, dtype=jnp.int32) // 2)
    starts = jnp.arange(n_tiles, dtype=jnp.int32) * tm
    tile_expert = jnp.minimum(jnp.sum((ends[None, :] <= starts[:, None]).astype(jnp.int32), axis=1),
                              N_EXPERTS - 1)
    n_valid = (ends[-1] // tm).astype(jnp.int32).reshape(1)
    token_end = (ends - padded + counts)[tile_expert]
    valid_rows = jnp.clip(token_end - starts, 0, tm).astype(jnp.int32)
    return src, pos.reshape(n, 2), tile_expert, n_valid, valid_rows


def _rope_tables(n_ctx, seq, head_dim, copies):
    half = head_dim // 4
    t = np.arange(seq)
    freqs = ROPE_THETA ** (-jnp.arange(half, dtype=F32) / half)
    rows = jnp.asarray(t // GRID_W, F32)[:, None] * freqs[None, :]
    cols = jnp.asarray(t % GRID_W, F32)[:, None] * freqs[None, :]
    cos = jnp.concatenate([jnp.cos(rows)] * 2 + [jnp.cos(cols)] * 2, axis=1)
    sin = jnp.concatenate([-jnp.sin(rows), jnp.sin(rows), -jnp.sin(cols), jnp.sin(cols)], axis=1)
    cos = jnp.concatenate([jnp.ones((n_ctx, head_dim), F32), cos], axis=0)
    sin = jnp.concatenate([jnp.zeros((n_ctx, head_dim), F32), sin], axis=0)
    return jnp.tile(cos, (1, copies)), jnp.tile(sin, (1, copies))


def _group_mean_matrix(width, group):
    g = np.kron(np.eye(width // group, dtype=np.float32), np.ones((group, group), np.float32)) / group
    return jnp.asarray(g, BF16)


def kernel(x, c, ctx, c_ctx, w_mod, b_mod, norm_mix, norm_ffn, w_in, w_out, gla_a2_f, gla_ab_f, gla_a2_b,
           gla_ab_b, gla_norm, gqa_q_norm, gqa_k_norm, diff_lam_q1, diff_lam_k1, diff_lam_q2, diff_lam_k2,
           diff_norm, ffn_gate, ffn_up, ffn_down, moe_router, moe_gate, moe_up, moe_down, norm_f):
    bsz, seq, d = x.shape
    n_ctx = ctx.shape[1]
    depth = w_mod.shape[0]

    c_rows = jnp.concatenate([c, c_ctx[None, :], jnp.zeros((8 - bsz - 1, d), F32)], axis=0)
    mod = _modulation(c_rows, w_mod, b_mod).reshape(depth, 8, 6, d)
    mod = jnp.pad(mod, ((0, 0), (0, 0), (0, 2), (0, 0)))
    modsel = jnp.stack([jnp.broadcast_to(mod[:, bsz][:, None], (depth, bsz, 8, d)), mod[:, :bsz]], axis=2)

    tables = (_rope_tables(n_ctx, seq, GQA_HD, GQA_HEADS) + _rope_tables(n_ctx, seq, DIFF_QK, 2 * DIFF_HEADS))
    gmat = _group_mean_matrix(GQA_HEADS * GQA_HD, GQA_HD)
    gmat_v = _group_mean_matrix(GLA_HEADS * GLA_DV, GLA_DV)

    w_in_p = jnp.concatenate([w_in[:, :, :_W_IN_LR_END],
                              jnp.zeros((depth, d, _W_GQA_Q - _W_IN_LR_END), F32),
                              w_in[:, :, _W_IN_LR_END:]], axis=2).astype(BF16)
    w_out_b = w_out.astype(BF16)
    a2 = jnp.zeros((depth, 2, LANES, LANES), F32)
    a2 = a2.at[:, 0, 0:GLA_GATE_RANK].set(gla_a2_f).at[:, 1, GLA_GATE_RANK:2 * GLA_GATE_RANK].set(gla_a2_b)
    a2 = a2.astype(BF16)
    ab = jnp.stack([gla_ab_f, gla_ab_b], axis=1)[:, :, None, :]

    x_all = jnp.concatenate([ctx, x], axis=1)
    out = None
    for l in range(depth):
        ctx_out = l < depth - 1
        lambda_init = 0.8 - 0.6 * math.exp(-0.3 * l)
        gla_qk, gla_v, gla_r, gla_lr, gq, gkv, dq, dk, dv = _in_projection(
            x_all, modsel[l], norm_mix[l][None, :], w_in_p[l], gmat,
            jnp.tile(gqa_q_norm[l], GQA_HEADS)[None, :], jnp.tile(gqa_k_norm[l], GQA_KV_HEADS)[None, :], tables)
        o_f, o_b = _gla(gla_qk, gla_v, gla_lr, a2[l], ab[l], n_ctx // GLA_CHUNK)
        lam_rows = jnp.stack([diff_lam_q1[l], diff_lam_k1[l], diff_lam_q2[l], diff_lam_k2[l]], axis=0)
        gqa_o, diff_o = _attention(gq, gkv, dq, dk, dv, lam_rows, diff_norm[l][None, :], n_ctx, lambda_init,
                                   ctx_out)
        moe_layer = l % 2 == 1
        i = l // 2
        res = _out_projection(x_all, modsel[l], o_f, o_b, gla_r, gqa_o, diff_o, w_out_b[l], gmat_v,
                              jnp.tile(gla_norm[l], GLA_HEADS)[None, :], norm_ffn[l][None, :],
                              moe_router[i].T if moe_layer else None, ctx_out)
        if not moe_layer:
            x1, h2 = res
            if not ctx_out:
                raise NotImplementedError("dense feed-forward on the last layer")
            x_all = _dense_ffn(h2, x1, modsel[l], ffn_gate[i].astype(BF16), ffn_up[i].astype(BF16),
                               ffn_down[i].astype(BF16), n_ctx)
            if l == depth - 1:
                out = _final_norm(x_all, norm_f[None, :], n_ctx)
        else:
            x1, h2, route = res
            if ctx_out or l != depth - 1:
                raise NotImplementedError("expert feed-forward before the last layer")
            n = bsz * seq
            src, pos, tile_expert, n_valid, valid_rows = _moe_plan(route.reshape(n, LANES), MOE_TILE)
            xs = h2.reshape(n, d).at[src].get(mode="promise_in_bounds")
            ys = _moe_experts(tile_expert, n_valid, valid_rows, xs, moe_gate[i], moe_up[i], moe_down[i])
            ya = ys.at[pos[:, 0]].get(mode="promise_in_bounds").reshape(bsz, seq, d)
            yb = ys.at[pos[:, 1]].get(mode="promise_in_bounds").reshape(bsz, seq, d)
            out = _moe_combine(x1, modsel[l], route, ya, yb, norm_f[None, :])
    return out
```

```python
import functools
import math

import jax
import jax.numpy as jnp
import numpy as np
from jax import lax
from jax.experimental import pallas as pl
from jax.experimental.pallas import tpu as pltpu

F32 = jnp.float32
BF16 = jnp.bfloat16

GRID_W = 64
GLA_HEADS, GLA_DK, GLA_DV = 4, 32, 64
GLA_GATE_RANK = 16
GLA_GATE_NORM = 16.0
GLA_CHUNK = 64
GQA_HEADS, GQA_KV_HEADS, GQA_HD = 8, 2, 64
GQA_GROUP = GQA_HEADS // GQA_KV_HEADS
DIFF_HEADS, DIFF_QK, DIFF_V = 4, 32, 64
ROPE_THETA = 10000.0
NORM_EPS = 1e-6
N_EXPERTS = 8

LANES = 128
TOKEN_BLOCK = 256
VMEM_LIMIT = 56 * 1024 * 1024

_W_GLA_QK = 0
_W_GLA_V = 256
_W_GLA_R = 512
_W_GLA_LR = 768
_W_GQA_Q = 896
_W_GQA_K = 1408
_W_GQA_V = 1536
_W_DIFF_Q = 1664
_W_DIFF_K = 1920
_W_DIFF_V = 2176
_W_IN_PAD = 2432
_W_IN_LR_END = 800


def _cparams(*sem):
    return pltpu.CompilerParams(dimension_semantics=sem, vmem_limit_bytes=VMEM_LIMIT)


def _rms(x, w):
    ms = jnp.mean(x * x, axis=-1, keepdims=True)
    return x * lax.rsqrt(ms + NORM_EPS) * w


def _silu(x):
    return x * (1.0 / (1.0 + jnp.exp(-x)))


def _nt_dot(a, b):
    return lax.dot_general(a, b, (((1,), (1,)), ((), ())), preferred_element_type=F32)


def _tn_dot(a, b):
    return lax.dot_general(a, b, (((0,), (0,)), ((), ())), preferred_element_type=F32)


def _mod_kernel(c_ref, w_ref, b_ref, o_ref):
    s = _silu(c_ref[...])
    o_ref[...] = jnp.dot(s.astype(BF16), w_ref[...].astype(BF16),
                         preferred_element_type=F32) + b_ref[...]


def _modulation(c_rows, w_mod, b_mod):
    depth, d, n = w_mod.shape
    tn = 1536
    return pl.pallas_call(
        _mod_kernel,
        grid=(depth, n // tn),
        in_specs=[pl.BlockSpec((8, d), lambda l, j: (0, 0)),
                  pl.BlockSpec((None, d, tn), lambda l, j: (l, 0, j)),
                  pl.BlockSpec((None, 1, tn), lambda l, j: (l, 0, j))],
        out_specs=pl.BlockSpec((None, 8, tn), lambda l, j: (l, 0, j)),
        out_shape=jax.ShapeDtypeStruct((depth, 8, n), F32),
        compiler_params=_cparams("parallel", "parallel"),
        name="modulation",
    )(c_rows, w_mod, b_mod.reshape(depth, 1, n))


def _rope(x, cos, sin_signed, half):
    width = x.shape[1]
    lane = lax.broadcasted_iota(jnp.int32, x.shape, 1)
    first = (lane % (2 * half)) < half
    rot = jnp.where(first, pltpu.roll(x, width - half, 1), pltpu.roll(x, half, 1))
    return x * cos + rot * sin_signed


def _in_kernel(x_ref, mod_ref, nw_ref, w_ref, g_ref, qg_ref, kg_ref, cg_ref, sg_ref, cd_ref, sd_ref,
               glaqk_ref, glav_ref, glar_ref, glalr_ref, gq_ref, gkv_ref, dq_ref, dk_ref, dv_ref):
    mod = mod_ref[...]
    h = _rms(x_ref[...], nw_ref[...]) * (1.0 + mod[1:2]) + mod[0:1]
    hb = h.astype(BF16)

    def proj(a, b):
        return jnp.dot(hb, w_ref[:, a:b], preferred_element_type=F32)

    glaqk_ref[...] = proj(_W_GLA_QK, _W_GLA_V)
    glav_ref[...] = proj(_W_GLA_V, _W_GLA_R).astype(BF16)
    glar_ref[...] = proj(_W_GLA_R, _W_GLA_LR)
    glalr_ref[...] = proj(_W_GLA_LR, _W_GQA_Q)

    cos_g = cg_ref[...]
    sin_g = sg_ref[...]
    q = proj(_W_GQA_Q, _W_GQA_K)
    ms = jnp.dot((q * q).astype(BF16), g_ref[...], preferred_element_type=F32)
    qn = q * lax.rsqrt(ms + NORM_EPS) * qg_ref[...]
    gq_ref[...] = (_rope(qn, cos_g, sin_g, GQA_HD // 4) * GQA_HD ** -0.5).astype(BF16)

    kw = GQA_KV_HEADS * GQA_HD
    k = proj(_W_GQA_K, _W_GQA_V)
    ms = jnp.dot((k * k).astype(BF16), g_ref[0:kw, 0:kw], preferred_element_type=F32)
    kn = k * lax.rsqrt(ms + NORM_EPS) * kg_ref[...]
    gkv_ref[:, 0:kw] = _rope(kn, cos_g[:, 0:kw], sin_g[:, 0:kw], GQA_HD // 4).astype(BF16)
    gkv_ref[:, kw:2 * kw] = proj(_W_GQA_V, _W_DIFF_Q).astype(BF16)

    cos_d = cd_ref[...]
    sin_d = sd_ref[...]
    dq = proj(_W_DIFF_Q, _W_DIFF_K)
    dq_ref[...] = (_rope(dq, cos_d, sin_d, DIFF_QK // 4) * DIFF_QK ** -0.5).astype(BF16)
    dk = proj(_W_DIFF_K, _W_DIFF_V)
    dk_ref[...] = _rope(dk, cos_d, sin_d, DIFF_QK // 4).astype(BF16)
    dv_ref[...] = proj(_W_DIFF_V, _W_IN_PAD).astype(BF16)


def _in_projection(x_all, modsel, norm_w, w_in_p, gmat, q_gain, k_gain, tables):
    bsz, s, d = x_all.shape
    tm = TOKEN_BLOCK
    cos_g, sin_g, cos_d, sin_d = tables
    row = lambda w: pl.BlockSpec((None, tm, w), lambda b, j: (b, j, 0))
    const = lambda shape: pl.BlockSpec(shape, lambda b, j: (0,) * len(shape))
    tab = lambda w: pl.BlockSpec((tm, w), lambda b, j: (j, 0))
    widths = [(256, F32), (256, BF16), (256, F32), (128, F32), (512, BF16), (256, BF16),
              (256, BF16), (256, BF16), (256, BF16)]
    return pl.pallas_call(
        _in_kernel,
        grid=(bsz, s // tm),
        in_specs=[row(d),
                  pl.BlockSpec((None, None, 8, d), lambda b, j: (b, jnp.minimum(j, 1), 0, 0)),
                  const((1, d)), const((d, _W_IN_PAD)), const((512, 512)),
                  const((1, 512)), const((1, 128)),
                  tab(512), tab(512), tab(256), tab(256)],
        out_specs=[row(w) for w, _ in widths],
        out_shape=[jax.ShapeDtypeStruct((bsz, s, w), dt) for w, dt in widths],
        compiler_params=_cparams("parallel", "parallel"),
        name="in_projection",
    )(x_all, modsel, norm_w, w_in_p, gmat, q_gain, k_gain, cos_g, sin_g, cos_d, sin_d)


def _split3(g):
    g1 = g.astype(BF16)
    r1 = g - g1.astype(F32)
    g2 = r1.astype(BF16)
    g3 = (r1 - g2.astype(F32)).astype(BF16)
    return jnp.concatenate([g1, g2, g3], axis=1)


def _gla_kernel(qkf_ref, vf_ref, lrf_ref, qkb_ref, vb_ref, lrb_ref, a2_ref, ab_ref,
                of_ref, ob_ref, st_ref):
    @pl.when(pl.program_id(0) == 0)
    def _():
        st_ref[...] = jnp.zeros_like(st_ref)

    n_b = qkf_ref.shape[0]
    c = GLA_CHUNK
    hk = GLA_HEADS * GLA_DK
    r_i = lax.broadcasted_iota(jnp.int32, (c, c), 0)
    c_i = lax.broadcasted_iota(jnp.int32, (c, c), 1)
    dirs = ((qkf_ref, vf_ref, lrf_ref, of_ref, c_i <= r_i, c - 1),
            (qkb_ref, vb_ref, lrb_ref, ob_ref, c_i >= r_i, 0))
    loaded = [[(qk_ref[b], lr_ref[b], v_ref[b], st_ref[d, b]) for b in range(n_b)]
              for d, (qk_ref, v_ref, lr_ref, _, _, _) in enumerate(dirs)]
    results = []
    for d, (_, _, _, o_ref, tri, last) in enumerate(dirs):
        tri_b = jnp.where(tri, 1.0, 0.0).astype(BF16)
        for b in range(n_b):
            qk, lr, v, st = loaded[d][b]
            pre = jnp.dot(lr.astype(BF16), a2_ref[d], preferred_element_type=F32) + ab_ref[d]
            log_a = (jnp.minimum(pre, 0.0) - jnp.log(1.0 + jnp.exp(-jnp.abs(pre)))) * (1.0 / GLA_GATE_NORM)
            cs = jnp.dot(tri_b, _split3(log_a), preferred_element_type=F32)
            bc = cs[:, 0:hk] + cs[:, hk:2 * hk] + cs[:, 2 * hk:3 * hk]
            tot = bc[last:last + 1, :]
            q = qk[:, 0:hk] * GLA_DK ** -0.5
            k = qk[:, hk:2 * hk]
            qd = (q * jnp.exp(bc)).astype(BF16)
            kd = (k * jnp.exp(-bc)).astype(BF16)
            kl = (k * jnp.exp(tot - bc)).astype(BF16)
            outs, kvs = [], []
            for h in range(GLA_HEADS):
                ks = slice(h * GLA_DK, (h + 1) * GLA_DK)
                v_h = v[:, h * GLA_DV:(h + 1) * GLA_DV]
                a = jnp.where(tri, _nt_dot(qd[:, ks], kd[:, ks]), 0.0).astype(BF16)
                outs.append(jnp.dot(a, v_h, preferred_element_type=F32)
                            + _nt_dot(qd[:, ks], st[:, ks].astype(BF16)))
                kvs.append(_tn_dot(v_h, kl[:, ks]))
            results.append((o_ref, d, b, jnp.concatenate(outs, axis=1),
                            st * jnp.exp(tot) + jnp.concatenate(kvs, axis=1)))
    for o_ref, d, b, o, st_new in results:
        o_ref[b] = o
        st_ref[d, b] = st_new


def _gla(gla_qk, gla_v, gla_lr, a2, ab, n_ctx_chunks):
    bsz, s, _ = gla_qk.shape
    c = GLA_CHUNK
    n = s // c
    fwd = lambda i: (0, i, 0)
    bwd = lambda i: (0, jnp.where(i < n_ctx_chunks, n_ctx_chunks - 1 - i, n + n_ctx_chunks - 1 - i), 0)
    blk = lambda w, m: pl.BlockSpec((bsz, c, w), m)
    hv = GLA_HEADS * GLA_DV
    return pl.pallas_call(
        _gla_kernel,
        grid=(n,),
        in_specs=[blk(256, fwd), blk(hv, fwd), blk(LANES, fwd),
                  blk(256, bwd), blk(hv, bwd), blk(LANES, bwd),
                  pl.BlockSpec((2, LANES, LANES), lambda i: (0, 0, 0)),
                  pl.BlockSpec((2, 1, LANES), lambda i: (0, 0, 0))],
        out_specs=[blk(hv, fwd), blk(hv, bwd)],
        out_shape=[jax.ShapeDtypeStruct((bsz, s, hv), F32)] * 2,
        scratch_shapes=[pltpu.VMEM((2, bsz, GLA_DV, GLA_HEADS * GLA_DK), F32)],
        compiler_params=_cparams("arbitrary"),
        name="gla_scan",
    )(gla_qk, gla_v, gla_lr, gla_qk, gla_v, gla_lr, a2, ab)


ATTN_V_ROWS = 80


def _probs_t(q, k):
    s = _nt_dot(k, q)
    return jnp.exp((s - jnp.max(s, axis=0, keepdims=True)).astype(BF16))


def _attend_t(vt, q, k):
    o = jnp.dot(vt, _probs_t(q, k), preferred_element_type=F32)
    hd = ATTN_V_ROWS - 16
    return o[0:hd, :] * (1.0 / o[hd:hd + 1, :])


def _attn_kernel(gq_ref, gkv_ref, gvt_ref, dq_ref, dk_ref, dvt_ref, lam_ref, dnw_ref, go_ref, do_ref,
                 *, n_ctx, out_scale, first_block):
    lam = (jnp.exp(jnp.sum(lam_ref[0:1, :] * lam_ref[1:2, :], axis=-1, keepdims=True))
           - jnp.exp(jnp.sum(lam_ref[2:3, :] * lam_ref[3:4, :], axis=-1, keepdims=True))
           + (1.0 - out_scale))
    vr = ATTN_V_ROWS

    def attend(nk):
        outs = []
        for kh in range(GQA_KV_HEADS):
            k = gkv_ref[0:nk, kh * GQA_HD:(kh + 1) * GQA_HD]
            vt = gvt_ref[kh * vr:(kh + 1) * vr, 0:nk]
            for g in range(GQA_GROUP):
                hs = slice((kh * GQA_GROUP + g) * GQA_HD, (kh * GQA_GROUP + g + 1) * GQA_HD)
                outs.append(_attend_t(vt, gq_ref[:, hs], k))
        go_ref[...] = jnp.concatenate(outs, axis=0).T.astype(BF16)
        outs = []
        for h in range(DIFF_HEADS):
            base = h * 2 * DIFF_QK
            vt = dvt_ref[h * vr:(h + 1) * vr, 0:nk]
            o = (_attend_t(vt, dq_ref[:, base:base + DIFF_QK], dk_ref[0:nk, base:base + DIFF_QK])
                 - lam * _attend_t(vt, dq_ref[:, base + DIFF_QK:base + 2 * DIFF_QK],
                                   dk_ref[0:nk, base + DIFF_QK:base + 2 * DIFF_QK]))
            ms = jnp.mean(o * o, axis=0, keepdims=True)
            outs.append(o * lax.rsqrt(ms + NORM_EPS) * (dnw_ref[...] * out_scale))
        do_ref[...] = jnp.concatenate(outs, axis=0).T.astype(BF16)

    if first_block == 0:
        @pl.when(pl.program_id(1) == 0)
        def _():
            attend(n_ctx)

        @pl.when(pl.program_id(1) > 0)
        def _():
            attend(gkv_ref.shape[0])
    else:
        attend(gkv_ref.shape[0])


def _values_t(v, heads):
    bsz, s, _ = v.shape
    vt = jnp.swapaxes(v, 1, 2).reshape(bsz, heads, -1, s)
    ones = jnp.ones((bsz, heads, ATTN_V_ROWS - vt.shape[2], s), v.dtype)
    return jnp.concatenate([vt, ones], axis=2).reshape(bsz, heads * ATTN_V_ROWS, s)


def _attention(gq, gkv, dq, dk, dv, lam_rows, diff_norm_col, n_ctx, lambda_init, ctx_out):
    bsz, s, _ = gq.shape
    tq = TOKEN_BLOCK
    assert n_ctx == tq
    first = 0 if ctx_out else 1
    s_out = s - first * tq
    gvt = _values_t(gkv[:, :, GQA_KV_HEADS * GQA_HD:], GQA_KV_HEADS)
    dvt = _values_t(dv, DIFF_HEADS)
    qrow = lambda w: pl.BlockSpec((None, tq, w), lambda b, j: (b, j + first, 0))
    orow = lambda w: pl.BlockSpec((None, tq, w), lambda b, j: (b, j, 0))
    full = lambda w: pl.BlockSpec((None, s, w), lambda b, j: (b, 0, 0))
    full_t = lambda r: pl.BlockSpec((None, r, s), lambda b, j: (b, 0, 0))
    return pl.pallas_call(
        functools.partial(_attn_kernel, n_ctx=n_ctx, out_scale=1.0 - lambda_init, first_block=first),
        grid=(bsz, s_out // tq),
        in_specs=[qrow(512), full(256), full_t(GQA_KV_HEADS * ATTN_V_ROWS),
                  qrow(256), full(256), full_t(DIFF_HEADS * ATTN_V_ROWS),
                  pl.BlockSpec((4, DIFF_QK), lambda b, j: (0, 0)),
                  pl.BlockSpec((DIFF_V, 1), lambda b, j: (0, 0))],
        out_specs=[orow(512), orow(256)],
        out_shape=[jax.ShapeDtypeStruct((bsz, s_out, 512), BF16),
                   jax.ShapeDtypeStruct((bsz, s_out, 256), BF16)],
        compiler_params=_cparams("parallel", "parallel"),
        name="attention",
    )(gq, gkv, gvt, dq, dk, dvt, lam_rows, diff_norm_col)


def _route(h, wr_ref):
    logits = [jnp.sum(h * wr_ref[e:e + 1, :], axis=-1, keepdims=True) for e in range(N_EXPERTS)]

    def top(ls):
        m = functools.reduce(jnp.maximum, ls)
        idx = jnp.full_like(m, float(N_EXPERTS))
        for e in reversed(range(N_EXPERTS)):
            idx = jnp.where(ls[e] == m, float(e), idx)
        return m, idx

    m1, i1 = top(logits)
    m2, i2 = top([jnp.where(i1 == float(e), -jnp.inf, logits[e]) for e in range(N_EXPERTS)])
    e2 = jnp.exp(m2 - m1)
    w1 = 1.0 / (1.0 + e2)
    w2 = e2 * w1
    lane = lax.broadcasted_iota(jnp.int32, (h.shape[0], LANES), 1)
    return jnp.where(lane == 0, w1, jnp.where(lane == 1, w2, jnp.where(lane == 2, i1,
                     jnp.where(lane == 3, i2, 0.0))))


def _out_kernel(*refs, with_router):
    if with_router:
        (x_ref, mod_ref, of_ref, ob_ref, r_ref, go_ref, do_ref, w_ref, g_ref, gnw_ref, nw_ref, wr_ref,
         x1_ref, h2_ref, route_ref) = refs
    else:
        (x_ref, mod_ref, of_ref, ob_ref, r_ref, go_ref, do_ref, w_ref, g_ref, gnw_ref, nw_ref,
         x1_ref, h2_ref) = refs
    mod = mod_ref[...]
    hv = GLA_HEADS * GLA_DV
    gw = GQA_HEADS * GQA_HD
    o = of_ref[...] + ob_ref[...]
    ms = jnp.dot((o * o).astype(BF16), g_ref[...], preferred_element_type=F32)
    gla = o * lax.rsqrt(ms + NORM_EPS) * gnw_ref[...] * _silu(r_ref[...])
    y = (jnp.dot(gla.astype(BF16), w_ref[0:hv, :], preferred_element_type=F32)
         + jnp.dot(go_ref[...], w_ref[hv:hv + gw, :], preferred_element_type=F32)
         + jnp.dot(do_ref[...], w_ref[hv + gw:, :], preferred_element_type=F32))
    x1 = x_ref[...] + mod[2:3] * y
    x1_ref[...] = x1
    h2 = _rms(x1, nw_ref[...]) * (1.0 + mod[4:5]) + mod[3:4]
    h2_ref[...] = h2.astype(h2_ref.dtype)
    if with_router:
        route_ref[...] = _route(h2, wr_ref)


def _out_projection(x_all, modsel, o_f, o_b, gla_r, gqa_o, diff_o, w_out, gmat, gla_norm_w, norm_ffn_w,
                    router_t, ctx_out):
    bsz, s, d = x_all.shape
    tm = TOKEN_BLOCK
    first = 0 if ctx_out else 1
    s_out = s - first * tm
    arow = lambda w: pl.BlockSpec((None, tm, w), lambda b, j: (b, j + first, 0))
    orow = lambda w: pl.BlockSpec((None, tm, w), lambda b, j: (b, j, 0))
    const = lambda shape: pl.BlockSpec(shape, lambda b, j: (0,) * len(shape))
    with_router = router_t is not None
    in_specs = [arow(d),
                pl.BlockSpec((None, None, 8, d), lambda b, j: (b, jnp.minimum(j + first, 1), 0, 0)),
                arow(256), arow(256), arow(256), orow(512), orow(256),
                const((d, d)), const((256, 256)), const((1, 256)), const((1, d))]
    args = [x_all, modsel, o_f, o_b, gla_r, gqa_o, diff_o, w_out, gmat, gla_norm_w, norm_ffn_w]
    out_specs = [orow(d), orow(d)]
    out_shape = [jax.ShapeDtypeStruct((bsz, s_out, d), F32),
                 jax.ShapeDtypeStruct((bsz, s_out, d), F32 if with_router else BF16)]
    if with_router:
        in_specs.append(const((N_EXPERTS, d)))
        args.append(router_t)
        out_specs.append(orow(LANES))
        out_shape.append(jax.ShapeDtypeStruct((bsz, s_out, LANES), F32))
    return pl.pallas_call(
        functools.partial(_out_kernel, with_router=with_router),
        grid=(bsz, s_out // tm),
        in_specs=in_specs, out_specs=out_specs, out_shape=out_shape,
        compiler_params=_cparams("parallel", "parallel"),
        name="out_projection",
    )(*args)


def _ffn_kernel(h_ref, x_ref, mod_ref, wg_ref, wu_ref, wd_ref, o_ref, acc_ref, *, n_ctx):
    f = pl.program_id(2)

    @pl.when(f == 0)
    def _():
        acc_ref[...] = jnp.zeros_like(acc_ref)

    h = h_ref[...]
    g = jnp.dot(h, wg_ref[...], preferred_element_type=F32)
    u = jnp.dot(h, wu_ref[...], preferred_element_type=F32)
    acc_ref[...] += jnp.dot((_silu(g) * u).astype(BF16), wd_ref[...], preferred_element_type=F32)

    @pl.when(f == pl.num_programs(2) - 1)
    def _():
        tm = h_ref.shape[0]
        row = pl.program_id(1) * tm + lax.broadcasted_iota(jnp.int32, (tm, 1), 0)
        gate = jnp.where(row < n_ctx, mod_ref[0, 5:6, :], mod_ref[1, 5:6, :])
        o_ref[...] = x_ref[...] + gate * acc_ref[...]


def _dense_ffn(h2, x1, modsel, wg, wu, wd, n_ctx):
    bsz, s, d = x1.shape
    ff = wg.shape[1]
    tm = s // 4
    tf = 256
    row = lambda: pl.BlockSpec((None, tm, d), lambda b, t, f: (b, t, 0))
    return pl.pallas_call(
        functools.partial(_ffn_kernel, n_ctx=n_ctx),
        grid=(bsz, s // tm, ff // tf),
        in_specs=[row(), row(),
                  pl.BlockSpec((None, 2, 8, d), lambda b, t, f: (b, 0, 0, 0)),
                  pl.BlockSpec((d, tf), lambda b, t, f: (0, f)),
                  pl.BlockSpec((d, tf), lambda b, t, f: (0, f)),
                  pl.BlockSpec((tf, d), lambda b, t, f: (f, 0))],
        out_specs=row(),
        out_shape=jax.ShapeDtypeStruct((bsz, s, d), F32),
        scratch_shapes=[pltpu.VMEM((tm, d), F32)],
        compiler_params=_cparams("parallel", "parallel", "arbitrary"),
        name="dense_ffn",
    )(h2, x1, modsel, wg, wu, wd)


MOE_TILE = 1024
MOE_SUB = 256
MOE_FF_BLOCK = 512


def _moe_kernel(te_ref, nv_ref, vr_ref, xs_ref, wg_ref, wu_ref, wd_ref, o_ref, acc_ref, xb_ref):
    f = pl.program_id(1)
    last_f = pl.num_programs(1) - 1
    valid = vr_ref[pl.program_id(0)]

    @pl.when(valid > 0)
    def _():
        wg = wg_ref[...].astype(BF16)
        wu = wu_ref[...].astype(BF16)
        wd = wd_ref[...].astype(BF16)
        for sb in range(MOE_TILE // MOE_SUB):
            rows = pl.ds(sb * MOE_SUB, MOE_SUB)

            @pl.when(sb * MOE_SUB < valid)
            def _():
                @pl.when(f == 0)
                def _():
                    acc_ref[rows, :] = jnp.zeros((MOE_SUB, acc_ref.shape[1]), F32)
                    xb_ref[rows, :] = xs_ref[rows, :].astype(BF16)

                h = xb_ref[rows, :]
                g = jnp.dot(h, wg, preferred_element_type=F32)
                u = jnp.dot(h, wu, preferred_element_type=F32)
                acc_ref[rows, :] += jnp.dot((_silu(g) * u).astype(BF16), wd, preferred_element_type=F32)

                @pl.when(f == last_f)
                def _():
                    o_ref[rows, :] = acc_ref[rows, :]

            @pl.when(jnp.logical_and(sb * MOE_SUB >= valid, f == last_f))
            def _():
                o_ref[rows, :] = jnp.zeros((MOE_SUB, o_ref.shape[1]), F32)

    @pl.when(jnp.logical_and(valid == 0, f == last_f))
    def _():
        o_ref[...] = jnp.zeros_like(o_ref)


def _moe_experts(tile_expert, n_valid, valid_rows, xs, wg, wu, wd):
    rows, d = xs.shape
    ff = wg.shape[2]
    tm = MOE_TILE
    tf = MOE_FF_BLOCK
    nf = ff // tf

    def tile(i, nv):
        return jnp.minimum(i, nv[0] - 1)

    def fidx(i, f, nv):
        return jnp.where(i < nv[0], f, nf - 1)

    return pl.pallas_call(
        _moe_kernel,
        grid_spec=pltpu.PrefetchScalarGridSpec(
            num_scalar_prefetch=3,
            grid=(rows // tm, nf),
            in_specs=[pl.BlockSpec((tm, d), lambda i, f, te, nv, vr: (tile(i, nv), 0)),
                      pl.BlockSpec((None, d, tf), lambda i, f, te, nv, vr: (te[tile(i, nv)], 0, fidx(i, f, nv))),
                      pl.BlockSpec((None, d, tf), lambda i, f, te, nv, vr: (te[tile(i, nv)], 0, fidx(i, f, nv))),
                      pl.BlockSpec((None, tf, d), lambda i, f, te, nv, vr: (te[tile(i, nv)], fidx(i, f, nv), 0))],
            out_specs=pl.BlockSpec((tm, d), lambda i, f, te, nv, vr: (i, 0)),
            scratch_shapes=[pltpu.VMEM((tm, d), F32), pltpu.VMEM((tm, d), BF16)]),
        out_shape=jax.ShapeDtypeStruct((rows, d), F32),
        compiler_params=_cparams("arbitrary", "arbitrary"),
        name="moe_experts",
    )(tile_expert, n_valid, valid_rows, xs, wg, wu, wd)


def _dispatch_kernel(pos_ref, h_ref, init_ref, xs_ref, sem):
    del init_ref
    tm = h_ref.shape[0]
    base = pl.program_id(0) * (2 * tm)
    for r in range(tm):
        for k in range(2):
            pltpu.make_async_copy(h_ref.at[pl.ds(r, 1), :],
                                  xs_ref.at[pl.ds(pos_ref[base + 2 * r + k], 1), :], sem).start()
    for k in range(2):
        pltpu.make_async_copy(h_ref, xs_ref.at[pl.ds(0, tm), :], sem).wait()


def _moe_dispatch(pos_flat, h2, rows):
    n, d = h2.shape
    tm = TOKEN_BLOCK
    return pl.pallas_call(
        _dispatch_kernel,
        grid_spec=pltpu.PrefetchScalarGridSpec(
            num_scalar_prefetch=1,
            grid=(n // tm,),
            in_specs=[pl.BlockSpec((tm, d), lambda i, pos: (i, 0)),
                      pl.BlockSpec(memory_space=pl.ANY)],
            out_specs=pl.BlockSpec(memory_space=pl.ANY),
            scratch_shapes=[pltpu.SemaphoreType.DMA(())]),
        out_shape=jax.ShapeDtypeStruct((rows, d), F32),
        input_output_aliases={2: 0},
        compiler_params=_cparams("arbitrary"),
        name="moe_dispatch",
    )(pos_flat, h2, jnp.zeros((rows, d), F32))


def _combine_kernel(x_ref, mod_ref, route_ref, ya_ref, yb_ref, nf_ref, o_ref):
    route = route_ref[...]
    moe = route[:, 0:1] * ya_ref[...] + route[:, 1:2] * yb_ref[...]
    x2 = x_ref[...] + mod_ref[5:6, :] * moe
    o_ref[...] = _rms(x2, nf_ref[...])


def _moe_combine(x1, modsel, route, ya, yb, norm_f_w):
    bsz, s, d = x1.shape
    tm = TOKEN_BLOCK
    row = lambda w: pl.BlockSpec((None, tm, w), lambda b, j: (b, j, 0))
    return pl.pallas_call(
        _combine_kernel,
        grid=(bsz, s // tm),
        in_specs=[row(d), pl.BlockSpec((None, None, 8, d), lambda b, j: (b, 1, 0, 0)),
                  row(LANES), row(d), row(d), pl.BlockSpec((1, d), lambda b, j: (0, 0))],
        out_specs=row(d),
        out_shape=jax.ShapeDtypeStruct((bsz, s, d), F32),
        compiler_params=_cparams("parallel", "parallel"),
        name="moe_combine",
    )(x1, modsel, route, ya, yb, norm_f_w)


def _final_norm_kernel(x_ref, nf_ref, o_ref):
    o_ref[...] = _rms(x_ref[...], nf_ref[...])


def _final_norm(x, norm_f_w, first_row):
    bsz, s, d = x.shape
    tm = TOKEN_BLOCK
    first = first_row // tm
    return pl.pallas_call(
        _final_norm_kernel,
        grid=(bsz, (s - first_row) // tm),
        in_specs=[pl.BlockSpec((None, tm, d), lambda b, j: (b, j + first, 0)),
                  pl.BlockSpec((1, d), lambda b, j: (0, 0))],
        out_specs=pl.BlockSpec((None, tm, d), lambda b, j: (b, j, 0)),
        out_shape=jax.ShapeDtypeStruct((bsz, s - first_row, d), F32),
        compiler_params=_cparams("parallel", "parallel"),
        name="final_norm",
    )(x, norm_f_w)


def _moe_plan(route, tm):
    n = route.shape[0]
    e_flat = route[:, 2:4].astype(jnp.int32).reshape(-1)
    onehot = (e_flat[:, None] == jnp.arange(N_EXPERTS, dtype=jnp.int32)[None, :]).astype(jnp.int32)
    csum = jnp.cumsum(onehot, axis=0)
    counts = csum[-1]
    padded = ((counts + tm - 1) // tm) * tm
    ends = jnp.cumsum(padded)
    pos = jnp.sum(onehot * (csum - 1 + (ends - padded)[None, :]), axis=1)
    rows = 2 * n + N_EXPERTS * tm
    n_tiles = rows // tm
    starts = jnp.arange(n_tiles, dtype=jnp.int32) * tm
    tile_expert = jnp.minimum(jnp.sum((ends[None, :] <= starts[:, None]).astype(jnp.int32), axis=1),
                              N_EXPERTS - 1)
    n_valid = (ends[-1] // tm).astype(jnp.int32).reshape(1)
    token_end = (ends - padded + counts)[tile_expert]
    valid_rows = jnp.clip(token_end - starts, 0, tm).astype(jnp.int32)
    return rows, pos.astype(jnp.int32), tile_expert, n_valid, valid_rows


def _rope_tables(n_ctx, seq, head_dim, copies):
    half = head_dim // 4
    t = np.arange(seq)
    freqs = ROPE_THETA ** (-jnp.arange(half, dtype=F32) / half)
    rows = jnp.asarray(t // GRID_W, F32)[:, None] * freqs[None, :]
    cols = jnp.asarray(t % GRID_W, F32)[:, None] * freqs[None, :]
    cos = jnp.concatenate([jnp.cos(rows)] * 2 + [jnp.cos(cols)] * 2, axis=1)
    sin = jnp.concatenate([-jnp.sin(rows), jnp.sin(rows), -jnp.sin(cols), jnp.sin(cols)], axis=1)
    cos = jnp.concatenate([jnp.ones((n_ctx, head_dim), F32), cos], axis=0)
    sin = jnp.concatenate([jnp.zeros((n_ctx, head_dim), F32), sin], axis=0)
    return jnp.tile(cos, (1, copies)), jnp.tile(sin, (1, copies))


def _group_mean_matrix(width, group):
    g = np.kron(np.eye(width // group, dtype=np.float32), np.ones((group, group), np.float32)) / group
    return jnp.asarray(g, BF16)


def kernel(x, c, ctx, c_ctx, w_mod, b_mod, norm_mix, norm_ffn, w_in, w_out, gla_a2_f, gla_ab_f, gla_a2_b,
           gla_ab_b, gla_norm, gqa_q_norm, gqa_k_norm, diff_lam_q1, diff_lam_k1, diff_lam_q2, diff_lam_k2,
           diff_norm, ffn_gate, ffn_up, ffn_down, moe_router, moe_gate, moe_up, moe_down, norm_f):
    bsz, seq, d = x.shape
    n_ctx = ctx.shape[1]
    depth = w_mod.shape[0]

    c_rows = jnp.concatenate([c, c_ctx[None, :], jnp.zeros((8 - bsz - 1, d), F32)], axis=0)
    mod = _modulation(c_rows, w_mod, b_mod).reshape(depth, 8, 6, d)
    mod = jnp.pad(mod, ((0, 0), (0, 0), (0, 2), (0, 0)))
    modsel = jnp.stack([jnp.broadcast_to(mod[:, bsz][:, None], (depth, bsz, 8, d)), mod[:, :bsz]], axis=2)

    tables = (_rope_tables(n_ctx, seq, GQA_HD, GQA_HEADS) + _rope_tables(n_ctx, seq, DIFF_QK, 2 * DIFF_HEADS))
    gmat = _group_mean_matrix(GQA_HEADS * GQA_HD, GQA_HD)
    gmat_v = _group_mean_matrix(GLA_HEADS * GLA_DV, GLA_DV)

    w_in_p = jnp.concatenate([w_in[:, :, :_W_IN_LR_END],
                              jnp.zeros((depth, d, _W_GQA_Q - _W_IN_LR_END), F32),
                              w_in[:, :, _W_IN_LR_END:]], axis=2).astype(BF16)
    w_out_b = w_out.astype(BF16)
    a2 = jnp.zeros((depth, 2, LANES, LANES), F32)
    a2 = a2.at[:, 0, 0:GLA_GATE_RANK].set(gla_a2_f).at[:, 1, GLA_GATE_RANK:2 * GLA_GATE_RANK].set(gla_a2_b)
    a2 = a2.astype(BF16)
    ab = jnp.stack([gla_ab_f, gla_ab_b], axis=1)[:, :, None, :]

    x_all = jnp.concatenate([ctx, x], axis=1)
    out = None
    for l in range(depth):
        ctx_out = l < depth - 1
        lambda_init = 0.8 - 0.6 * math.exp(-0.3 * l)
        gla_qk, gla_v, gla_r, gla_lr, gq, gkv, dq, dk, dv = _in_projection(
            x_all, modsel[l], norm_mix[l][None, :], w_in_p[l], gmat,
            jnp.tile(gqa_q_norm[l], GQA_HEADS)[None, :], jnp.tile(gqa_k_norm[l], GQA_KV_HEADS)[None, :], tables)
        o_f, o_b = _gla(gla_qk, gla_v, gla_lr, a2[l], ab[l], n_ctx // GLA_CHUNK)
        lam_rows = jnp.stack([diff_lam_q1[l], diff_lam_k1[l], diff_lam_q2[l], diff_lam_k2[l]], axis=0)
        gqa_o, diff_o = _attention(gq, gkv, dq, dk, dv, lam_rows, diff_norm[l][:, None], n_ctx, lambda_init,
                                   ctx_out)
        moe_layer = l % 2 == 1
        i = l // 2
        res = _out_projection(x_all, modsel[l], o_f, o_b, gla_r, gqa_o, diff_o, w_out_b[l], gmat_v,
                              jnp.tile(gla_norm[l], GLA_HEADS)[None, :], norm_ffn[l][None, :],
                              moe_router[i].T if moe_layer else None, ctx_out)
        if not moe_layer:
            x1, h2 = res
            if not ctx_out:
                raise NotImplementedError("dense feed-forward on the last layer")
            x_all = _dense_ffn(h2, x1, modsel[l], ffn_gate[i].astype(BF16), ffn_up[i].astype(BF16),
                               ffn_down[i].astype(BF16), n_ctx)
            if l == depth - 1:
                out = _final_norm(x_all, norm_f[None, :], n_ctx)
        else:
            x1, h2, route = res
            if ctx_out or l != depth - 1:
                raise NotImplementedError("expert feed-forward before the last layer")
            n = bsz * seq
            rows, pos, tile_expert, n_valid, valid_rows = _moe_plan(route.reshape(n, LANES), MOE_TILE)
            xs = _moe_dispatch(pos, h2.reshape(n, d), rows)
            ys = _moe_experts(tile_expert, n_valid, valid_rows, xs, moe_gate[i], moe_up[i], moe_down[i])
            ya = ys.at[pos[0::2]].get(mode="promise_in_bounds").reshape(bsz, seq, d)
            yb = ys.at[pos[1::2]].get(mode="promise_in_bounds").reshape(bsz, seq, d)
            out = _moe_combine(x1, modsel[l], route, ya, yb, norm_f[None, :])
    return out
```

```python
import functools
import math

import jax
import jax.numpy as jnp
import numpy as np
from jax import lax
from jax.experimental import pallas as pl
from jax.experimental.pallas import tpu as pltpu

F32 = jnp.float32
BF16 = jnp.bfloat16

GRID_W = 64
GLA_HEADS, GLA_DK, GLA_DV = 4, 32, 64
GLA_GATE_RANK = 16
GLA_GATE_NORM = 16.0
GLA_CHUNK = 64
GQA_HEADS, GQA_KV_HEADS, GQA_HD = 8, 2, 64
GQA_GROUP = GQA_HEADS // GQA_KV_HEADS
DIFF_HEADS, DIFF_QK, DIFF_V = 4, 32, 64
ROPE_THETA = 10000.0
NORM_EPS = 1e-6
N_EXPERTS = 8

LANES = 128
TOKEN_BLOCK = 256
VMEM_LIMIT = 56 * 1024 * 1024

_W_GLA_QK = 0
_W_GLA_V = 256
_W_GLA_R = 512
_W_GLA_LR = 768
_W_GQA_Q = 896
_W_GQA_K = 1408
_W_GQA_V = 1536
_W_DIFF_Q = 1664
_W_DIFF_K = 1920
_W_DIFF_V = 2176
_W_IN_PAD = 2432
_W_IN_LR_END = 800


def _cparams(*sem):
    return pltpu.CompilerParams(dimension_semantics=sem, vmem_limit_bytes=VMEM_LIMIT)


def _rms(x, w):
    ms = jnp.mean(x * x, axis=-1, keepdims=True)
    return x * lax.rsqrt(ms + NORM_EPS) * w


def _silu(x):
    return x * (1.0 / (1.0 + jnp.exp(-x)))


def _nt_dot(a, b):
    return lax.dot_general(a, b, (((1,), (1,)), ((), ())), preferred_element_type=F32)


def _tn_dot(a, b):
    return lax.dot_general(a, b, (((0,), (0,)), ((), ())), preferred_element_type=F32)


def _mod_kernel(c_ref, w_ref, b_ref, o_ref):
    s = _silu(c_ref[...])
    o_ref[...] = jnp.dot(s.astype(BF16), w_ref[...].astype(BF16),
                         preferred_element_type=F32) + b_ref[...]


def _modulation(c_rows, w_mod, b_mod):
    depth, d, n = w_mod.shape
    tn = 1536
    return pl.pallas_call(
        _mod_kernel,
        grid=(depth, n // tn),
        in_specs=[pl.BlockSpec((8, d), lambda l, j: (0, 0)),
                  pl.BlockSpec((None, d, tn), lambda l, j: (l, 0, j)),
                  pl.BlockSpec((None, 1, tn), lambda l, j: (l, 0, j))],
        out_specs=pl.BlockSpec((None, 8, tn), lambda l, j: (l, 0, j)),
        out_shape=jax.ShapeDtypeStruct((depth, 8, n), F32),
        compiler_params=_cparams("parallel", "parallel"),
        name="modulation",
    )(c_rows, w_mod, b_mod.reshape(depth, 1, n))


def _rope(x, cos, sin_signed, half):
    width = x.shape[1]
    lane = lax.broadcasted_iota(jnp.int32, x.shape, 1)
    first = (lane % (2 * half)) < half
    rot = jnp.where(first, pltpu.roll(x, width - half, 1), pltpu.roll(x, half, 1))
    return x * cos + rot * sin_signed


def _in_kernel(x_ref, mod_ref, nw_ref, w_ref, g_ref, qg_ref, kg_ref, cg_ref, sg_ref, cd_ref, sd_ref,
               glaqk_ref, glav_ref, glar_ref, glalr_ref, gq_ref, gkv_ref, dq_ref, dk_ref, dv_ref):
    mod = mod_ref[...]
    h = _rms(x_ref[...], nw_ref[...]) * (1.0 + mod[1:2]) + mod[0:1]
    hb = h.astype(BF16)

    def proj(a, b):
        return jnp.dot(hb, w_ref[:, a:b], preferred_element_type=F32)

    glaqk_ref[...] = proj(_W_GLA_QK, _W_GLA_V)
    glav_ref[...] = proj(_W_GLA_V, _W_GLA_R).astype(BF16)
    glar_ref[...] = proj(_W_GLA_R, _W_GLA_LR)
    glalr_ref[...] = proj(_W_GLA_LR, _W_GQA_Q)

    cos_g = cg_ref[...]
    sin_g = sg_ref[...]
    q = proj(_W_GQA_Q, _W_GQA_K)
    ms = jnp.dot((q * q).astype(BF16), g_ref[...], preferred_element_type=F32)
    qn = q * lax.rsqrt(ms + NORM_EPS) * qg_ref[...]
    gq_ref[...] = (_rope(qn, cos_g, sin_g, GQA_HD // 4) * GQA_HD ** -0.5).astype(BF16)

    kw = GQA_KV_HEADS * GQA_HD
    k = proj(_W_GQA_K, _W_GQA_V)
    ms = jnp.dot((k * k).astype(BF16), g_ref[0:kw, 0:kw], preferred_element_type=F32)
    kn = k * lax.rsqrt(ms + NORM_EPS) * kg_ref[...]
    gkv_ref[:, 0:kw] = _rope(kn, cos_g[:, 0:kw], sin_g[:, 0:kw], GQA_HD // 4).astype(BF16)
    gkv_ref[:, kw:2 * kw] = proj(_W_GQA_V, _W_DIFF_Q).astype(BF16)

    cos_d = cd_ref[...]
    sin_d = sd_ref[...]
    dq = proj(_W_DIFF_Q, _W_DIFF_K)
    dq_ref[...] = (_rope(dq, cos_d, sin_d, DIFF_QK // 4) * DIFF_QK ** -0.5).astype(BF16)
    dk = proj(_W_DIFF_K, _W_DIFF_V)
    dk_ref[...] = _rope(dk, cos_d, sin_d, DIFF_QK // 4).astype(BF16)
    dv_ref[...] = proj(_W_DIFF_V, _W_IN_PAD).astype(BF16)


def _in_projection(x_all, modsel, norm_w, w_in_p, gmat, q_gain, k_gain, tables):
    bsz, s, d = x_all.shape
    tm = TOKEN_BLOCK
    cos_g, sin_g, cos_d, sin_d = tables
    row = lambda w: pl.BlockSpec((None, tm, w), lambda b, j: (b, j, 0))
    const = lambda shape: pl.BlockSpec(shape, lambda b, j: (0,) * len(shape))
    tab = lambda w: pl.BlockSpec((tm, w), lambda b, j: (j, 0))
    widths = [(256, F32), (256, BF16), (256, F32), (128, F32), (512, BF16), (256, BF16),
              (256, BF16), (256, BF16), (256, BF16)]
    return pl.pallas_call(
        _in_kernel,
        grid=(bsz, s // tm),
        in_specs=[row(d),
                  pl.BlockSpec((None, None, 8, d), lambda b, j: (b, jnp.minimum(j, 1), 0, 0)),
                  const((1, d)), const((d, _W_IN_PAD)), const((512, 512)),
                  const((1, 512)), const((1, 128)),
                  tab(512), tab(512), tab(256), tab(256)],
        out_specs=[row(w) for w, _ in widths],
        out_shape=[jax.ShapeDtypeStruct((bsz, s, w), dt) for w, dt in widths],
        compiler_params=_cparams("parallel", "parallel"),
        name="in_projection",
    )(x_all, modsel, norm_w, w_in_p, gmat, q_gain, k_gain, cos_g, sin_g, cos_d, sin_d)


def _split3(g):
    g1 = g.astype(BF16)
    r1 = g - g1.astype(F32)
    g2 = r1.astype(BF16)
    g3 = (r1 - g2.astype(F32)).astype(BF16)
    return jnp.concatenate([g1, g2, g3], axis=1)


def _gla_kernel(qkf_ref, vf_ref, lrf_ref, qkb_ref, vb_ref, lrb_ref, a2_ref, ab_ref,
                of_ref, ob_ref, st_ref):
    @pl.when(pl.program_id(0) == 0)
    def _():
        st_ref[...] = jnp.zeros_like(st_ref)

    n_b = qkf_ref.shape[0]
    c = GLA_CHUNK
    hk = GLA_HEADS * GLA_DK
    hv = GLA_HEADS * GLA_DV
    r_i = lax.broadcasted_iota(jnp.int32, (c, c), 0)
    c_i = lax.broadcasted_iota(jnp.int32, (c, c), 1)
    head_of_k = lax.broadcasted_iota(jnp.int32, (c, hk), 1) // GLA_DK
    head_of_v = lax.broadcasted_iota(jnp.int32, (c, hv), 1) // GLA_DV
    dirs = ((qkf_ref, vf_ref, lrf_ref, of_ref, c_i <= r_i, c - 1),
            (qkb_ref, vb_ref, lrb_ref, ob_ref, c_i >= r_i, 0))
    for d, (qk_ref, v_ref, lr_ref, o_ref, tri, last) in enumerate(dirs):
        tri_b = jnp.where(tri, 1.0, 0.0).astype(BF16)
        tri_heads = jnp.concatenate([tri] * GLA_HEADS, axis=0)
        lr = jnp.concatenate([lr_ref[b] for b in range(n_b)], axis=0)
        pre = jnp.dot(lr.astype(BF16), a2_ref[d], preferred_element_type=F32) + ab_ref[d]
        log_a = (jnp.minimum(pre, 0.0) - jnp.log(1.0 + jnp.exp(-jnp.abs(pre)))) * (1.0 / GLA_GATE_NORM)
        parts = _split3(log_a)
        cs = jnp.dot(tri_b, jnp.concatenate([parts[b * c:(b + 1) * c] for b in range(n_b)], axis=1),
                     preferred_element_type=F32)
        for b in range(n_b):
            base = b * 3 * hk
            bc = cs[:, base:base + hk] + cs[:, base + hk:base + 2 * hk] + cs[:, base + 2 * hk:base + 3 * hk]
            tot = bc[last:last + 1, :]
            qk = qk_ref[b]
            q = qk[:, 0:hk] * GLA_DK ** -0.5
            k = qk[:, hk:2 * hk]
            qd = q * jnp.exp(bc)
            kd = (k * jnp.exp(-bc)).astype(BF16)
            kl = (k * jnp.exp(tot - bc)).astype(BF16)
            v = v_ref[b]
            st = st_ref[d, b]
            q_heads = jnp.concatenate([jnp.where(head_of_k == h, qd, 0.0) for h in range(GLA_HEADS)],
                                      axis=0).astype(BF16)
            a = jnp.where(tri_heads, _nt_dot(q_heads, kd), 0.0).astype(BF16)
            o_all = jnp.dot(a, v, preferred_element_type=F32) + _nt_dot(q_heads, st.astype(BF16))
            o = jnp.where(head_of_v == 0, o_all[0:c], 0.0)
            for h in range(1, GLA_HEADS):
                o = o + jnp.where(head_of_v == h, o_all[h * c:(h + 1) * c], 0.0)
            o_ref[b] = o
            st_ref[d, b] = st * jnp.exp(tot) + _tn_dot(v, kl)


def _gla(gla_qk, gla_v, gla_lr, a2, ab, n_ctx_chunks):
    bsz, s, _ = gla_qk.shape
    c = GLA_CHUNK
    n = s // c
    fwd = lambda i: (0, i, 0)
    bwd = lambda i: (0, jnp.where(i < n_ctx_chunks, n_ctx_chunks - 1 - i, n + n_ctx_chunks - 1 - i), 0)
    blk = lambda w, m: pl.BlockSpec((bsz, c, w), m)
    hv = GLA_HEADS * GLA_DV
    return pl.pallas_call(
        _gla_kernel,
        grid=(n,),
        in_specs=[blk(256, fwd), blk(hv, fwd), blk(LANES, fwd),
                  blk(256, bwd), blk(hv, bwd), blk(LANES, bwd),
                  pl.BlockSpec((2, LANES, LANES), lambda i: (0, 0, 0)),
                  pl.BlockSpec((2, 1, LANES), lambda i: (0, 0, 0))],
        out_specs=[blk(hv, fwd), blk(hv, bwd)],
        out_shape=[jax.ShapeDtypeStruct((bsz, s, hv), F32)] * 2,
        scratch_shapes=[pltpu.VMEM((2, bsz, hv, GLA_HEADS * GLA_DK), F32)],
        compiler_params=_cparams("arbitrary"),
        name="gla_scan",
    )(gla_qk, gla_v, gla_lr, gla_qk, gla_v, gla_lr, a2, ab)


def _softmax_parts(q, k):
    s = _nt_dot(q, k)
    p = jnp.exp(s - jnp.max(s, axis=-1, keepdims=True))
    return p.astype(BF16), 1.0 / jnp.sum(p, axis=-1, keepdims=True)


def _attn_kernel(gq_ref, gkv_ref, dq_ref, dk_ref, dv_ref, lam_ref, dnw_ref, go_ref, do_ref,
                 *, n_ctx, out_scale, first_block):
    lam = (jnp.exp(jnp.sum(lam_ref[0:1, :] * lam_ref[1:2, :], axis=-1, keepdims=True))
           - jnp.exp(jnp.sum(lam_ref[2:3, :] * lam_ref[3:4, :], axis=-1, keepdims=True))
           + (1.0 - out_scale))
    kw = GQA_KV_HEADS * GQA_HD

    def attend(nk):
        for kh in range(GQA_KV_HEADS):
            k = gkv_ref[0:nk, kh * GQA_HD:(kh + 1) * GQA_HD]
            v = gkv_ref[0:nk, kw + kh * GQA_HD:kw + (kh + 1) * GQA_HD]
            for g in range(GQA_GROUP):
                hs = slice((kh * GQA_GROUP + g) * GQA_HD, (kh * GQA_GROUP + g + 1) * GQA_HD)
                p, inv = _softmax_parts(gq_ref[:, hs], k)
                go_ref[:, hs] = (jnp.dot(p, v, preferred_element_type=F32) * inv).astype(BF16)
        for h in range(DIFF_HEADS):
            base = h * 2 * DIFF_QK
            v = dv_ref[0:nk, h * DIFF_V:(h + 1) * DIFF_V]
            p1, inv1 = _softmax_parts(dq_ref[:, base:base + DIFF_QK], dk_ref[0:nk, base:base + DIFF_QK])
            p2, inv2 = _softmax_parts(dq_ref[:, base + DIFF_QK:base + 2 * DIFF_QK],
                                      dk_ref[0:nk, base + DIFF_QK:base + 2 * DIFF_QK])
            o = (jnp.dot(p1, v, preferred_element_type=F32) * inv1
                 - jnp.dot(p2, v, preferred_element_type=F32) * (inv2 * lam))
            do_ref[:, h * DIFF_V:(h + 1) * DIFF_V] = (_rms(o, dnw_ref[...]) * out_scale).astype(BF16)

    if first_block == 0:
        @pl.when(pl.program_id(1) == 0)
        def _():
            attend(n_ctx)

        @pl.when(pl.program_id(1) > 0)
        def _():
            attend(gkv_ref.shape[0])
    else:
        attend(gkv_ref.shape[0])


def _attention(gq, gkv, dq, dk, dv, lam_rows, diff_norm_w, n_ctx, lambda_init, ctx_out):
    bsz, s, _ = gq.shape
    tq = TOKEN_BLOCK
    assert n_ctx == tq
    first = 0 if ctx_out else 1
    s_out = s - first * tq
    qrow = lambda w: pl.BlockSpec((None, tq, w), lambda b, j: (b, j + first, 0))
    orow = lambda w: pl.BlockSpec((None, tq, w), lambda b, j: (b, j, 0))
    full = lambda w: pl.BlockSpec((None, s, w), lambda b, j: (b, 0, 0))
    return pl.pallas_call(
        functools.partial(_attn_kernel, n_ctx=n_ctx, out_scale=1.0 - lambda_init, first_block=first),
        grid=(bsz, s_out // tq),
        in_specs=[qrow(512), full(256), qrow(256), full(256), full(256),
                  pl.BlockSpec((4, DIFF_QK), lambda b, j: (0, 0)),
                  pl.BlockSpec((1, DIFF_V), lambda b, j: (0, 0))],
        out_specs=[orow(512), orow(256)],
        out_shape=[jax.ShapeDtypeStruct((bsz, s_out, 512), BF16),
                   jax.ShapeDtypeStruct((bsz, s_out, 256), BF16)],
        compiler_params=_cparams("parallel", "parallel"),
        name="attention",
    )(gq, gkv, dq, dk, dv, lam_rows, diff_norm_w)


def _route(h, wr_ref):
    logits = [jnp.sum(h * wr_ref[e:e + 1, :], axis=-1, keepdims=True) for e in range(N_EXPERTS)]

    def top(ls):
        m = functools.reduce(jnp.maximum, ls)
        idx = jnp.full_like(m, float(N_EXPERTS))
        for e in reversed(range(N_EXPERTS)):
            idx = jnp.where(ls[e] == m, float(e), idx)
        return m, idx

    m1, i1 = top(logits)
    m2, i2 = top([jnp.where(i1 == float(e), -jnp.inf, logits[e]) for e in range(N_EXPERTS)])
    e2 = jnp.exp(m2 - m1)
    w1 = 1.0 / (1.0 + e2)
    w2 = e2 * w1
    lane = lax.broadcasted_iota(jnp.int32, (h.shape[0], LANES), 1)
    return jnp.where(lane == 0, w1, jnp.where(lane == 1, w2, jnp.where(lane == 2, i1,
                     jnp.where(lane == 3, i2, 0.0))))


def _out_kernel(*refs, with_router):
    if with_router:
        (x_ref, mod_ref, of_ref, ob_ref, r_ref, go_ref, do_ref, w_ref, g_ref, gnw_ref, nw_ref, wr_ref,
         x1_ref, h2_ref, route_ref) = refs
    else:
        (x_ref, mod_ref, of_ref, ob_ref, r_ref, go_ref, do_ref, w_ref, g_ref, gnw_ref, nw_ref,
         x1_ref, h2_ref) = refs
    mod = mod_ref[...]
    hv = GLA_HEADS * GLA_DV
    gw = GQA_HEADS * GQA_HD
    o = of_ref[...] + ob_ref[...]
    ms = jnp.dot((o * o).astype(BF16), g_ref[...], preferred_element_type=F32)
    gla = o * lax.rsqrt(ms + NORM_EPS) * gnw_ref[...] * _silu(r_ref[...])
    y = (jnp.dot(gla.astype(BF16), w_ref[0:hv, :], preferred_element_type=F32)
         + jnp.dot(go_ref[...], w_ref[hv:hv + gw, :], preferred_element_type=F32)
         + jnp.dot(do_ref[...], w_ref[hv + gw:, :], preferred_element_type=F32))
    x1 = x_ref[...] + mod[2:3] * y
    x1_ref[...] = x1
    h2 = _rms(x1, nw_ref[...]) * (1.0 + mod[4:5]) + mod[3:4]
    h2_ref[...] = h2.astype(h2_ref.dtype)
    if with_router:
        route_ref[...] = _route(h2, wr_ref)


def _out_projection(x_all, modsel, o_f, o_b, gla_r, gqa_o, diff_o, w_out, gmat, gla_norm_w, norm_ffn_w,
                    router_t, ctx_out):
    bsz, s, d = x_all.shape
    tm = TOKEN_BLOCK
    first = 0 if ctx_out else 1
    s_out = s - first * tm
    arow = lambda w: pl.BlockSpec((None, tm, w), lambda b, j: (b, j + first, 0))
    orow = lambda w: pl.BlockSpec((None, tm, w), lambda b, j: (b, j, 0))
    const = lambda shape: pl.BlockSpec(shape, lambda b, j: (0,) * len(shape))
    with_router = router_t is not None
    in_specs = [arow(d),
                pl.BlockSpec((None, None, 8, d), lambda b, j: (b, jnp.minimum(j + first, 1), 0, 0)),
                arow(256), arow(256), arow(256), orow(512), orow(256),
                const((d, d)), const((256, 256)), const((1, 256)), const((1, d))]
    args = [x_all, modsel, o_f, o_b, gla_r, gqa_o, diff_o, w_out, gmat, gla_norm_w, norm_ffn_w]
    out_specs = [orow(d), orow(d)]
    out_shape = [jax.ShapeDtypeStruct((bsz, s_out, d), F32),
                 jax.ShapeDtypeStruct((bsz, s_out, d), F32 if with_router else BF16)]
    if with_router:
        in_specs.append(const((N_EXPERTS, d)))
        args.append(router_t)
        out_specs.append(orow(LANES))
        out_shape.append(jax.ShapeDtypeStruct((bsz, s_out, LANES), F32))
    return pl.pallas_call(
        functools.partial(_out_kernel, with_router=with_router),
        grid=(bsz, s_out // tm),
        in_specs=in_specs, out_specs=out_specs, out_shape=out_shape,
        compiler_params=_cparams("parallel", "parallel"),
        name="out_projection",
    )(*args)


def _ffn_kernel(h_ref, x_ref, mod_ref, wg_ref, wu_ref, wd_ref, o_ref, acc_ref, *, n_ctx):
    f = pl.program_id(2)

    @pl.when(f == 0)
    def _():
        acc_ref[...] = jnp.zeros_like(acc_ref)

    h = h_ref[...]
    g = jnp.dot(h, wg_ref[...], preferred_element_type=F32)
    u = jnp.dot(h, wu_ref[...], preferred_element_type=F32)
    acc_ref[...] += jnp.dot((_silu(g) * u).astype(BF16), wd_ref[...], preferred_element_type=F32)

    @pl.when(f == pl.num_programs(2) - 1)
    def _():
        tm = h_ref.shape[0]
        row = pl.program_id(1) * tm + lax.broadcasted_iota(jnp.int32, (tm, 1), 0)
        gate = jnp.where(row < n_ctx, mod_ref[0, 5:6, :], mod_ref[1, 5:6, :])
        o_ref[...] = x_ref[...] + gate * acc_ref[...]


def _dense_ffn(h2, x1, modsel, wg, wu, wd, n_ctx):
    bsz, s, d = x1.shape
    ff = wg.shape[1]
    tm = s // 4
    tf = 256
    row = lambda: pl.BlockSpec((None, tm, d), lambda b, t, f: (b, t, 0))
    return pl.pallas_call(
        functools.partial(_ffn_kernel, n_ctx=n_ctx),
        grid=(bsz, s // tm, ff // tf),
        in_specs=[row(), row(),
                  pl.BlockSpec((None, 2, 8, d), lambda b, t, f: (b, 0, 0, 0)),
                  pl.BlockSpec((d, tf), lambda b, t, f: (0, f)),
                  pl.BlockSpec((d, tf), lambda b, t, f: (0, f)),
                  pl.BlockSpec((tf, d), lambda b, t, f: (f, 0))],
        out_specs=row(),
        out_shape=jax.ShapeDtypeStruct((bsz, s, d), F32),
        scratch_shapes=[pltpu.VMEM((tm, d), F32)],
        compiler_params=_cparams("parallel", "parallel", "arbitrary"),
        name="dense_ffn",
    )(h2, x1, modsel, wg, wu, wd)


MOE_TILE = 1024
MOE_SUB = 256
MOE_FF_BLOCK = 512


def _moe_kernel(te_ref, nv_ref, vr_ref, xs_ref, wg_ref, wu_ref, wd_ref, o_ref, acc_ref, xb_ref):
    f = pl.program_id(1)
    last_f = pl.num_programs(1) - 1
    valid = vr_ref[pl.program_id(0)]

    @pl.when(valid > 0)
    def _():
        wg = wg_ref[...].astype(BF16)
        wu = wu_ref[...].astype(BF16)
        wd = wd_ref[...].astype(BF16)
        for sb in range(MOE_TILE // MOE_SUB):
            rows = pl.ds(sb * MOE_SUB, MOE_SUB)

            @pl.when(sb * MOE_SUB < valid)
            def _():
                @pl.when(f == 0)
                def _():
                    acc_ref[rows, :] = jnp.zeros((MOE_SUB, acc_ref.shape[1]), F32)
                    xb_ref[rows, :] = xs_ref[rows, :].astype(BF16)

                h = xb_ref[rows, :]
                g = jnp.dot(h, wg, preferred_element_type=F32)
                u = jnp.dot(h, wu, preferred_element_type=F32)
                acc_ref[rows, :] += jnp.dot((_silu(g) * u).astype(BF16), wd, preferred_element_type=F32)

                @pl.when(f == last_f)
                def _():
                    o_ref[rows, :] = acc_ref[rows, :]

            @pl.when(jnp.logical_and(sb * MOE_SUB >= valid, f == last_f))
            def _():
                o_ref[rows, :] = jnp.zeros((MOE_SUB, o_ref.shape[1]), F32)

    @pl.when(jnp.logical_and(valid == 0, f == last_f))
    def _():
        o_ref[...] = jnp.zeros_like(o_ref)


def _moe_experts(tile_expert, n_valid, valid_rows, xs, wg, wu, wd):
    rows, d = xs.shape
    ff = wg.shape[2]
    tm = MOE_TILE
    tf = MOE_FF_BLOCK
    nf = ff // tf

    def tile(i, nv):
        return jnp.minimum(i, nv[0] - 1)

    def fidx(i, f, nv):
        return jnp.where(i < nv[0], f, nf - 1)

    return pl.pallas_call(
        _moe_kernel,
        grid_spec=pltpu.PrefetchScalarGridSpec(
            num_scalar_prefetch=3,
            grid=(rows // tm, nf),
            in_specs=[pl.BlockSpec((tm, d), lambda i, f, te, nv, vr: (tile(i, nv), 0)),
                      pl.BlockSpec((None, d, tf), lambda i, f, te, nv, vr: (te[tile(i, nv)], 0, fidx(i, f, nv))),
                      pl.BlockSpec((None, d, tf), lambda i, f, te, nv, vr: (te[tile(i, nv)], 0, fidx(i, f, nv))),
                      pl.BlockSpec((None, tf, d), lambda i, f, te, nv, vr: (te[tile(i, nv)], fidx(i, f, nv), 0))],
            out_specs=pl.BlockSpec((tm, d), lambda i, f, te, nv, vr: (i, 0)),
            scratch_shapes=[pltpu.VMEM((tm, d), F32), pltpu.VMEM((tm, d), BF16)]),
        out_shape=jax.ShapeDtypeStruct((rows, d), F32),
        compiler_params=_cparams("arbitrary", "arbitrary"),
        name="moe_experts",
    )(tile_expert, n_valid, valid_rows, xs, wg, wu, wd)


def _dispatch_kernel(pos_ref, h_ref, init_ref, xs_ref, sem):
    del init_ref
    tm = h_ref.shape[0]
    base = pl.program_id(0) * (2 * tm)
    for r in range(tm):
        for k in range(2):
            pltpu.make_async_copy(h_ref.at[pl.ds(r, 1), :],
                                  xs_ref.at[pl.ds(pos_ref[base + 2 * r + k], 1), :], sem).start()
    for k in range(2):
        pltpu.make_async_copy(h_ref, xs_ref.at[pl.ds(0, tm), :], sem).wait()


def _moe_dispatch(pos_flat, h2, rows):
    n, d = h2.shape
    tm = TOKEN_BLOCK
    return pl.pallas_call(
        _dispatch_kernel,
        grid_spec=pltpu.PrefetchScalarGridSpec(
            num_scalar_prefetch=1,
            grid=(n // tm,),
            in_specs=[pl.BlockSpec((tm, d), lambda i, pos: (i, 0)),
                      pl.BlockSpec(memory_space=pl.ANY)],
            out_specs=pl.BlockSpec(memory_space=pl.ANY),
            scratch_shapes=[pltpu.SemaphoreType.DMA(())]),
        out_shape=jax.ShapeDtypeStruct((rows, d), F32),
        input_output_aliases={2: 0},
        compiler_params=_cparams("arbitrary"),
        name="moe_dispatch",
    )(pos_flat, h2, jnp.zeros((rows, d), F32))


def _combine_kernel(x_ref, mod_ref, route_ref, ya_ref, yb_ref, nf_ref, o_ref):
    route = route_ref[...]
    moe = route[:, 0:1] * ya_ref[...] + route[:, 1:2] * yb_ref[...]
    x2 = x_ref[...] + mod_ref[5:6, :] * moe
    o_ref[...] = _rms(x2, nf_ref[...])


def _moe_combine(x1, modsel, route, ya, yb, norm_f_w):
    bsz, s, d = x1.shape
    tm = TOKEN_BLOCK
    row = lambda w: pl.BlockSpec((None, tm, w), lambda b, j: (b, j, 0))
    return pl.pallas_call(
        _combine_kernel,
        grid=(bsz, s // tm),
        in_specs=[row(d), pl.BlockSpec((None, None, 8, d), lambda b, j: (b, 1, 0, 0)),
                  row(LANES), row(d), row(d), pl.BlockSpec((1, d), lambda b, j: (0, 0))],
        out_specs=row(d),
        out_shape=jax.ShapeDtypeStruct((bsz, s, d), F32),
        compiler_params=_cparams("parallel", "parallel"),
        name="moe_combine",
    )(x1, modsel, route, ya, yb, norm_f_w)


def _final_norm_kernel(x_ref, nf_ref, o_ref):
    o_ref[...] = _rms(x_ref[...], nf_ref[...])


def _final_norm(x, norm_f_w, first_row):
    bsz, s, d = x.shape
    tm = TOKEN_BLOCK
    first = first_row // tm
    return pl.pallas_call(
        _final_norm_kernel,
        grid=(bsz, (s - first_row) // tm),
        in_specs=[pl.BlockSpec((None, tm, d), lambda b, j: (b, j + first, 0)),
                  pl.BlockSpec((1, d), lambda b, j: (0, 0))],
        out_specs=pl.BlockSpec((None, tm, d), lambda b, j: (b, j, 0)),
        out_shape=jax.ShapeDtypeStruct((bsz, s - first_row, d), F32),
        compiler_params=_cparams("parallel", "parallel"),
        name="final_norm",
    )(x, norm_f_w)


def _moe_plan(route, tm):
    n = route.shape[0]
    e_flat = route[:, 2:4].astype(jnp.int32).reshape(-1)
    onehot = (e_flat[:, None] == jnp.arange(N_EXPERTS, dtype=jnp.int32)[None, :]).astype(jnp.int32)
    csum = jnp.cumsum(onehot, axis=0)
    counts = csum[-1]
    padded = ((counts + tm - 1) // tm) * tm
    ends = jnp.cumsum(padded)
    pos = jnp.sum(onehot * (csum - 1 + (ends - padded)[None, :]), axis=1)
    rows = 2 * n + N_EXPERTS * tm
    n_tiles = rows // tm
    starts = jnp.arange(n_tiles, dtype=jnp.int32) * tm
    tile_expert = jnp.minimum(jnp.sum((ends[None, :] <= starts[:, None]).astype(jnp.int32), axis=1),
                              N_EXPERTS - 1)
    n_valid = (ends[-1] // tm).astype(jnp.int32).reshape(1)
    token_end = (ends - padded + counts)[tile_expert]
    valid_rows = jnp.clip(token_end - starts, 0, tm).astype(jnp.int32)
    return rows, pos.astype(jnp.int32), tile_expert, n_valid, valid_rows


def _rope_tables(n_ctx, seq, head_dim, copies):
    half = head_dim // 4
    t = np.arange(seq)
    freqs = ROPE_THETA ** (-jnp.arange(half, dtype=F32) / half)
    rows = jnp.asarray(t // GRID_W, F32)[:, None] * freqs[None, :]
    cols = jnp.asarray(t % GRID_W, F32)[:, None] * freqs[None, :]
    cos = jnp.concatenate([jnp.cos(rows)] * 2 + [jnp.cos(cols)] * 2, axis=1)
    sin = jnp.concatenate([-jnp.sin(rows), jnp.sin(rows), -jnp.sin(cols), jnp.sin(cols)], axis=1)
    cos = jnp.concatenate([jnp.ones((n_ctx, head_dim), F32), cos], axis=0)
    sin = jnp.concatenate([jnp.zeros((n_ctx, head_dim), F32), sin], axis=0)
    return jnp.tile(cos, (1, copies)), jnp.tile(sin, (1, copies))


def _group_mean_matrix(width, group):
    g = np.kron(np.eye(width // group, dtype=np.float32), np.ones((group, group), np.float32)) / group
    return jnp.asarray(g, BF16)


def kernel(x, c, ctx, c_ctx, w_mod, b_mod, norm_mix, norm_ffn, w_in, w_out, gla_a2_f, gla_ab_f, gla_a2_b,
           gla_ab_b, gla_norm, gqa_q_norm, gqa_k_norm, diff_lam_q1, diff_lam_k1, diff_lam_q2, diff_lam_k2,
           diff_norm, ffn_gate, ffn_up, ffn_down, moe_router, moe_gate, moe_up, moe_down, norm_f):
    bsz, seq, d = x.shape
    n_ctx = ctx.shape[1]
    depth = w_mod.shape[0]

    c_rows = jnp.concatenate([c, c_ctx[None, :], jnp.zeros((8 - bsz - 1, d), F32)], axis=0)
    mod = _modulation(c_rows, w_mod, b_mod).reshape(depth, 8, 6, d)
    mod = jnp.pad(mod, ((0, 0), (0, 0), (0, 2), (0, 0)))
    modsel = jnp.stack([jnp.broadcast_to(mod[:, bsz][:, None], (depth, bsz, 8, d)), mod[:, :bsz]], axis=2)

    tables = (_rope_tables(n_ctx, seq, GQA_HD, GQA_HEADS) + _rope_tables(n_ctx, seq, DIFF_QK, 2 * DIFF_HEADS))
    gmat = _group_mean_matrix(GQA_HEADS * GQA_HD, GQA_HD)
    gmat_v = _group_mean_matrix(GLA_HEADS * GLA_DV, GLA_DV)

    w_in_p = jnp.concatenate([w_in[:, :, :_W_IN_LR_END],
                              jnp.zeros((depth, d, _W_GQA_Q - _W_IN_LR_END), F32),
                              w_in[:, :, _W_IN_LR_END:]], axis=2).astype(BF16)
    w_out_b = w_out.astype(BF16)
    a2 = jnp.zeros((depth, 2, LANES, LANES), F32)
    a2 = a2.at[:, 0, 0:GLA_GATE_RANK].set(gla_a2_f).at[:, 1, GLA_GATE_RANK:2 * GLA_GATE_RANK].set(gla_a2_b)
    a2 = a2.astype(BF16)
    ab = jnp.stack([gla_ab_f, gla_ab_b], axis=1)[:, :, None, :]

    x_all = jnp.concatenate([ctx, x], axis=1)
    out = None
    for l in range(depth):
        ctx_out = l < depth - 1
        lambda_init = 0.8 - 0.6 * math.exp(-0.3 * l)
        gla_qk, gla_v, gla_r, gla_lr, gq, gkv, dq, dk, dv = _in_projection(
            x_all, modsel[l], norm_mix[l][None, :], w_in_p[l], gmat,
            jnp.tile(gqa_q_norm[l], GQA_HEADS)[None, :], jnp.tile(gqa_k_norm[l], GQA_KV_HEADS)[None, :], tables)
        o_f, o_b = _gla(gla_qk, gla_v, gla_lr, a2[l], ab[l], n_ctx // GLA_CHUNK)
        lam_rows = jnp.stack([diff_lam_q1[l], diff_lam_k1[l], diff_lam_q2[l], diff_lam_k2[l]], axis=0)
        gqa_o, diff_o = _attention(gq, gkv, dq, dk, dv, lam_rows, diff_norm[l][None, :], n_ctx, lambda_init,
                                   ctx_out)
        moe_layer = l % 2 == 1
        i = l // 2
        res = _out_projection(x_all, modsel[l], o_f, o_b, gla_r, gqa_o, diff_o, w_out_b[l], gmat_v,
                              jnp.tile(gla_norm[l], GLA_HEADS)[None, :], norm_ffn[l][None, :],
                              moe_router[i].T if moe_layer else None, ctx_out)
        if not moe_layer:
            x1, h2 = res
            if not ctx_out:
                raise NotImplementedError("dense feed-forward on the last layer")
            x_all = _dense_ffn(h2, x1, modsel[l], ffn_gate[i].astype(BF16), ffn_up[i].astype(BF16),
                               ffn_down[i].astype(BF16), n_ctx)
            if l == depth - 1:
                out = _final_norm(x_all, norm_f[None, :], n_ctx)
        else:
            x1, h2, route = res
            if ctx_out or l != depth - 1:
                raise NotImplementedError("expert feed-forward before the last layer")
            n = bsz * seq
            rows, pos, tile_expert, n_valid, valid_rows = _moe_plan(route.reshape(n, LANES), MOE_TILE)
            xs = _moe_dispatch(pos, h2.reshape(n, d), rows)
            ys = _moe_experts(tile_expert, n_valid, valid_rows, xs, moe_gate[i], moe_up[i], moe_down[i])
            ya = ys.at[pos[0::2]].get(mode="promise_in_bounds").reshape(bsz, seq, d)
            yb = ys.at[pos[1::2]].get(mode="promise_in_bounds").reshape(bsz, seq, d)
            out = _moe_combine(x1, modsel[l], route, ya, yb, norm_f[None, :])
    return out
```

```python
import functools
import math

import jax
import jax.numpy as jnp
import numpy as np
from jax import lax
from jax.experimental import pallas as pl
from jax.experimental.pallas import tpu as pltpu

F32 = jnp.float32
BF16 = jnp.bfloat16

GRID_W = 64
GLA_HEADS, GLA_DK, GLA_DV = 4, 32, 64
GLA_GATE_RANK = 16
GLA_GATE_NORM = 16.0
GLA_CHUNK = 64
GQA_HEADS, GQA_KV_HEADS, GQA_HD = 8, 2, 64
GQA_GROUP = GQA_HEADS // GQA_KV_HEADS
DIFF_HEADS, DIFF_QK, DIFF_V = 4, 32, 64
ROPE_THETA = 10000.0
NORM_EPS = 1e-6
N_EXPERTS = 8

LANES = 128
TOKEN_BLOCK = 256
VMEM_LIMIT = 56 * 1024 * 1024

_W_GLA_QK = 0
_W_GLA_V = 256
_W_GLA_R = 512
_W_GLA_LR = 768
_W_GQA_Q = 896
_W_GQA_K = 1408
_W_GQA_V = 1536
_W_DIFF_Q = 1664
_W_DIFF_K = 1920
_W_DIFF_V = 2176
_W_IN_PAD = 2432
_W_IN_LR_END = 800


def _cparams(*sem):
    return pltpu.CompilerParams(dimension_semantics=sem, vmem_limit_bytes=VMEM_LIMIT)


def _rms(x, w):
    ms = jnp.mean(x * x, axis=-1, keepdims=True)
    return x * lax.rsqrt(ms + NORM_EPS) * w


def _silu(x):
    return x * (1.0 / (1.0 + jnp.exp(-x)))


def _nt_dot(a, b):
    return lax.dot_general(a, b, (((1,), (1,)), ((), ())), preferred_element_type=F32)


def _tn_dot(a, b):
    return lax.dot_general(a, b, (((0,), (0,)), ((), ())), preferred_element_type=F32)


def _mod_kernel(c_ref, w_ref, b_ref, o_ref):
    s = _silu(c_ref[...])
    o_ref[...] = jnp.dot(s.astype(BF16), w_ref[...].astype(BF16),
                         preferred_element_type=F32) + b_ref[...]


def _modulation(c_rows, w_mod, b_mod):
    depth, d, n = w_mod.shape
    tn = 1536
    return pl.pallas_call(
        _mod_kernel,
        grid=(depth, n // tn),
        in_specs=[pl.BlockSpec((8, d), lambda l, j: (0, 0)),
                  pl.BlockSpec((None, d, tn), lambda l, j: (l, 0, j)),
                  pl.BlockSpec((None, 1, tn), lambda l, j: (l, 0, j))],
        out_specs=pl.BlockSpec((None, 8, tn), lambda l, j: (l, 0, j)),
        out_shape=jax.ShapeDtypeStruct((depth, 8, n), F32),
        compiler_params=_cparams("parallel", "parallel"),
        name="modulation",
    )(c_rows, w_mod, b_mod.reshape(depth, 1, n))


def _rope(x, cos, sin_signed, half):
    width = x.shape[1]
    lane = lax.broadcasted_iota(jnp.int32, x.shape, 1)
    first = (lane % (2 * half)) < half
    rot = jnp.where(first, pltpu.roll(x, width - half, 1), pltpu.roll(x, half, 1))
    return x * cos + rot * sin_signed


def _token_rows(refs, split_ctx):
    if not split_ctx:
        return refs[0][...], refs[1:]
    return jnp.where(pl.program_id(1) == 0, refs[0][...], refs[1][...]), refs[2:]


def _token_specs(x, ctx, tm, first):
    d = x.shape[-1]
    if ctx is None:
        return [pl.BlockSpec((None, tm, d), lambda b, j: (b, j + first, 0))], [x]
    assert ctx.shape[1] == tm and first == 0
    return ([pl.BlockSpec((None, tm, d), lambda b, j: (b, 0, 0)),
             pl.BlockSpec((None, tm, d), lambda b, j: (b, jnp.maximum(j - 1, 0), 0))], [ctx, x])


def _in_kernel(*refs, split_ctx):
    x, refs = _token_rows(refs, split_ctx)
    (mod_ref, nw_ref, w_ref, g_ref, qg_ref, kg_ref, cg_ref, sg_ref, cd_ref, sd_ref,
     glaqk_ref, glav_ref, glar_ref, glalr_ref, gq_ref, gkv_ref, dq_ref, dk_ref, dv_ref) = refs
    mod = mod_ref[...]
    h = _rms(x, nw_ref[...]) * (1.0 + mod[1:2]) + mod[0:1]
    hb = h.astype(BF16)

    def proj(a, b):
        return jnp.dot(hb, w_ref[:, a:b], preferred_element_type=F32)

    glaqk_ref[...] = proj(_W_GLA_QK, _W_GLA_V)
    glav_ref[...] = proj(_W_GLA_V, _W_GLA_R).astype(BF16)
    glar_ref[...] = proj(_W_GLA_R, _W_GLA_LR)
    glalr_ref[...] = proj(_W_GLA_LR, _W_GQA_Q)

    cos_g = cg_ref[...]
    sin_g = sg_ref[...]
    q = proj(_W_GQA_Q, _W_GQA_K)
    ms = jnp.dot((q * q).astype(BF16), g_ref[...], preferred_element_type=F32)
    qn = q * lax.rsqrt(ms + NORM_EPS) * qg_ref[...]
    gq_ref[...] = (_rope(qn, cos_g, sin_g, GQA_HD // 4) * GQA_HD ** -0.5).astype(BF16)

    kw = GQA_KV_HEADS * GQA_HD
    k = proj(_W_GQA_K, _W_GQA_V)
    ms = jnp.dot((k * k).astype(BF16), g_ref[0:kw, 0:kw], preferred_element_type=F32)
    kn = k * lax.rsqrt(ms + NORM_EPS) * kg_ref[...]
    gkv_ref[:, 0:kw] = _rope(kn, cos_g[:, 0:kw], sin_g[:, 0:kw], GQA_HD // 4).astype(BF16)
    gkv_ref[:, kw:2 * kw] = proj(_W_GQA_V, _W_DIFF_Q).astype(BF16)

    cos_d = cd_ref[...]
    sin_d = sd_ref[...]
    dq = proj(_W_DIFF_Q, _W_DIFF_K)
    dq_ref[...] = (_rope(dq, cos_d, sin_d, DIFF_QK // 4) * DIFF_QK ** -0.5).astype(BF16)
    dk = proj(_W_DIFF_K, _W_DIFF_V)
    dk_ref[...] = _rope(dk, cos_d, sin_d, DIFF_QK // 4).astype(BF16)
    dv_ref[...] = proj(_W_DIFF_V, _W_IN_PAD).astype(BF16)


def _in_projection(x, ctx, modsel, norm_w, w_in_p, gmat, q_gain, k_gain, tables):
    bsz, s, d = x.shape
    tm = TOKEN_BLOCK
    if ctx is not None:
        s += ctx.shape[1]
    x_specs, x_args = _token_specs(x, ctx, tm, 0)
    cos_g, sin_g, cos_d, sin_d = tables
    row = lambda w: pl.BlockSpec((None, tm, w), lambda b, j: (b, j, 0))
    const = lambda shape: pl.BlockSpec(shape, lambda b, j: (0,) * len(shape))
    tab = lambda w: pl.BlockSpec((tm, w), lambda b, j: (j, 0))
    widths = [(256, F32), (256, BF16), (256, F32), (128, F32), (512, BF16), (256, BF16),
              (256, BF16), (256, BF16), (256, BF16)]
    return pl.pallas_call(
        functools.partial(_in_kernel, split_ctx=ctx is not None),
        grid=(bsz, s // tm),
        in_specs=x_specs + [
                  pl.BlockSpec((None, None, 8, d), lambda b, j: (b, jnp.minimum(j, 1), 0, 0)),
                  const((1, d)), const((d, _W_IN_PAD)), const((512, 512)),
                  const((1, 512)), const((1, 128)),
                  tab(512), tab(512), tab(256), tab(256)],
        out_specs=[row(w) for w, _ in widths],
        out_shape=[jax.ShapeDtypeStruct((bsz, s, w), dt) for w, dt in widths],
        compiler_params=_cparams("parallel", "parallel"),
        name="in_projection",
    )(*x_args, modsel, norm_w, w_in_p, gmat, q_gain, k_gain, cos_g, sin_g, cos_d, sin_d)


def _split3(g):
    g1 = g.astype(BF16)
    r1 = g - g1.astype(F32)
    g2 = r1.astype(BF16)
    g3 = (r1 - g2.astype(F32)).astype(BF16)
    return jnp.concatenate([g1, g2, g3], axis=1)


def _gla_kernel(qkf_ref, vf_ref, lrf_ref, qkb_ref, vb_ref, lrb_ref, a2_ref, ab_ref,
                of_ref, ob_ref, st_ref):
    @pl.when(pl.program_id(0) == 0)
    def _():
        st_ref[...] = jnp.zeros_like(st_ref)

    n_b = qkf_ref.shape[0]
    c = GLA_CHUNK
    hk = GLA_HEADS * GLA_DK
    hv = GLA_HEADS * GLA_DV
    r_i = lax.broadcasted_iota(jnp.int32, (c, c), 0)
    c_i = lax.broadcasted_iota(jnp.int32, (c, c), 1)
    head_of_k = lax.broadcasted_iota(jnp.int32, (c, hk), 1) // GLA_DK
    head_of_v = lax.broadcasted_iota(jnp.int32, (c, hv), 1) // GLA_DV
    dirs = ((qkf_ref, vf_ref, lrf_ref, of_ref, c_i <= r_i, c - 1),
            (qkb_ref, vb_ref, lrb_ref, ob_ref, c_i >= r_i, 0))
    for d, (qk_ref, v_ref, lr_ref, o_ref, tri, last) in enumerate(dirs):
        tri_b = jnp.where(tri, 1.0, 0.0).astype(BF16)
        tri_heads = jnp.concatenate([tri] * GLA_HEADS, axis=0)
        lr = jnp.concatenate([lr_ref[b] for b in range(n_b)], axis=0)
        pre = jnp.dot(lr.astype(BF16), a2_ref[d], preferred_element_type=F32) + ab_ref[d]
        log_a = (jnp.minimum(pre, 0.0) - jnp.log(1.0 + jnp.exp(-jnp.abs(pre)))) * (1.0 / GLA_GATE_NORM)
        parts = _split3(log_a)
        cs = jnp.dot(tri_b, jnp.concatenate([parts[b * c:(b + 1) * c] for b in range(n_b)], axis=1),
                     preferred_element_type=F32)
        for b in range(n_b):
            base = b * 3 * hk
            bc = cs[:, base:base + hk] + cs[:, base + hk:base + 2 * hk] + cs[:, base + 2 * hk:base + 3 * hk]
            tot = bc[last:last + 1, :]
            qk = qk_ref[b]
            q = qk[:, 0:hk] * GLA_DK ** -0.5
            k = qk[:, hk:2 * hk]
            qd = q * jnp.exp(bc)
            kd = (k * jnp.exp(-bc)).astype(BF16)
            kl = (k * jnp.exp(tot - bc)).astype(BF16)
            v = v_ref[b]
            st = st_ref[d, b]
            q_heads = jnp.concatenate([jnp.where(head_of_k == h, qd, 0.0) for h in range(GLA_HEADS)],
                                      axis=0).astype(BF16)
            a = jnp.where(tri_heads, _nt_dot(q_heads, kd), 0.0).astype(BF16)
            o_all = jnp.dot(a, v, preferred_element_type=F32) + _nt_dot(q_heads, st.astype(BF16))
            o = jnp.where(head_of_v == 0, o_all[0:c], 0.0)
            for h in range(1, GLA_HEADS):
                o = o + jnp.where(head_of_v == h, o_all[h * c:(h + 1) * c], 0.0)
            o_ref[b] = o
            st_ref[d, b] = st * jnp.exp(tot) + _tn_dot(v, kl)


def _gla(gla_qk, gla_v, gla_lr, a2, ab, n_ctx_chunks):
    bsz, s, _ = gla_qk.shape
    c = GLA_CHUNK
    n = s // c
    fwd = lambda i: (0, i, 0)
    bwd = lambda i: (0, jnp.where(i < n_ctx_chunks, n_ctx_chunks - 1 - i, n + n_ctx_chunks - 1 - i), 0)
    blk = lambda w, m: pl.BlockSpec((bsz, c, w), m)
    hv = GLA_HEADS * GLA_DV
    return pl.pallas_call(
        _gla_kernel,
        grid=(n,),
        in_specs=[blk(256, fwd), blk(hv, fwd), blk(LANES, fwd),
                  blk(256, bwd), blk(hv, bwd), blk(LANES, bwd),
                  pl.BlockSpec((2, LANES, LANES), lambda i: (0, 0, 0)),
                  pl.BlockSpec((2, 1, LANES), lambda i: (0, 0, 0))],
        out_specs=[blk(hv, fwd), blk(hv, bwd)],
        out_shape=[jax.ShapeDtypeStruct((bsz, s, hv), F32)] * 2,
        scratch_shapes=[pltpu.VMEM((2, bsz, hv, GLA_HEADS * GLA_DK), F32)],
        compiler_params=_cparams("arbitrary"),
        name="gla_scan",
    )(gla_qk, gla_v, gla_lr, gla_qk, gla_v, gla_lr, a2, ab)


def _softmax_attend(q, k, v_ones):
    s = _nt_dot(q, k)
    p = jnp.exp((s - jnp.max(s, axis=-1, keepdims=True)).astype(BF16))
    o = jnp.dot(p, v_ones, preferred_element_type=F32)
    hd = v_ones.shape[1] // 2
    return o[:, 0:hd] * (1.0 / o[:, hd:hd + 1])


def _attn_kernel(gq_ref, gk_ref, gv_ref, dq_ref, dk_ref, dv_ref, lam_ref, dnw_ref, go_ref, do_ref,
                 *, n_ctx, out_scale, first_block):
    lam = (jnp.exp(jnp.sum(lam_ref[0:1, :] * lam_ref[1:2, :], axis=-1, keepdims=True))
           - jnp.exp(jnp.sum(lam_ref[2:3, :] * lam_ref[3:4, :], axis=-1, keepdims=True))
           + (1.0 - out_scale))
    tq = gq_ref.shape[0]

    def attend(nk):
        for kh in range(GQA_KV_HEADS):
            heads = [slice((kh * GQA_GROUP + g) * GQA_HD, (kh * GQA_GROUP + g + 1) * GQA_HD)
                     for g in range(GQA_GROUP)]
            q = jnp.concatenate([gq_ref[:, hs] for hs in heads], axis=0)
            o = _softmax_attend(q, gk_ref[0:nk, kh * GQA_HD:(kh + 1) * GQA_HD],
                                gv_ref[0:nk, 2 * kh * GQA_HD:2 * (kh + 1) * GQA_HD])
            for g, hs in enumerate(heads):
                go_ref[:, hs] = o[g * tq:(g + 1) * tq].astype(BF16)
        for h in range(DIFF_HEADS):
            base = h * 2 * DIFF_QK
            v = dv_ref[0:nk, 2 * h * DIFF_V:2 * (h + 1) * DIFF_V]
            o = (_softmax_attend(dq_ref[:, base:base + DIFF_QK], dk_ref[0:nk, base:base + DIFF_QK], v)
                 - lam * _softmax_attend(dq_ref[:, base + DIFF_QK:base + 2 * DIFF_QK],
                                         dk_ref[0:nk, base + DIFF_QK:base + 2 * DIFF_QK], v))
            do_ref[:, h * DIFF_V:(h + 1) * DIFF_V] = (_rms(o, dnw_ref[...]) * out_scale).astype(BF16)

    if first_block == 0:
        @pl.when(pl.program_id(1) == 0)
        def _():
            attend(n_ctx)

        @pl.when(pl.program_id(1) > 0)
        def _():
            attend(gk_ref.shape[0])
    else:
        attend(gk_ref.shape[0])


def _with_ones(v, head_dim):
    bsz, s, w = v.shape
    vh = v.reshape(bsz, s, w // head_dim, head_dim)
    return jnp.concatenate([vh, jnp.ones_like(vh)], axis=3).reshape(bsz, s, 2 * w)


def _attention(gq, gkv, dq, dk, dv, lam_rows, diff_norm_w, n_ctx, lambda_init, ctx_out):
    bsz, s, _ = gq.shape
    tq = TOKEN_BLOCK
    assert n_ctx == tq
    first = 0 if ctx_out else 1
    s_out = s - first * tq
    kw = GQA_KV_HEADS * GQA_HD
    gk = gkv[:, :, 0:kw]
    gv = _with_ones(gkv[:, :, kw:], GQA_HD)
    dv = _with_ones(dv, DIFF_V)
    qrow = lambda w: pl.BlockSpec((None, tq, w), lambda b, j: (b, j + first, 0))
    orow = lambda w: pl.BlockSpec((None, tq, w), lambda b, j: (b, j, 0))
    full = lambda w: pl.BlockSpec((None, s, w), lambda b, j: (b, 0, 0))
    return pl.pallas_call(
        functools.partial(_attn_kernel, n_ctx=n_ctx, out_scale=1.0 - lambda_init, first_block=first),
        grid=(bsz, s_out // tq),
        in_specs=[qrow(512), full(kw), full(2 * kw), qrow(256), full(256), full(512),
                  pl.BlockSpec((4, DIFF_QK), lambda b, j: (0, 0)),
                  pl.BlockSpec((1, DIFF_V), lambda b, j: (0, 0))],
        out_specs=[orow(512), orow(256)],
        out_shape=[jax.ShapeDtypeStruct((bsz, s_out, 512), BF16),
                   jax.ShapeDtypeStruct((bsz, s_out, 256), BF16)],
        compiler_params=_cparams("parallel", "parallel"),
        name="attention",
    )(gq, gk, gv, dq, dk, dv, lam_rows, diff_norm_w)


def _route(h, wr_ref):
    logits = [jnp.sum(h * wr_ref[e:e + 1, :], axis=-1, keepdims=True) for e in range(N_EXPERTS)]

    def top(ls):
        m = functools.reduce(jnp.maximum, ls)
        idx = jnp.full_like(m, float(N_EXPERTS))
        for e in reversed(range(N_EXPERTS)):
            idx = jnp.where(ls[e] == m, float(e), idx)
        return m, idx

    m1, i1 = top(logits)
    m2, i2 = top([jnp.where(i1 == float(e), -jnp.inf, logits[e]) for e in range(N_EXPERTS)])
    e2 = jnp.exp(m2 - m1)
    w1 = 1.0 / (1.0 + e2)
    w2 = e2 * w1
    lane = lax.broadcasted_iota(jnp.int32, (h.shape[0], LANES), 1)
    return jnp.where(lane == 0, w1, jnp.where(lane == 1, w2, jnp.where(lane == 2, i1,
                     jnp.where(lane == 3, i2, 0.0))))


def _out_kernel(*refs, with_router, split_ctx):
    x, refs = _token_rows(refs, split_ctx)
    if with_router:
        (mod_ref, of_ref, ob_ref, r_ref, go_ref, do_ref, w_ref, g_ref, gnw_ref, nw_ref, wr_ref,
         x1_ref, h2_ref, route_ref) = refs
    else:
        (mod_ref, of_ref, ob_ref, r_ref, go_ref, do_ref, w_ref, g_ref, gnw_ref, nw_ref,
         x1_ref, h2_ref) = refs
    mod = mod_ref[...]
    hv = GLA_HEADS * GLA_DV
    gw = GQA_HEADS * GQA_HD
    o = of_ref[...] + ob_ref[...]
    ms = jnp.dot((o * o).astype(BF16), g_ref[...], preferred_element_type=F32)
    gla = o * lax.rsqrt(ms + NORM_EPS) * gnw_ref[...] * _silu(r_ref[...])
    y = (jnp.dot(gla.astype(BF16), w_ref[0:hv, :], preferred_element_type=F32)
         + jnp.dot(go_ref[...], w_ref[hv:hv + gw, :], preferred_element_type=F32)
         + jnp.dot(do_ref[...], w_ref[hv + gw:, :], preferred_element_type=F32))
    x1 = x + mod[2:3] * y
    x1_ref[...] = x1
    h2 = _rms(x1, nw_ref[...]) * (1.0 + mod[4:5]) + mod[3:4]
    h2_ref[...] = h2.astype(h2_ref.dtype)
    if with_router:
        route_ref[...] = _route(h2, wr_ref)


def _out_projection(x, ctx, modsel, o_f, o_b, gla_r, gqa_o, diff_o, w_out, gmat, gla_norm_w, norm_ffn_w,
                    router_t, ctx_out):
    bsz, s, d = x.shape
    tm = TOKEN_BLOCK
    if ctx is not None:
        s += ctx.shape[1]
    first = 0 if ctx_out else 1
    s_out = s - first * tm
    x_specs, x_args = _token_specs(x, ctx, tm, first)
    arow = lambda w: pl.BlockSpec((None, tm, w), lambda b, j: (b, j + first, 0))
    orow = lambda w: pl.BlockSpec((None, tm, w), lambda b, j: (b, j, 0))
    const = lambda shape: pl.BlockSpec(shape, lambda b, j: (0,) * len(shape))
    with_router = router_t is not None
    in_specs = x_specs + [
                pl.BlockSpec((None, None, 8, d), lambda b, j: (b, jnp.minimum(j + first, 1), 0, 0)),
                arow(256), arow(256), arow(256), orow(512), orow(256),
                const((d, d)), const((256, 256)), const((1, 256)), const((1, d))]
    args = x_args + [modsel, o_f, o_b, gla_r, gqa_o, diff_o, w_out, gmat, gla_norm_w, norm_ffn_w]
    out_specs = [orow(d), orow(d)]
    out_shape = [jax.ShapeDtypeStruct((bsz, s_out, d), F32),
                 jax.ShapeDtypeStruct((bsz, s_out, d), F32 if with_router else BF16)]
    if with_router:
        in_specs.append(const((N_EXPERTS, d)))
        args.append(router_t)
        out_specs.append(orow(LANES))
        out_shape.append(jax.ShapeDtypeStruct((bsz, s_out, LANES), F32))
    return pl.pallas_call(
        functools.partial(_out_kernel, with_router=with_router, split_ctx=ctx is not None),
        grid=(bsz, s_out // tm),
        in_specs=in_specs, out_specs=out_specs, out_shape=out_shape,
        compiler_params=_cparams("parallel", "parallel"),
        name="out_projection",
    )(*args)


def _ffn_kernel(h_ref, x_ref, mod_ref, wg_ref, wu_ref, wd_ref, o_ref, acc_ref, *, n_ctx):
    f = pl.program_id(2)

    @pl.when(f == 0)
    def _():
        acc_ref[...] = jnp.zeros_like(acc_ref)

    h = h_ref[...]
    g = jnp.dot(h, wg_ref[...], preferred_element_type=F32)
    u = jnp.dot(h, wu_ref[...], preferred_element_type=F32)
    acc_ref[...] += jnp.dot((_silu(g) * u).astype(BF16), wd_ref[...], preferred_element_type=F32)

    @pl.when(f == pl.num_programs(2) - 1)
    def _():
        tm = h_ref.shape[0]
        row = pl.program_id(1) * tm + lax.broadcasted_iota(jnp.int32, (tm, 1), 0)
        gate = jnp.where(row < n_ctx, mod_ref[0, 5:6, :], mod_ref[1, 5:6, :])
        o_ref[...] = x_ref[...] + gate * acc_ref[...]


def _dense_ffn(h2, x1, modsel, wg, wu, wd, n_ctx):
    bsz, s, d = x1.shape
    ff = wg.shape[1]
    tm = s // 8
    tf = ff // 2
    row = lambda: pl.BlockSpec((None, tm, d), lambda b, t, f: (b, t, 0))
    return pl.pallas_call(
        functools.partial(_ffn_kernel, n_ctx=n_ctx),
        grid=(bsz, s // tm, ff // tf),
        in_specs=[row(), row(),
                  pl.BlockSpec((None, 2, 8, d), lambda b, t, f: (b, 0, 0, 0)),
                  pl.BlockSpec((d, tf), lambda b, t, f: (0, f)),
                  pl.BlockSpec((d, tf), lambda b, t, f: (0, f)),
                  pl.BlockSpec((tf, d), lambda b, t, f: (f, 0))],
        out_specs=row(),
        out_shape=jax.ShapeDtypeStruct((bsz, s, d), F32),
        scratch_shapes=[pltpu.VMEM((tm, d), F32)],
        compiler_params=_cparams("parallel", "parallel", "arbitrary"),
        name="dense_ffn",
    )(h2, x1, modsel, wg, wu, wd)


MOE_TILE = 1024
MOE_SUB = 256
MOE_FF_BLOCK = 512


def _moe_kernel(te_ref, nv_ref, vr_ref, xs_ref, wg_ref, wu_ref, wd_ref, o_ref, xb_ref):
    f = pl.program_id(1)
    last_f = pl.num_programs(1) - 1
    valid = vr_ref[pl.program_id(0)]

    @pl.when(valid > 0)
    def _():
        wg = wg_ref[...].astype(BF16)
        wu = wu_ref[...].astype(BF16)
        wd = wd_ref[...].astype(BF16)
        for sb in range(MOE_TILE // MOE_SUB):
            rows = pl.ds(sb * MOE_SUB, MOE_SUB)

            @pl.when(sb * MOE_SUB < valid)
            def _():
                @pl.when(f == 0)
                def _():
                    o_ref[rows, :] = jnp.zeros((MOE_SUB, o_ref.shape[1]), F32)
                    xb_ref[rows, :] = xs_ref[rows, :].astype(BF16)

                h = xb_ref[rows, :]
                g = jnp.dot(h, wg, preferred_element_type=F32)
                u = jnp.dot(h, wu, preferred_element_type=F32)
                o_ref[rows, :] += jnp.dot((_silu(g) * u).astype(BF16), wd, preferred_element_type=F32)

            @pl.when(jnp.logical_and(sb * MOE_SUB >= valid, f == last_f))
            def _():
                o_ref[rows, :] = jnp.zeros((MOE_SUB, o_ref.shape[1]), F32)

    @pl.when(jnp.logical_and(valid == 0, f == last_f))
    def _():
        o_ref[...] = jnp.zeros_like(o_ref)


def _moe_experts(tile_expert, n_valid, valid_rows, xs, wg, wu, wd):
    rows, d = xs.shape
    ff = wg.shape[2]
    tm = MOE_TILE
    tf = MOE_FF_BLOCK
    nf = ff // tf

    def tile(i, nv):
        return jnp.minimum(i, nv[0] - 1)

    def fidx(i, f, nv):
        return jnp.where(i < nv[0], f, nf - 1)

    return pl.pallas_call(
        _moe_kernel,
        grid_spec=pltpu.PrefetchScalarGridSpec(
            num_scalar_prefetch=3,
            grid=(rows // tm, nf),
            in_specs=[pl.BlockSpec((tm, d), lambda i, f, te, nv, vr: (tile(i, nv), 0)),
                      pl.BlockSpec((None, d, tf), lambda i, f, te, nv, vr: (te[tile(i, nv)], 0, fidx(i, f, nv))),
                      pl.BlockSpec((None, d, tf), lambda i, f, te, nv, vr: (te[tile(i, nv)], 0, fidx(i, f, nv))),
                      pl.BlockSpec((None, tf, d), lambda i, f, te, nv, vr: (te[tile(i, nv)], fidx(i, f, nv), 0))],
            out_specs=pl.BlockSpec((tm, d), lambda i, f, te, nv, vr: (i, 0)),
            scratch_shapes=[pltpu.VMEM((tm, d), BF16)]),
        out_shape=jax.ShapeDtypeStruct((rows, d), F32),
        compiler_params=_cparams("arbitrary", "arbitrary"),
        name="moe_experts",
    )(tile_expert, n_valid, valid_rows, xs, wg, wu, wd)


def _dispatch_kernel(pos_ref, h_ref, init_ref, xs_ref, sem):
    del init_ref
    tm = h_ref.shape[0]
    base = pl.program_id(0) * (2 * tm)
    for r in range(tm):
        for k in range(2):
            pltpu.make_async_copy(h_ref.at[pl.ds(r, 1), :],
                                  xs_ref.at[pl.ds(pos_ref[base + 2 * r + k], 1), :], sem).start()
    for k in range(2):
        pltpu.make_async_copy(h_ref, xs_ref.at[pl.ds(0, tm), :], sem).wait()


def _moe_dispatch(pos_flat, h2, rows):
    n, d = h2.shape
    tm = TOKEN_BLOCK
    return pl.pallas_call(
        _dispatch_kernel,
        grid_spec=pltpu.PrefetchScalarGridSpec(
            num_scalar_prefetch=1,
            grid=(n // tm,),
            in_specs=[pl.BlockSpec((tm, d), lambda i, pos: (i, 0)),
                      pl.BlockSpec(memory_space=pl.ANY)],
            out_specs=pl.BlockSpec(memory_space=pl.ANY),
            scratch_shapes=[pltpu.SemaphoreType.DMA(())]),
        out_shape=jax.ShapeDtypeStruct((rows, d), F32),
        input_output_aliases={2: 0},
        compiler_params=_cparams("arbitrary"),
        name="moe_dispatch",
    )(pos_flat, h2, jnp.zeros((rows, d), F32))


def _combine_kernel(x_ref, mod_ref, route_ref, ya_ref, yb_ref, nf_ref, o_ref):
    route = route_ref[...]
    moe = route[:, 0:1] * ya_ref[...] + route[:, 1:2] * yb_ref[...]
    x2 = x_ref[...] + mod_ref[5:6, :] * moe
    o_ref[...] = _rms(x2, nf_ref[...])


def _moe_combine(x1, modsel, route, ya, yb, norm_f_w):
    bsz, s, d = x1.shape
    tm = TOKEN_BLOCK
    row = lambda w: pl.BlockSpec((None, tm, w), lambda b, j: (b, j, 0))
    return pl.pallas_call(
        _combine_kernel,
        grid=(bsz, s // tm),
        in_specs=[row(d), pl.BlockSpec((None, None, 8, d), lambda b, j: (b, 1, 0, 0)),
                  row(LANES), row(d), row(d), pl.BlockSpec((1, d), lambda b, j: (0, 0))],
        out_specs=row(d),
        out_shape=jax.ShapeDtypeStruct((bsz, s, d), F32),
        compiler_params=_cparams("parallel", "parallel"),
        name="moe_combine",
    )(x1, modsel, route, ya, yb, norm_f_w)


def _final_norm_kernel(x_ref, nf_ref, o_ref):
    o_ref[...] = _rms(x_ref[...], nf_ref[...])


def _final_norm(x, norm_f_w, first_row):
    bsz, s, d = x.shape
    tm = TOKEN_BLOCK
    first = first_row // tm
    return pl.pallas_call(
        _final_norm_kernel,
        grid=(bsz, (s - first_row) // tm),
        in_specs=[pl.BlockSpec((None, tm, d), lambda b, j: (b, j + first, 0)),
                  pl.BlockSpec((1, d), lambda b, j: (0, 0))],
        out_specs=pl.BlockSpec((None, tm, d), lambda b, j: (b, j, 0)),
        out_shape=jax.ShapeDtypeStruct((bsz, s - first_row, d), F32),
        compiler_params=_cparams("parallel", "parallel"),
        name="final_norm",
    )(x, norm_f_w)


def _moe_plan(route, tm):
    n = route.shape[0]
    e_flat = route[:, 2:4].astype(jnp.int32).reshape(-1)
    onehot = (e_flat[:, None] == jnp.arange(N_EXPERTS, dtype=jnp.int32)[None, :]).astype(jnp.int32)
    csum = jnp.cumsum(onehot, axis=0)
    counts = csum[-1]
    padded = ((counts + tm - 1) // tm) * tm
    ends = jnp.cumsum(padded)
    pos = jnp.sum(onehot * (csum - 1 + (ends - padded)[None, :]), axis=1)
    rows = 2 * n + N_EXPERTS * tm
    n_tiles = rows // tm
    starts = jnp.arange(n_tiles, dtype=jnp.int32) * tm
    tile_expert = jnp.minimum(jnp.sum((ends[None, :] <= starts[:, None]).astype(jnp.int32), axis=1),
                              N_EXPERTS - 1)
    n_valid = (ends[-1] // tm).astype(jnp.int32).reshape(1)
    token_end = (ends - padded + counts)[tile_expert]
    valid_rows = jnp.clip(token_end - starts, 0, tm).astype(jnp.int32)
    return rows, pos.astype(jnp.int32), tile_expert, n_valid, valid_rows


def _rope_tables(n_ctx, seq, head_dim, copies):
    half = head_dim // 4
    t = np.arange(seq)
    freqs = ROPE_THETA ** (-jnp.arange(half, dtype=F32) / half)
    rows = jnp.asarray(t // GRID_W, F32)[:, None] * freqs[None, :]
    cols = jnp.asarray(t % GRID_W, F32)[:, None] * freqs[None, :]
    cos = jnp.concatenate([jnp.cos(rows)] * 2 + [jnp.cos(cols)] * 2, axis=1)
    sin = jnp.concatenate([-jnp.sin(rows), jnp.sin(rows), -jnp.sin(cols), jnp.sin(cols)], axis=1)
    cos = jnp.concatenate([jnp.ones((n_ctx, head_dim), F32), cos], axis=0)
    sin = jnp.concatenate([jnp.zeros((n_ctx, head_dim), F32), sin], axis=0)
    return jnp.tile(cos, (1, copies)), jnp.tile(sin, (1, copies))


def _group_mean_matrix(width, group):
    g = np.kron(np.eye(width // group, dtype=np.float32), np.ones((group, group), np.float32)) / group
    return jnp.asarray(g, BF16)


def kernel(x, c, ctx, c_ctx, w_mod, b_mod, norm_mix, norm_ffn, w_in, w_out, gla_a2_f, gla_ab_f, gla_a2_b,
           gla_ab_b, gla_norm, gqa_q_norm, gqa_k_norm, diff_lam_q1, diff_lam_k1, diff_lam_q2, diff_lam_k2,
           diff_norm, ffn_gate, ffn_up, ffn_down, moe_router, moe_gate, moe_up, moe_down, norm_f):
    bsz, seq, d = x.shape
    n_ctx = ctx.shape[1]
    depth = w_mod.shape[0]

    c_rows = jnp.concatenate([c, c_ctx[None, :], jnp.zeros((8 - bsz - 1, d), F32)], axis=0)
    mod = _modulation(c_rows, w_mod, b_mod).reshape(depth, 8, 6, d)
    mod = jnp.pad(mod, ((0, 0), (0, 0), (0, 2), (0, 0)))
    modsel = jnp.stack([jnp.broadcast_to(mod[:, bsz][:, None], (depth, bsz, 8, d)), mod[:, :bsz]], axis=2)

    tables = (_rope_tables(n_ctx, seq, GQA_HD, GQA_HEADS) + _rope_tables(n_ctx, seq, DIFF_QK, 2 * DIFF_HEADS))
    gmat = _group_mean_matrix(GQA_HEADS * GQA_HD, GQA_HD)
    gmat_v = _group_mean_matrix(GLA_HEADS * GLA_DV, GLA_DV)

    w_in_p = jnp.concatenate([w_in[:, :, :_W_IN_LR_END],
                              jnp.zeros((depth, d, _W_GQA_Q - _W_IN_LR_END), F32),
                              w_in[:, :, _W_IN_LR_END:]], axis=2).astype(BF16)
    w_out_b = w_out.astype(BF16)
    a2 = jnp.zeros((depth, 2, LANES, LANES), F32)
    a2 = a2.at[:, 0, 0:GLA_GATE_RANK].set(gla_a2_f).at[:, 1, GLA_GATE_RANK:2 * GLA_GATE_RANK].set(gla_a2_b)
    a2 = a2.astype(BF16)
    ab = jnp.stack([gla_ab_f, gla_ab_b], axis=1)[:, :, None, :]

    x_all, x_ctx = x, ctx
    out = None
    for l in range(depth):
        ctx_out = l < depth - 1
        lambda_init = 0.8 - 0.6 * math.exp(-0.3 * l)
        gla_qk, gla_v, gla_r, gla_lr, gq, gkv, dq, dk, dv = _in_projection(
            x_all, x_ctx, modsel[l], norm_mix[l][None, :], w_in_p[l], gmat,
            jnp.tile(gqa_q_norm[l], GQA_HEADS)[None, :], jnp.tile(gqa_k_norm[l], GQA_KV_HEADS)[None, :], tables)
        o_f, o_b = _gla(gla_qk, gla_v, gla_lr, a2[l], ab[l], n_ctx // GLA_CHUNK)
        lam_rows = jnp.stack([diff_lam_q1[l], diff_lam_k1[l], diff_lam_q2[l], diff_lam_k2[l]], axis=0)
        gqa_o, diff_o = _attention(gq, gkv, dq, dk, dv, lam_rows, diff_norm[l][None, :], n_ctx, lambda_init,
                                   ctx_out)
        moe_layer = l % 2 == 1
        i = l // 2
        res = _out_projection(x_all, x_ctx, modsel[l], o_f, o_b, gla_r, gqa_o, diff_o, w_out_b[l], gmat_v,
                              jnp.tile(gla_norm[l], GLA_HEADS)[None, :], norm_ffn[l][None, :],
                              moe_router[i].T if moe_layer else None, ctx_out)
        if not moe_layer:
            x1, h2 = res
            if not ctx_out:
                raise NotImplementedError("dense feed-forward on the last layer")
            x_all = _dense_ffn(h2, x1, modsel[l], ffn_gate[i].astype(BF16), ffn_up[i].astype(BF16),
                               ffn_down[i].astype(BF16), n_ctx)
            x_ctx = None
            if l == depth - 1:
                out = _final_norm(x_all, norm_f[None, :], n_ctx)
        else:
            x1, h2, route = res
            if ctx_out or l != depth - 1:
                raise NotImplementedError("expert feed-forward before the last layer")
            n = bsz * seq
            rows, pos, tile_expert, n_valid, valid_rows = _moe_plan(route.reshape(n, LANES), MOE_TILE)
            xs = _moe_dispatch(pos, h2.reshape(n, d), rows)
            ys = _moe_experts(tile_expert, n_valid, valid_rows, xs, moe_gate[i], moe_up[i], moe_down[i])
            ya = ys.at[pos[0::2]].get(mode="promise_in_bounds").reshape(bsz, seq, d)
            yb = ys.at[pos[1::2]].get(mode="promise_in_bounds").reshape(bsz, seq, d)
            out = _moe_combine(x1, modsel[l], route, ya, yb, norm_f[None, :])
    return out
```

```python
import functools
import math

import jax
import jax.numpy as jnp
import numpy as np
from jax import lax
from jax.experimental import pallas as pl
from jax.experimental.pallas import tpu as pltpu

F32 = jnp.float32
BF16 = jnp.bfloat16

GRID_W = 64
GLA_HEADS, GLA_DK, GLA_DV = 4, 32, 64
GLA_GATE_RANK = 16
GLA_GATE_NORM = 16.0
GLA_CHUNK = 64
GQA_HEADS, GQA_KV_HEADS, GQA_HD = 8, 2, 64
GQA_GROUP = GQA_HEADS // GQA_KV_HEADS
DIFF_HEADS, DIFF_QK, DIFF_V = 4, 32, 64
ROPE_THETA = 10000.0
NORM_EPS = 1e-6
N_EXPERTS = 8

LANES = 128
TOKEN_BLOCK = 256
VMEM_LIMIT = 56 * 1024 * 1024

_W_GLA_QK = 0
_W_GLA_V = 256
_W_GLA_R = 512
_W_GLA_LR = 768
_W_GQA_Q = 896
_W_GQA_K = 1408
_W_GQA_V = 1536
_W_DIFF_Q = 1664
_W_DIFF_K = 1920
_W_DIFF_V = 2176
_W_IN_PAD = 2432
_W_IN_LR_END = 800


def _cparams(*sem):
    return pltpu.CompilerParams(dimension_semantics=sem, vmem_limit_bytes=VMEM_LIMIT)


def _rms(x, w):
    ms = jnp.mean(x * x, axis=-1, keepdims=True)
    return x * lax.rsqrt(ms + NORM_EPS) * w


def _silu(x):
    return x * (1.0 / (1.0 + jnp.exp(-x)))


def _nt_dot(a, b):
    return lax.dot_general(a, b, (((1,), (1,)), ((), ())), preferred_element_type=F32)


def _tn_dot(a, b):
    return lax.dot_general(a, b, (((0,), (0,)), ((), ())), preferred_element_type=F32)


def _mod_kernel(c_ref, w_ref, b_ref, o_ref):
    s = _silu(c_ref[...])
    o_ref[...] = jnp.dot(s.astype(BF16), w_ref[...].astype(BF16),
                         preferred_element_type=F32) + b_ref[...]


def _modulation(c_rows, w_mod, b_mod):
    depth, d, n = w_mod.shape
    tn = 1536
    return pl.pallas_call(
        _mod_kernel,
        grid=(depth, n // tn),
        in_specs=[pl.BlockSpec((8, d), lambda l, j: (0, 0)),
                  pl.BlockSpec((None, d, tn), lambda l, j: (l, 0, j)),
                  pl.BlockSpec((None, 1, tn), lambda l, j: (l, 0, j))],
        out_specs=pl.BlockSpec((None, 8, tn), lambda l, j: (l, 0, j)),
        out_shape=jax.ShapeDtypeStruct((depth, 8, n), F32),
        compiler_params=_cparams("parallel", "parallel"),
        name="modulation",
    )(c_rows, w_mod, b_mod.reshape(depth, 1, n))


def _rope(x, cos, sin_signed, half):
    width = x.shape[1]
    lane = lax.broadcasted_iota(jnp.int32, x.shape, 1)
    first = (lane % (2 * half)) < half
    rot = jnp.where(first, pltpu.roll(x, width - half, 1), pltpu.roll(x, half, 1))
    return x * cos + rot * sin_signed


def _token_rows(refs, split_ctx):
    if not split_ctx:
        return refs[0][...], refs[1:]
    return jnp.where(pl.program_id(1) == 0, refs[0][...], refs[1][...]), refs[2:]


def _token_specs(x, ctx, tm, first):
    d = x.shape[-1]
    if ctx is None:
        return [pl.BlockSpec((None, tm, d), lambda b, j: (b, j + first, 0))], [x]
    assert ctx.shape[1] == tm and first == 0
    return ([pl.BlockSpec((None, tm, d), lambda b, j: (b, 0, 0)),
             pl.BlockSpec((None, tm, d), lambda b, j: (b, jnp.maximum(j - 1, 0), 0))], [ctx, x])


def _values_and_ones(v, head_dim):
    ones = jnp.ones((v.shape[0], head_dim), v.dtype)
    parts = []
    for h in range(v.shape[1] // head_dim):
        parts += [v[:, h * head_dim:(h + 1) * head_dim], ones]
    return jnp.concatenate(parts, axis=1)


def _in_kernel(*refs, split_ctx):
    x, refs = _token_rows(refs, split_ctx)
    (mod_ref, nw_ref, w_ref, g_ref, qg_ref, kg_ref, cg_ref, sg_ref, cd_ref, sd_ref,
     glaqk_ref, glav_ref, glar_ref, glalr_ref, gq_ref, gk_ref, gv_ref, dq_ref, dk_ref, dv_ref) = refs
    mod = mod_ref[...]
    h = _rms(x, nw_ref[...]) * (1.0 + mod[1:2]) + mod[0:1]
    hb = h.astype(BF16)

    def proj(a, b):
        return jnp.dot(hb, w_ref[:, a:b], preferred_element_type=F32)

    glaqk_ref[...] = proj(_W_GLA_QK, _W_GLA_V)
    glav_ref[...] = proj(_W_GLA_V, _W_GLA_R).astype(BF16)
    glar_ref[...] = proj(_W_GLA_R, _W_GLA_LR)
    glalr_ref[...] = proj(_W_GLA_LR, _W_GQA_Q)

    cos_k = cg_ref[...]
    sin_k = sg_ref[...]
    reps = GQA_HEADS * GQA_HD // LANES
    q = proj(_W_GQA_Q, _W_GQA_K)
    ms = jnp.dot((q * q).astype(BF16), g_ref[...], preferred_element_type=F32)
    qn = q * lax.rsqrt(ms + NORM_EPS) * qg_ref[...]
    gq_ref[...] = (_rope(qn, jnp.concatenate([cos_k] * reps, axis=1), jnp.concatenate([sin_k] * reps, axis=1),
                         GQA_HD // 4) * GQA_HD ** -0.5).astype(BF16)

    kw = GQA_KV_HEADS * GQA_HD
    k = proj(_W_GQA_K, _W_GQA_V)
    ms = jnp.dot((k * k).astype(BF16), g_ref[0:kw, 0:kw], preferred_element_type=F32)
    kn = k * lax.rsqrt(ms + NORM_EPS) * kg_ref[...]
    gk_ref[...] = _rope(kn, cos_k, sin_k, GQA_HD // 4).astype(BF16)
    gv_ref[...] = _values_and_ones(proj(_W_GQA_V, _W_DIFF_Q), GQA_HD).astype(BF16)

    reps = 2 * DIFF_HEADS * DIFF_QK // LANES
    cos_d = jnp.concatenate([cd_ref[...]] * reps, axis=1)
    sin_d = jnp.concatenate([sd_ref[...]] * reps, axis=1)
    dq = proj(_W_DIFF_Q, _W_DIFF_K)
    dq_ref[...] = (_rope(dq, cos_d, sin_d, DIFF_QK // 4) * DIFF_QK ** -0.5).astype(BF16)
    dk = proj(_W_DIFF_K, _W_DIFF_V)
    dk_ref[...] = _rope(dk, cos_d, sin_d, DIFF_QK // 4).astype(BF16)
    dv_ref[...] = _values_and_ones(proj(_W_DIFF_V, _W_IN_PAD), DIFF_V).astype(BF16)


def _in_projection(x, ctx, modsel, norm_w, w_in_p, gmat, q_gain, k_gain, tables):
    bsz, s, d = x.shape
    tm = TOKEN_BLOCK
    if ctx is not None:
        s += ctx.shape[1]
    x_specs, x_args = _token_specs(x, ctx, tm, 0)
    cos_g, sin_g, cos_d, sin_d = tables
    row = lambda w: pl.BlockSpec((None, tm, w), lambda b, j: (b, j, 0))
    const = lambda shape: pl.BlockSpec(shape, lambda b, j: (0,) * len(shape))
    tab = lambda w: pl.BlockSpec((tm, w), lambda b, j: (j, 0))
    widths = [(256, F32), (256, BF16), (256, F32), (128, F32), (512, BF16), (128, BF16), (256, BF16),
              (256, BF16), (256, BF16), (512, BF16)]
    return pl.pallas_call(
        functools.partial(_in_kernel, split_ctx=ctx is not None),
        grid=(bsz, s // tm),
        in_specs=x_specs + [
                  pl.BlockSpec((None, None, 8, d), lambda b, j: (b, jnp.minimum(j, 1), 0, 0)),
                  const((1, d)), const((d, _W_IN_PAD)), const((512, 512)),
                  const((1, 512)), const((1, 128)),
                  tab(LANES), tab(LANES), tab(LANES), tab(LANES)],
        out_specs=[row(w) for w, _ in widths],
        out_shape=[jax.ShapeDtypeStruct((bsz, s, w), dt) for w, dt in widths],
        compiler_params=_cparams("parallel", "parallel"),
        name="in_projection",
    )(*x_args, modsel, norm_w, w_in_p, gmat, q_gain, k_gain, cos_g, sin_g, cos_d, sin_d)


def _split3(g):
    g1 = g.astype(BF16)
    r1 = g - g1.astype(F32)
    g2 = r1.astype(BF16)
    g3 = (r1 - g2.astype(F32)).astype(BF16)
    return jnp.concatenate([g1, g2, g3], axis=1)


def _gla_kernel(qkf_ref, vf_ref, lrf_ref, qkb_ref, vb_ref, lrb_ref, a2_ref, ab_ref,
                of_ref, ob_ref, st_ref):
    @pl.when(pl.program_id(0) == 0)
    def _():
        st_ref[...] = jnp.zeros_like(st_ref)

    n_b = qkf_ref.shape[0]
    c = GLA_CHUNK
    hk = GLA_HEADS * GLA_DK
    hv = GLA_HEADS * GLA_DV
    r_i = lax.broadcasted_iota(jnp.int32, (c, c), 0)
    c_i = lax.broadcasted_iota(jnp.int32, (c, c), 1)
    head_of_k = lax.broadcasted_iota(jnp.int32, (c, hk), 1) // GLA_DK
    head_of_v = lax.broadcasted_iota(jnp.int32, (c, hv), 1) // GLA_DV
    dirs = ((qkf_ref, vf_ref, lrf_ref, of_ref, c_i <= r_i, c - 1),
            (qkb_ref, vb_ref, lrb_ref, ob_ref, c_i >= r_i, 0))
    for d, (qk_ref, v_ref, lr_ref, o_ref, tri, last) in enumerate(dirs):
        tri_b = jnp.where(tri, 1.0, 0.0).astype(BF16)
        tri_heads = jnp.concatenate([tri] * GLA_HEADS, axis=0)
        lr = jnp.concatenate([lr_ref[b] for b in range(n_b)], axis=0)
        pre = jnp.dot(lr.astype(BF16), a2_ref[d], preferred_element_type=F32) + ab_ref[d]
        log_a = (jnp.minimum(pre, 0.0) - jnp.log(1.0 + jnp.exp(-jnp.abs(pre)))) * (1.0 / GLA_GATE_NORM)
        parts = _split3(log_a)
        cs = jnp.dot(tri_b, jnp.concatenate([parts[b * c:(b + 1) * c] for b in range(n_b)], axis=1),
                     preferred_element_type=F32)
        for b in range(n_b):
            base = b * 3 * hk
            bc = cs[:, base:base + hk] + cs[:, base + hk:base + 2 * hk] + cs[:, base + 2 * hk:base + 3 * hk]
            tot = bc[last:last + 1, :]
            qk = qk_ref[b]
            q = qk[:, 0:hk] * GLA_DK ** -0.5
            k = qk[:, hk:2 * hk]
            qd = q * jnp.exp(bc)
            kd = (k * jnp.exp(-bc)).astype(BF16)
            kl = (k * jnp.exp(tot - bc)).astype(BF16)
            v = v_ref[b]
            st = st_ref[d, b]
            q_heads = jnp.concatenate([jnp.where(head_of_k == h, qd, 0.0) for h in range(GLA_HEADS)],
                                      axis=0).astype(BF16)
            a = jnp.where(tri_heads, _nt_dot(q_heads, kd), 0.0).astype(BF16)
            o_all = jnp.dot(a, v, preferred_element_type=F32) + _nt_dot(q_heads, st.astype(BF16))
            o = jnp.where(head_of_v == 0, o_all[0:c], 0.0)
            for h in range(1, GLA_HEADS):
                o = o + jnp.where(head_of_v == h, o_all[h * c:(h + 1) * c], 0.0)
            o_ref[b] = o
            st_ref[d, b] = st * jnp.exp(tot) + _tn_dot(v, kl)


def _gla(gla_qk, gla_v, gla_lr, a2, ab, n_ctx_chunks):
    bsz, s, _ = gla_qk.shape
    c = GLA_CHUNK
    n = s // c
    fwd = lambda i: (0, i, 0)
    bwd = lambda i: (0, jnp.where(i < n_ctx_chunks, n_ctx_chunks - 1 - i, n + n_ctx_chunks - 1 - i), 0)
    blk = lambda w, m: pl.BlockSpec((bsz, c, w), m)
    hv = GLA_HEADS * GLA_DV
    return pl.pallas_call(
        _gla_kernel,
        grid=(n,),
        in_specs=[blk(256, fwd), blk(hv, fwd), blk(LANES, fwd),
                  blk(256, bwd), blk(hv, bwd), blk(LANES, bwd),
                  pl.BlockSpec((2, LANES, LANES), lambda i: (0, 0, 0)),
                  pl.BlockSpec((2, 1, LANES), lambda i: (0, 0, 0))],
        out_specs=[blk(hv, fwd), blk(hv, bwd)],
        out_shape=[jax.ShapeDtypeStruct((bsz, s, hv), F32)] * 2,
        scratch_shapes=[pltpu.VMEM((2, bsz, hv, GLA_HEADS * GLA_DK), F32)],
        compiler_params=_cparams("arbitrary"),
        name="gla_scan",
    )(gla_qk, gla_v, gla_lr, gla_qk, gla_v, gla_lr, a2, ab)


def _softmax_attend(q, k, v_ones):
    s = _nt_dot(q, k)
    p = jnp.exp((s - jnp.max(s, axis=-1, keepdims=True)).astype(BF16))
    o = jnp.dot(p, v_ones, preferred_element_type=F32)
    hd = v_ones.shape[1] // 2
    return o[:, 0:hd] * (1.0 / o[:, hd:hd + 1])


def _attn_kernel(gq_ref, gk_ref, gv_ref, dq_ref, dk_ref, dv_ref, lam_ref, dnw_ref, go_ref, do_ref,
                 *, n_ctx, out_scale, first_block):
    lam = (jnp.exp(jnp.sum(lam_ref[0:1, :] * lam_ref[1:2, :], axis=-1, keepdims=True))
           - jnp.exp(jnp.sum(lam_ref[2:3, :] * lam_ref[3:4, :], axis=-1, keepdims=True))
           + (1.0 - out_scale))
    tq = gq_ref.shape[0]

    def attend(nk):
        for kh in range(GQA_KV_HEADS):
            heads = [slice((kh * GQA_GROUP + g) * GQA_HD, (kh * GQA_GROUP + g + 1) * GQA_HD)
                     for g in range(GQA_GROUP)]
            q = jnp.concatenate([gq_ref[:, hs] for hs in heads], axis=0)
            o = _softmax_attend(q, gk_ref[0:nk, kh * GQA_HD:(kh + 1) * GQA_HD],
                                gv_ref[0:nk, 2 * kh * GQA_HD:2 * (kh + 1) * GQA_HD])
            for g, hs in enumerate(heads):
                go_ref[:, hs] = o[g * tq:(g + 1) * tq].astype(BF16)
        for h in range(DIFF_HEADS):
            base = h * 2 * DIFF_QK
            v = dv_ref[0:nk, 2 * h * DIFF_V:2 * (h + 1) * DIFF_V]
            o = (_softmax_attend(dq_ref[:, base:base + DIFF_QK], dk_ref[0:nk, base:base + DIFF_QK], v)
                 - lam * _softmax_attend(dq_ref[:, base + DIFF_QK:base + 2 * DIFF_QK],
                                         dk_ref[0:nk, base + DIFF_QK:base + 2 * DIFF_QK], v))
            do_ref[:, h * DIFF_V:(h + 1) * DIFF_V] = (_rms(o, dnw_ref[...]) * out_scale).astype(BF16)

    if first_block == 0:
        @pl.when(pl.program_id(1) == 0)
        def _():
            attend(n_ctx)

        @pl.when(pl.program_id(1) > 0)
        def _():
            attend(gk_ref.shape[0])
    else:
        attend(gk_ref.shape[0])


def _attention(gq, gk, gv, dq, dk, dv, lam_rows, diff_norm_w, n_ctx, lambda_init, ctx_out):
    bsz, s, _ = gq.shape
    tq = TOKEN_BLOCK
    assert n_ctx == tq
    first = 0 if ctx_out else 1
    s_out = s - first * tq
    kw = GQA_KV_HEADS * GQA_HD
    qrow = lambda w: pl.BlockSpec((None, tq, w), lambda b, j: (b, j + first, 0))
    orow = lambda w: pl.BlockSpec((None, tq, w), lambda b, j: (b, j, 0))
    full = lambda w: pl.BlockSpec((None, s, w), lambda b, j: (b, 0, 0))
    return pl.pallas_call(
        functools.partial(_attn_kernel, n_ctx=n_ctx, out_scale=1.0 - lambda_init, first_block=first),
        grid=(bsz, s_out // tq),
        in_specs=[qrow(512), full(kw), full(2 * kw), qrow(256), full(256), full(512),
                  pl.BlockSpec((4, DIFF_QK), lambda b, j: (0, 0)),
                  pl.BlockSpec((1, DIFF_V), lambda b, j: (0, 0))],
        out_specs=[orow(512), orow(256)],
        out_shape=[jax.ShapeDtypeStruct((bsz, s_out, 512), BF16),
                   jax.ShapeDtypeStruct((bsz, s_out, 256), BF16)],
        compiler_params=_cparams("parallel", "parallel"),
        name="attention",
    )(gq, gk, gv, dq, dk, dv, lam_rows, diff_norm_w)


def _route(h, wr_ref):
    logits = [jnp.sum(h * wr_ref[e:e + 1, :], axis=-1, keepdims=True) for e in range(N_EXPERTS)]

    def top(ls):
        m = functools.reduce(jnp.maximum, ls)
        idx = jnp.full_like(m, float(N_EXPERTS))
        for e in reversed(range(N_EXPERTS)):
            idx = jnp.where(ls[e] == m, float(e), idx)
        return m, idx

    m1, i1 = top(logits)
    m2, i2 = top([jnp.where(i1 == float(e), -jnp.inf, logits[e]) for e in range(N_EXPERTS)])
    e2 = jnp.exp(m2 - m1)
    w1 = 1.0 / (1.0 + e2)
    w2 = e2 * w1
    lane = lax.broadcasted_iota(jnp.int32, (h.shape[0], LANES), 1)
    return jnp.where(lane == 0, w1, jnp.where(lane == 1, w2, jnp.where(lane == 2, i1,
                     jnp.where(lane == 3, i2, 0.0))))


def _out_kernel(*refs, with_router, split_ctx):
    x, refs = _token_rows(refs, split_ctx)
    if with_router:
        (mod_ref, of_ref, ob_ref, r_ref, go_ref, do_ref, w_ref, g_ref, gnw_ref, nw_ref, wr_ref,
         x1_ref, h2_ref, route_ref) = refs
    else:
        (mod_ref, of_ref, ob_ref, r_ref, go_ref, do_ref, w_ref, g_ref, gnw_ref, nw_ref,
         x1_ref, h2_ref) = refs
    mod = mod_ref[...]
    hv = GLA_HEADS * GLA_DV
    gw = GQA_HEADS * GQA_HD
    o = of_ref[...] + ob_ref[...]
    ms = jnp.dot((o * o).astype(BF16), g_ref[...], preferred_element_type=F32)
    gla = o * lax.rsqrt(ms + NORM_EPS) * gnw_ref[...] * _silu(r_ref[...])
    y = (jnp.dot(gla.astype(BF16), w_ref[0:hv, :], preferred_element_type=F32)
         + jnp.dot(go_ref[...], w_ref[hv:hv + gw, :], preferred_element_type=F32)
         + jnp.dot(do_ref[...], w_ref[hv + gw:, :], preferred_element_type=F32))
    x1 = x + mod[2:3] * y
    x1_ref[...] = x1
    h2 = _rms(x1, nw_ref[...]) * (1.0 + mod[4:5]) + mod[3:4]
    h2_ref[...] = h2.astype(h2_ref.dtype)
    if with_router:
        route_ref[...] = _route(h2, wr_ref)


def _out_projection(x, ctx, modsel, o_f, o_b, gla_r, gqa_o, diff_o, w_out, gmat, gla_norm_w, norm_ffn_w,
                    router_t, ctx_out):
    bsz, s, d = x.shape
    tm = TOKEN_BLOCK
    if ctx is not None:
        s += ctx.shape[1]
    first = 0 if ctx_out else 1
    s_out = s - first * tm
    x_specs, x_args = _token_specs(x, ctx, tm, first)
    arow = lambda w: pl.BlockSpec((None, tm, w), lambda b, j: (b, j + first, 0))
    orow = lambda w: pl.BlockSpec((None, tm, w), lambda b, j: (b, j, 0))
    const = lambda shape: pl.BlockSpec(shape, lambda b, j: (0,) * len(shape))
    with_router = router_t is not None
    in_specs = x_specs + [
                pl.BlockSpec((None, None, 8, d), lambda b, j: (b, jnp.minimum(j + first, 1), 0, 0)),
                arow(256), arow(256), arow(256), orow(512), orow(256),
                const((d, d)), const((256, 256)), const((1, 256)), const((1, d))]
    args = x_args + [modsel, o_f, o_b, gla_r, gqa_o, diff_o, w_out, gmat, gla_norm_w, norm_ffn_w]
    out_specs = [orow(d), orow(d)]
    out_shape = [jax.ShapeDtypeStruct((bsz, s_out, d), F32),
                 jax.ShapeDtypeStruct((bsz, s_out, d), F32 if with_router else BF16)]
    if with_router:
        in_specs.append(const((N_EXPERTS, d)))
        args.append(router_t)
        out_specs.append(orow(LANES))
        out_shape.append(jax.ShapeDtypeStruct((bsz, s_out, LANES), F32))
    return pl.pallas_call(
        functools.partial(_out_kernel, with_router=with_router, split_ctx=ctx is not None),
        grid=(bsz, s_out // tm),
        in_specs=in_specs, out_specs=out_specs, out_shape=out_shape,
        compiler_params=_cparams("parallel", "parallel"),
        name="out_projection",
    )(*args)


def _ffn_kernel(h_ref, x_ref, mod_ref, wg_ref, wu_ref, wd_ref, o_ref, acc_ref, *, n_ctx):
    f = pl.program_id(2)

    @pl.when(f == 0)
    def _():
        acc_ref[...] = jnp.zeros_like(acc_ref)

    h = h_ref[...]
    g = jnp.dot(h, wg_ref[...], preferred_element_type=F32)
    u = jnp.dot(h, wu_ref[...], preferred_element_type=F32)
    acc_ref[...] += jnp.dot((_silu(g) * u).astype(BF16), wd_ref[...], preferred_element_type=F32)

    @pl.when(f == pl.num_programs(2) - 1)
    def _():
        tm = h_ref.shape[0]
        row = pl.program_id(1) * tm + lax.broadcasted_iota(jnp.int32, (tm, 1), 0)
        gate = jnp.where(row < n_ctx, mod_ref[0, 5:6, :], mod_ref[1, 5:6, :])
        o_ref[...] = x_ref[...] + gate * acc_ref[...]


def _dense_ffn(h2, x1, modsel, wg, wu, wd, n_ctx):
    bsz, s, d = x1.shape
    ff = wg.shape[1]
    tm = s // 8
    tf = ff // 2
    row = lambda: pl.BlockSpec((None, tm, d), lambda b, t, f: (b, t, 0))
    return pl.pallas_call(
        functools.partial(_ffn_kernel, n_ctx=n_ctx),
        grid=(bsz, s // tm, ff // tf),
        in_specs=[row(), row(),
                  pl.BlockSpec((None, 2, 8, d), lambda b, t, f: (b, 0, 0, 0)),
                  pl.BlockSpec((d, tf), lambda b, t, f: (0, f)),
                  pl.BlockSpec((d, tf), lambda b, t, f: (0, f)),
                  pl.BlockSpec((tf, d), lambda b, t, f: (f, 0))],
        out_specs=row(),
        out_shape=jax.ShapeDtypeStruct((bsz, s, d), F32),
        scratch_shapes=[pltpu.VMEM((tm, d), F32)],
        compiler_params=_cparams("parallel", "parallel", "arbitrary"),
        name="dense_ffn",
    )(h2, x1, modsel, wg, wu, wd)


MOE_TILE = 1024
MOE_SUB = 256
MOE_FF_BLOCK = 512


def _moe_kernel(te_ref, nv_ref, vr_ref, xs_ref, wg_ref, wu_ref, wd_ref, o_ref, xb_ref):
    f = pl.program_id(1)
    last_f = pl.num_programs(1) - 1
    valid = vr_ref[pl.program_id(0)]

    @pl.when(valid > 0)
    def _():
        wg = wg_ref[...].astype(BF16)
        wu = wu_ref[...].astype(BF16)
        wd = wd_ref[...].astype(BF16)
        for sb in range(MOE_TILE // MOE_SUB):
            rows = pl.ds(sb * MOE_SUB, MOE_SUB)

            @pl.when(sb * MOE_SUB < valid)
            def _():
                @pl.when(f == 0)
                def _():
                    o_ref[rows, :] = jnp.zeros((MOE_SUB, o_ref.shape[1]), F32)
                    xb_ref[rows, :] = xs_ref[rows, :].astype(BF16)

                h = xb_ref[rows, :]
                g = jnp.dot(h, wg, preferred_element_type=F32)
                u = jnp.dot(h, wu, preferred_element_type=F32)
                o_ref[rows, :] += jnp.dot((_silu(g) * u).astype(BF16), wd, preferred_element_type=F32)

            @pl.when(jnp.logical_and(sb * MOE_SUB >= valid, f == last_f))
            def _():
                o_ref[rows, :] = jnp.zeros((MOE_SUB, o_ref.shape[1]), F32)

    @pl.when(jnp.logical_and(valid == 0, f == last_f))
    def _():
        o_ref[...] = jnp.zeros_like(o_ref)


def _moe_experts(tile_expert, n_valid, valid_rows, xs, wg, wu, wd):
    rows, d = xs.shape
    ff = wg.shape[2]
    tm = MOE_TILE
    tf = MOE_FF_BLOCK
    nf = ff // tf

    def tile(i, nv):
        return jnp.minimum(i, nv[0] - 1)

    def fidx(i, f, nv):
        return jnp.where(i < nv[0], f, nf - 1)

    return pl.pallas_call(
        _moe_kernel,
        grid_spec=pltpu.PrefetchScalarGridSpec(
            num_scalar_prefetch=3,
            grid=(rows // tm, nf),
            in_specs=[pl.BlockSpec((tm, d), lambda i, f, te, nv, vr: (tile(i, nv), 0)),
                      pl.BlockSpec((None, d, tf), lambda i, f, te, nv, vr: (te[tile(i, nv)], 0, fidx(i, f, nv))),
                      pl.BlockSpec((None, d, tf), lambda i, f, te, nv, vr: (te[tile(i, nv)], 0, fidx(i, f, nv))),
                      pl.BlockSpec((None, tf, d), lambda i, f, te, nv, vr: (te[tile(i, nv)], fidx(i, f, nv), 0))],
            out_specs=pl.BlockSpec((tm, d), lambda i, f, te, nv, vr: (i, 0)),
            scratch_shapes=[pltpu.VMEM((tm, d), BF16)]),
        out_shape=jax.ShapeDtypeStruct((rows, d), F32),
        compiler_params=_cparams("arbitrary", "arbitrary"),
        name="moe_experts",
    )(tile_expert, n_valid, valid_rows, xs, wg, wu, wd)


def _dispatch_kernel(pos_ref, h_ref, init_ref, xs_ref, sem):
    del init_ref
    tm = h_ref.shape[0]
    base = pl.program_id(0) * (2 * tm)
    for r in range(tm):
        for k in range(2):
            pltpu.make_async_copy(h_ref.at[pl.ds(r, 1), :],
                                  xs_ref.at[pl.ds(pos_ref[base + 2 * r + k], 1), :], sem).start()
    for k in range(2):
        pltpu.make_async_copy(h_ref, xs_ref.at[pl.ds(0, tm), :], sem).wait()


def _moe_dispatch(pos_flat, h2, rows):
    n, d = h2.shape
    tm = TOKEN_BLOCK
    return pl.pallas_call(
        _dispatch_kernel,
        grid_spec=pltpu.PrefetchScalarGridSpec(
            num_scalar_prefetch=1,
            grid=(n // tm,),
            in_specs=[pl.BlockSpec((tm, d), lambda i, pos: (i, 0)),
                      pl.BlockSpec(memory_space=pl.ANY)],
            out_specs=pl.BlockSpec(memory_space=pl.ANY),
            scratch_shapes=[pltpu.SemaphoreType.DMA(())]),
        out_shape=jax.ShapeDtypeStruct((rows, d), F32),
        input_output_aliases={2: 0},
        compiler_params=_cparams("arbitrary"),
        name="moe_dispatch",
    )(pos_flat, h2, jnp.zeros((rows, d), F32))


def _combine_kernel(x_ref, mod_ref, route_ref, ya_ref, yb_ref, nf_ref, o_ref):
    route = route_ref[...]
    moe = route[:, 0:1] * ya_ref[...] + route[:, 1:2] * yb_ref[...]
    x2 = x_ref[...] + mod_ref[5:6, :] * moe
    o_ref[...] = _rms(x2, nf_ref[...])


def _moe_combine(x1, modsel, route, ya, yb, norm_f_w):
    bsz, s, d = x1.shape
    tm = TOKEN_BLOCK
    row = lambda w: pl.BlockSpec((None, tm, w), lambda b, j: (b, j, 0))
    return pl.pallas_call(
        _combine_kernel,
        grid=(bsz, s // tm),
        in_specs=[row(d), pl.BlockSpec((None, None, 8, d), lambda b, j: (b, 1, 0, 0)),
                  row(LANES), row(d), row(d), pl.BlockSpec((1, d), lambda b, j: (0, 0))],
        out_specs=row(d),
        out_shape=jax.ShapeDtypeStruct((bsz, s, d), F32),
        compiler_params=_cparams("parallel", "parallel"),
        name="moe_combine",
    )(x1, modsel, route, ya, yb, norm_f_w)


def _final_norm_kernel(x_ref, nf_ref, o_ref):
    o_ref[...] = _rms(x_ref[...], nf_ref[...])


def _final_norm(x, norm_f_w, first_row):
    bsz, s, d = x.shape
    tm = TOKEN_BLOCK
    first = first_row // tm
    return pl.pallas_call(
        _final_norm_kernel,
        grid=(bsz, (s - first_row) // tm),
        in_specs=[pl.BlockSpec((None, tm, d), lambda b, j: (b, j + first, 0)),
                  pl.BlockSpec((1, d), lambda b, j: (0, 0))],
        out_specs=pl.BlockSpec((None, tm, d), lambda b, j: (b, j, 0)),
        out_shape=jax.ShapeDtypeStruct((bsz, s - first_row, d), F32),
        compiler_params=_cparams("parallel", "parallel"),
        name="final_norm",
    )(x, norm_f_w)


def _moe_plan(route, tm):
    n = route.shape[0]
    e_flat = route[:, 2:4].astype(jnp.int32).reshape(-1)
    onehot = (e_flat[:, None] == jnp.arange(N_EXPERTS, dtype=jnp.int32)[None, :]).astype(jnp.int32)
    csum = jnp.cumsum(onehot, axis=0)
    counts = csum[-1]
    padded = ((counts + tm - 1) // tm) * tm
    ends = jnp.cumsum(padded)
    pos = jnp.sum(onehot * (csum - 1 + (ends - padded)[None, :]), axis=1)
    rows = 2 * n + N_EXPERTS * tm
    n_tiles = rows // tm
    starts = jnp.arange(n_tiles, dtype=jnp.int32) * tm
    tile_expert = jnp.minimum(jnp.sum((ends[None, :] <= starts[:, None]).astype(jnp.int32), axis=1),
                              N_EXPERTS - 1)
    n_valid = (ends[-1] // tm).astype(jnp.int32).reshape(1)
    token_end = (ends - padded + counts)[tile_expert]
    valid_rows = jnp.clip(token_end - starts, 0, tm).astype(jnp.int32)
    return rows, pos.astype(jnp.int32), tile_expert, n_valid, valid_rows


def _rope_tables(n_ctx, seq, head_dim, copies):
    half = head_dim // 4
    t = np.arange(seq)
    freqs = ROPE_THETA ** (-jnp.arange(half, dtype=F32) / half)
    rows = jnp.asarray(t // GRID_W, F32)[:, None] * freqs[None, :]
    cols = jnp.asarray(t % GRID_W, F32)[:, None] * freqs[None, :]
    cos = jnp.concatenate([jnp.cos(rows)] * 2 + [jnp.cos(cols)] * 2, axis=1)
    sin = jnp.concatenate([-jnp.sin(rows), jnp.sin(rows), -jnp.sin(cols), jnp.sin(cols)], axis=1)
    cos = jnp.concatenate([jnp.ones((n_ctx, head_dim), F32), cos], axis=0)
    sin = jnp.concatenate([jnp.zeros((n_ctx, head_dim), F32), sin], axis=0)
    return jnp.tile(cos, (1, copies)), jnp.tile(sin, (1, copies))


def _group_mean_matrix(width, group):
    g = np.kron(np.eye(width // group, dtype=np.float32), np.ones((group, group), np.float32)) / group
    return jnp.asarray(g, BF16)


def kernel(x, c, ctx, c_ctx, w_mod, b_mod, norm_mix, norm_ffn, w_in, w_out, gla_a2_f, gla_ab_f, gla_a2_b,
           gla_ab_b, gla_norm, gqa_q_norm, gqa_k_norm, diff_lam_q1, diff_lam_k1, diff_lam_q2, diff_lam_k2,
           diff_norm, ffn_gate, ffn_up, ffn_down, moe_router, moe_gate, moe_up, moe_down, norm_f):
    bsz, seq, d = x.shape
    n_ctx = ctx.shape[1]
    depth = w_mod.shape[0]

    c_rows = jnp.concatenate([c, c_ctx[None, :], jnp.zeros((8 - bsz - 1, d), F32)], axis=0)
    mod = _modulation(c_rows, w_mod, b_mod).reshape(depth, 8, 6, d)
    mod = jnp.pad(mod, ((0, 0), (0, 0), (0, 2), (0, 0)))
    modsel = jnp.stack([jnp.broadcast_to(mod[:, bsz][:, None], (depth, bsz, 8, d)), mod[:, :bsz]], axis=2)

    tables = (_rope_tables(n_ctx, seq, GQA_HD, LANES // GQA_HD) + _rope_tables(n_ctx, seq, DIFF_QK, LANES // DIFF_QK))
    gmat = _group_mean_matrix(GQA_HEADS * GQA_HD, GQA_HD)
    gmat_v = _group_mean_matrix(GLA_HEADS * GLA_DV, GLA_DV)

    w_in_p = jnp.concatenate([w_in[:, :, :_W_IN_LR_END],
                              jnp.zeros((depth, d, _W_GQA_Q - _W_IN_LR_END), F32),
                              w_in[:, :, _W_IN_LR_END:]], axis=2).astype(BF16)
    w_out_b = w_out.astype(BF16)
    a2 = jnp.zeros((depth, 2, LANES, LANES), F32)
    a2 = a2.at[:, 0, 0:GLA_GATE_RANK].set(gla_a2_f).at[:, 1, GLA_GATE_RANK:2 * GLA_GATE_RANK].set(gla_a2_b)
    a2 = a2.astype(BF16)
    ab = jnp.stack([gla_ab_f, gla_ab_b], axis=1)[:, :, None, :]

    x_all, x_ctx = x, ctx
    out = None
    for l in range(depth):
        ctx_out = l < depth - 1
        lambda_init = 0.8 - 0.6 * math.exp(-0.3 * l)
        gla_qk, gla_v, gla_r, gla_lr, gq, gk, gv, dq, dk, dv = _in_projection(
            x_all, x_ctx, modsel[l], norm_mix[l][None, :], w_in_p[l], gmat,
            jnp.tile(gqa_q_norm[l], GQA_HEADS)[None, :], jnp.tile(gqa_k_norm[l], GQA_KV_HEADS)[None, :], tables)
        o_f, o_b = _gla(gla_qk, gla_v, gla_lr, a2[l], ab[l], n_ctx // GLA_CHUNK)
        lam_rows = jnp.stack([diff_lam_q1[l], diff_lam_k1[l], diff_lam_q2[l], diff_lam_k2[l]], axis=0)
        gqa_o, diff_o = _attention(gq, gk, gv, dq, dk, dv, lam_rows, diff_norm[l][None, :], n_ctx, lambda_init,
                                   ctx_out)
        moe_layer = l % 2 == 1
        i = l // 2
        res = _out_projection(x_all, x_ctx, modsel[l], o_f, o_b, gla_r, gqa_o, diff_o, w_out_b[l], gmat_v,
                              jnp.tile(gla_norm[l], GLA_HEADS)[None, :], norm_ffn[l][None, :],
                              moe_router[i].T if moe_layer else None, ctx_out)
        if not moe_layer:
            x1, h2 = res
            if not ctx_out:
                raise NotImplementedError("dense feed-forward on the last layer")
            x_all = _dense_ffn(h2, x1, modsel[l], ffn_gate[i].astype(BF16), ffn_up[i].astype(BF16),
                               ffn_down[i].astype(BF16), n_ctx)
            x_ctx = None
            if l == depth - 1:
                out = _final_norm(x_all, norm_f[None, :], n_ctx)
        else:
            x1, h2, route = res
            if ctx_out or l != depth - 1:
                raise NotImplementedError("expert feed-forward before the last layer")
            n = bsz * seq
            rows, pos, tile_expert, n_valid, valid_rows = _moe_plan(route.reshape(n, LANES), MOE_TILE)
            xs = _moe_dispatch(pos, h2.reshape(n, d), rows)
            ys = _moe_experts(tile_expert, n_valid, valid_rows, xs, moe_gate[i], moe_up[i], moe_down[i])
            ya = ys.at[pos[0::2]].get(mode="promise_in_bounds").reshape(bsz, seq, d)
            yb = ys.at[pos[1::2]].get(mode="promise_in_bounds").reshape(bsz, seq, d)
            out = _moe_combine(x1, modsel[l], route, ya, yb, norm_f[None, :])
    return out
```

```python
import functools
import math

import jax
import jax.numpy as jnp
import numpy as np
from jax import lax
from jax.experimental import pallas as pl
from jax.experimental.pallas import tpu as pltpu

F32 = jnp.float32
BF16 = jnp.bfloat16

GRID_W = 64
GLA_HEADS, GLA_DK, GLA_DV = 4, 32, 64
GLA_GATE_RANK = 16
GLA_GATE_NORM = 16.0
GLA_CHUNK = 64
GQA_HEADS, GQA_KV_HEADS, GQA_HD = 8, 2, 64
GQA_GROUP = GQA_HEADS // GQA_KV_HEADS
DIFF_HEADS, DIFF_QK, DIFF_V = 4, 32, 64
ROPE_THETA = 10000.0
NORM_EPS = 1e-6
N_EXPERTS = 8

LANES = 128
TOKEN_BLOCK = 256
VMEM_LIMIT = 56 * 1024 * 1024

_W_GLA_QK = 0
_W_GLA_V = 256
_W_GLA_R = 512
_W_GLA_LR = 768
_W_GQA_Q = 896
_W_GQA_K = 1408
_W_GQA_V = 1536
_W_DIFF_Q = 1664
_W_DIFF_K = 1920
_W_DIFF_V = 2176
_W_IN_PAD = 2432
_W_IN_LR_END = 800


def _cparams(*sem):
    return pltpu.CompilerParams(dimension_semantics=sem, vmem_limit_bytes=VMEM_LIMIT)


def _rms(x, w):
    ms = jnp.mean(x * x, axis=-1, keepdims=True)
    return x * lax.rsqrt(ms + NORM_EPS) * w


def _silu(x):
    return x * (1.0 / (1.0 + jnp.exp(-x)))


def _nt_dot(a, b):
    return lax.dot_general(a, b, (((1,), (1,)), ((), ())), preferred_element_type=F32)


def _tn_dot(a, b):
    return lax.dot_general(a, b, (((0,), (0,)), ((), ())), preferred_element_type=F32)


def _mod_kernel(c_ref, w_ref, b_ref, o_ref):
    s = _silu(c_ref[...])
    o_ref[...] = jnp.dot(s.astype(BF16), w_ref[...].astype(BF16),
                         preferred_element_type=F32) + b_ref[...]


def _modulation(c_rows, w_mod, b_mod):
    depth, d, n = w_mod.shape
    tn = 1536
    return pl.pallas_call(
        _mod_kernel,
        grid=(depth, n // tn),
        in_specs=[pl.BlockSpec((8, d), lambda l, j: (0, 0)),
                  pl.BlockSpec((None, d, tn), lambda l, j: (l, 0, j)),
                  pl.BlockSpec((None, 1, tn), lambda l, j: (l, 0, j))],
        out_specs=pl.BlockSpec((None, 8, tn), lambda l, j: (l, 0, j)),
        out_shape=jax.ShapeDtypeStruct((depth, 8, n), F32),
        compiler_params=_cparams("parallel", "parallel"),
        name="modulation",
    )(c_rows, w_mod, b_mod.reshape(depth, 1, n))


def _rope(x, cos, sin_signed, half):
    width = x.shape[1]
    lane = lax.broadcasted_iota(jnp.int32, x.shape, 1)
    first = (lane % (2 * half)) < half
    rot = jnp.where(first, pltpu.roll(x, width - half, 1), pltpu.roll(x, half, 1))
    return x * cos + rot * sin_signed


def _token_rows(refs, split_ctx):
    if not split_ctx:
        return refs[0][...], refs[1:]
    return jnp.where(pl.program_id(1) == 0, refs[0][...], refs[1][...]), refs[2:]


def _token_specs(x, ctx, tm, first):
    d = x.shape[-1]
    if ctx is None:
        return [pl.BlockSpec((None, tm, d), lambda b, j: (b, j + first, 0))], [x]
    assert ctx.shape[1] == tm and first == 0
    return ([pl.BlockSpec((None, tm, d), lambda b, j: (b, 0, 0)),
             pl.BlockSpec((None, tm, d), lambda b, j: (b, jnp.maximum(j - 1, 0), 0))], [ctx, x])


def _values_and_ones(v, head_dim):
    ones = jnp.ones((v.shape[0], head_dim), v.dtype)
    parts = []
    for h in range(v.shape[1] // head_dim):
        parts += [v[:, h * head_dim:(h + 1) * head_dim], ones]
    return jnp.concatenate(parts, axis=1)


def _in_kernel(*refs, split_ctx):
    x, refs = _token_rows(refs, split_ctx)
    (mod_ref, nw_ref, w_ref, g_ref, qg_ref, kg_ref, cg_ref, sg_ref, cd_ref, sd_ref,
     glaqk_ref, glav_ref, glar_ref, glalr_ref, gq_ref, gk_ref, gv_ref, dq_ref, dk_ref, dv_ref) = refs
    mod = mod_ref[...]
    h = _rms(x, nw_ref[...]) * (1.0 + mod[1:2]) + mod[0:1]
    hb = h.astype(BF16)

    def proj(a, b):
        return jnp.dot(hb, w_ref[:, a:b], preferred_element_type=F32)

    glaqk_ref[...] = proj(_W_GLA_QK, _W_GLA_V)
    glav_ref[...] = proj(_W_GLA_V, _W_GLA_R).astype(BF16)
    glar_ref[...] = proj(_W_GLA_R, _W_GLA_LR)
    glalr_ref[...] = proj(_W_GLA_LR, _W_GQA_Q)

    cos_k = cg_ref[...]
    sin_k = sg_ref[...]
    reps = GQA_HEADS * GQA_HD // LANES
    q = proj(_W_GQA_Q, _W_GQA_K)
    ms = jnp.dot((q * q).astype(BF16), g_ref[...], preferred_element_type=F32)
    qn = q * lax.rsqrt(ms + NORM_EPS) * qg_ref[...]
    gq_ref[...] = (_rope(qn, jnp.concatenate([cos_k] * reps, axis=1), jnp.concatenate([sin_k] * reps, axis=1),
                         GQA_HD // 4) * GQA_HD ** -0.5).astype(BF16)

    kw = GQA_KV_HEADS * GQA_HD
    k = proj(_W_GQA_K, _W_GQA_V)
    ms = jnp.dot((k * k).astype(BF16), g_ref[0:kw, 0:kw], preferred_element_type=F32)
    kn = k * lax.rsqrt(ms + NORM_EPS) * kg_ref[...]
    gk_ref[...] = _rope(kn, cos_k, sin_k, GQA_HD // 4).astype(BF16)
    gv_ref[...] = _values_and_ones(proj(_W_GQA_V, _W_DIFF_Q), GQA_HD).astype(BF16)

    reps = 2 * DIFF_HEADS * DIFF_QK // LANES
    cos_d = jnp.concatenate([cd_ref[...]] * reps, axis=1)
    sin_d = jnp.concatenate([sd_ref[...]] * reps, axis=1)
    dq = proj(_W_DIFF_Q, _W_DIFF_K)
    dq_ref[...] = (_rope(dq, cos_d, sin_d, DIFF_QK // 4) * DIFF_QK ** -0.5).astype(BF16)
    dk = proj(_W_DIFF_K, _W_DIFF_V)
    dk_ref[...] = _rope(dk, cos_d, sin_d, DIFF_QK // 4).astype(BF16)
    dv_ref[...] = _values_and_ones(proj(_W_DIFF_V, _W_IN_PAD), DIFF_V).astype(BF16)


def _in_projection(x, ctx, modsel, norm_w, w_in_p, gmat, q_gain, k_gain, tables):
    bsz, s, d = x.shape
    tm = TOKEN_BLOCK
    if ctx is not None:
        s += ctx.shape[1]
    x_specs, x_args = _token_specs(x, ctx, tm, 0)
    cos_g, sin_g, cos_d, sin_d = tables
    row = lambda w: pl.BlockSpec((None, tm, w), lambda b, j: (b, j, 0))
    const = lambda shape: pl.BlockSpec(shape, lambda b, j: (0,) * len(shape))
    tab = lambda w: pl.BlockSpec((tm, w), lambda b, j: (j, 0))
    widths = [(256, F32), (256, BF16), (256, F32), (128, F32), (512, BF16), (128, BF16), (256, BF16),
              (256, BF16), (256, BF16), (512, BF16)]
    return pl.pallas_call(
        functools.partial(_in_kernel, split_ctx=ctx is not None),
        grid=(bsz, s // tm),
        in_specs=x_specs + [
                  pl.BlockSpec((None, None, 8, d), lambda b, j: (b, jnp.minimum(j, 1), 0, 0)),
                  const((1, d)), const((d, _W_IN_PAD)), const((512, 512)),
                  const((1, 512)), const((1, 128)),
                  tab(LANES), tab(LANES), tab(LANES), tab(LANES)],
        out_specs=[row(w) for w, _ in widths],
        out_shape=[jax.ShapeDtypeStruct((bsz, s, w), dt) for w, dt in widths],
        compiler_params=_cparams("parallel", "parallel"),
        name="in_projection",
    )(*x_args, modsel, norm_w, w_in_p, gmat, q_gain, k_gain, cos_g, sin_g, cos_d, sin_d)


def _split3(g):
    g1 = g.astype(BF16)
    r1 = g - g1.astype(F32)
    g2 = r1.astype(BF16)
    g3 = (r1 - g2.astype(F32)).astype(BF16)
    return jnp.concatenate([g1, g2, g3], axis=1)


def _gla_kernel(qkf_ref, vf_ref, lrf_ref, qkb_ref, vb_ref, lrb_ref, a2_ref, ab_ref,
                of_ref, ob_ref, st_ref):
    @pl.when(pl.program_id(0) == 0)
    def _():
        st_ref[...] = jnp.zeros_like(st_ref)

    n_b = qkf_ref.shape[0]
    c = GLA_CHUNK
    hk = GLA_HEADS * GLA_DK
    hv = GLA_HEADS * GLA_DV
    r_i = lax.broadcasted_iota(jnp.int32, (c, c), 0)
    c_i = lax.broadcasted_iota(jnp.int32, (c, c), 1)
    head_of_k = lax.broadcasted_iota(jnp.int32, (c, hk), 1) // GLA_DK
    head_of_v = lax.broadcasted_iota(jnp.int32, (c, hv), 1) // GLA_DV
    dirs = ((qkf_ref, vf_ref, lrf_ref, of_ref, c_i <= r_i, c - 1),
            (qkb_ref, vb_ref, lrb_ref, ob_ref, c_i >= r_i, 0))
    for d, (qk_ref, v_ref, lr_ref, o_ref, tri, last) in enumerate(dirs):
        tri_b = jnp.where(tri, 1.0, 0.0).astype(BF16)
        tri_heads = jnp.concatenate([tri] * GLA_HEADS, axis=0)
        lr = jnp.concatenate([lr_ref[b] for b in range(n_b)], axis=0)
        pre = jnp.dot(lr.astype(BF16), a2_ref[d], preferred_element_type=F32) + ab_ref[d]
        log_a = (jnp.minimum(pre, 0.0) - jnp.log(1.0 + jnp.exp(-jnp.abs(pre)))) * (1.0 / GLA_GATE_NORM)
        parts = _split3(log_a)
        cs = jnp.dot(tri_b, jnp.concatenate([parts[b * c:(b + 1) * c] for b in range(n_b)], axis=1),
                     preferred_element_type=F32)
        for b in range(n_b):
            base = b * 3 * hk
            bc = cs[:, base:base + hk] + cs[:, base + hk:base + 2 * hk] + cs[:, base + 2 * hk:base + 3 * hk]
            tot = bc[last:last + 1, :]
            qk = qk_ref[b]
            q = qk[:, 0:hk] * GLA_DK ** -0.5
            k = qk[:, hk:2 * hk]
            qd = q * jnp.exp(bc)
            kd = (k * jnp.exp(-bc)).astype(BF16)
            kl = (k * jnp.exp(tot - bc)).astype(BF16)
            v = v_ref[b]
            st = st_ref[d, b]
            q_heads = jnp.concatenate([jnp.where(head_of_k == h, qd, 0.0) for h in range(GLA_HEADS)],
                                      axis=0).astype(BF16)
            a = jnp.where(tri_heads, _nt_dot(q_heads, kd), 0.0).astype(BF16)
            o_all = jnp.dot(a, v, preferred_element_type=F32) + _nt_dot(q_heads, st.astype(BF16))
            o = jnp.where(head_of_v == 0, o_all[0:c], 0.0)
            for h in range(1, GLA_HEADS):
                o = o + jnp.where(head_of_v == h, o_all[h * c:(h + 1) * c], 0.0)
            o_ref[b] = o
            st_ref[d, b] = st * jnp.exp(tot) + _tn_dot(v, kl)


def _gla(gla_qk, gla_v, gla_lr, a2, ab, n_ctx_chunks):
    bsz, s, _ = gla_qk.shape
    c = GLA_CHUNK
    n = s // c
    fwd = lambda i: (0, i, 0)
    bwd = lambda i: (0, jnp.where(i < n_ctx_chunks, n_ctx_chunks - 1 - i, n + n_ctx_chunks - 1 - i), 0)
    blk = lambda w, m: pl.BlockSpec((bsz, c, w), m)
    hv = GLA_HEADS * GLA_DV
    return pl.pallas_call(
        _gla_kernel,
        grid=(n,),
        in_specs=[blk(256, fwd), blk(hv, fwd), blk(LANES, fwd),
                  blk(256, bwd), blk(hv, bwd), blk(LANES, bwd),
                  pl.BlockSpec((2, LANES, LANES), lambda i: (0, 0, 0)),
                  pl.BlockSpec((2, 1, LANES), lambda i: (0, 0, 0))],
        out_specs=[blk(hv, fwd), blk(hv, bwd)],
        out_shape=[jax.ShapeDtypeStruct((bsz, s, hv), F32)] * 2,
        scratch_shapes=[pltpu.VMEM((2, bsz, hv, GLA_HEADS * GLA_DK), F32)],
        compiler_params=_cparams("arbitrary"),
        name="gla_scan",
    )(gla_qk, gla_v, gla_lr, gla_qk, gla_v, gla_lr, a2, ab)


def _softmax_attend(q, k, v_ones, k_transposed=False):
    s = jnp.dot(q, k, preferred_element_type=F32) if k_transposed else _nt_dot(q, k)
    p = jnp.exp((s - jnp.max(s, axis=-1, keepdims=True)).astype(BF16))
    o = jnp.dot(p, v_ones, preferred_element_type=F32)
    hd = v_ones.shape[1] // 2
    return o[:, 0:hd] * (1.0 / o[:, hd:hd + 1])


def _attn_kernel(gq_ref, gk_ref, gv_ref, dq_ref, dkt_ref, dv_ref, lam_ref, dnw_ref, go_ref, do_ref,
                 *, n_ctx, out_scale, first_block):
    lam = (jnp.exp(jnp.sum(lam_ref[0:1, :] * lam_ref[1:2, :], axis=-1, keepdims=True))
           - jnp.exp(jnp.sum(lam_ref[2:3, :] * lam_ref[3:4, :], axis=-1, keepdims=True))
           + (1.0 - out_scale))
    tq = gq_ref.shape[0]

    def attend(nk):
        for kh in range(GQA_KV_HEADS):
            heads = [slice((kh * GQA_GROUP + g) * GQA_HD, (kh * GQA_GROUP + g + 1) * GQA_HD)
                     for g in range(GQA_GROUP)]
            q = jnp.concatenate([gq_ref[:, hs] for hs in heads], axis=0)
            o = _softmax_attend(q, gk_ref[0:nk, kh * GQA_HD:(kh + 1) * GQA_HD],
                                gv_ref[0:nk, 2 * kh * GQA_HD:2 * (kh + 1) * GQA_HD])
            for g, hs in enumerate(heads):
                go_ref[:, hs] = o[g * tq:(g + 1) * tq].astype(BF16)
        for h in range(DIFF_HEADS):
            base = h * 2 * DIFF_QK
            v = dv_ref[0:nk, 2 * h * DIFF_V:2 * (h + 1) * DIFF_V]
            o = (_softmax_attend(dq_ref[:, base:base + DIFF_QK], dkt_ref[base:base + DIFF_QK, 0:nk], v, True)
                 - lam * _softmax_attend(dq_ref[:, base + DIFF_QK:base + 2 * DIFF_QK],
                                         dkt_ref[base + DIFF_QK:base + 2 * DIFF_QK, 0:nk], v, True))
            do_ref[:, h * DIFF_V:(h + 1) * DIFF_V] = (_rms(o, dnw_ref[...]) * out_scale).astype(BF16)

    if first_block == 0:
        @pl.when(pl.program_id(1) == 0)
        def _():
            attend(n_ctx)

        @pl.when(pl.program_id(1) > 0)
        def _():
            attend(gk_ref.shape[0])
    else:
        attend(gk_ref.shape[0])


def _attention(gq, gk, gv, dq, dk, dv, lam_rows, diff_norm_w, n_ctx, lambda_init, ctx_out):
    bsz, s, _ = gq.shape
    tq = TOKEN_BLOCK
    assert n_ctx == tq
    first = 0 if ctx_out else 1
    s_out = s - first * tq
    kw = GQA_KV_HEADS * GQA_HD
    qrow = lambda w: pl.BlockSpec((None, tq, w), lambda b, j: (b, j + first, 0))
    orow = lambda w: pl.BlockSpec((None, tq, w), lambda b, j: (b, j, 0))
    full = lambda w: pl.BlockSpec((None, s, w), lambda b, j: (b, 0, 0))
    return pl.pallas_call(
        functools.partial(_attn_kernel, n_ctx=n_ctx, out_scale=1.0 - lambda_init, first_block=first),
        grid=(bsz, s_out // tq),
        in_specs=[qrow(512), full(kw), full(2 * kw), qrow(256),
                  pl.BlockSpec((None, dk.shape[2], s), lambda b, j: (b, 0, 0)), full(512),
                  pl.BlockSpec((4, DIFF_QK), lambda b, j: (0, 0)),
                  pl.BlockSpec((1, DIFF_V), lambda b, j: (0, 0))],
        out_specs=[orow(512), orow(256)],
        out_shape=[jax.ShapeDtypeStruct((bsz, s_out, 512), BF16),
                   jax.ShapeDtypeStruct((bsz, s_out, 256), BF16)],
        compiler_params=_cparams("parallel", "parallel"),
        name="attention",
    )(gq, gk, gv, dq, jnp.swapaxes(dk, 1, 2), dv, lam_rows, diff_norm_w)


def _route(h, wr_ref):
    logits = [jnp.sum(h * wr_ref[e:e + 1, :], axis=-1, keepdims=True) for e in range(N_EXPERTS)]

    def top(ls):
        m = functools.reduce(jnp.maximum, ls)
        idx = jnp.full_like(m, float(N_EXPERTS))
        for e in reversed(range(N_EXPERTS)):
            idx = jnp.where(ls[e] == m, float(e), idx)
        return m, idx

    m1, i1 = top(logits)
    m2, i2 = top([jnp.where(i1 == float(e), -jnp.inf, logits[e]) for e in range(N_EXPERTS)])
    e2 = jnp.exp(m2 - m1)
    w1 = 1.0 / (1.0 + e2)
    w2 = e2 * w1
    lane = lax.broadcasted_iota(jnp.int32, (h.shape[0], LANES), 1)
    return jnp.where(lane == 0, w1, jnp.where(lane == 1, w2, jnp.where(lane == 2, i1,
                     jnp.where(lane == 3, i2, 0.0))))


def _out_kernel(*refs, with_router, split_ctx):
    x, refs = _token_rows(refs, split_ctx)
    if with_router:
        (mod_ref, of_ref, ob_ref, r_ref, go_ref, do_ref, w_ref, g_ref, gnw_ref, nw_ref, wr_ref,
         x1_ref, h2_ref, route_ref) = refs
    else:
        (mod_ref, of_ref, ob_ref, r_ref, go_ref, do_ref, w_ref, g_ref, gnw_ref, nw_ref,
         x1_ref, h2_ref) = refs
    mod = mod_ref[...]
    hv = GLA_HEADS * GLA_DV
    gw = GQA_HEADS * GQA_HD
    o = of_ref[...] + ob_ref[...]
    ms = jnp.dot((o * o).astype(BF16), g_ref[...], preferred_element_type=F32)
    gla = o * lax.rsqrt(ms + NORM_EPS) * gnw_ref[...] * _silu(r_ref[...])
    y = (jnp.dot(gla.astype(BF16), w_ref[0:hv, :], preferred_element_type=F32)
         + jnp.dot(go_ref[...], w_ref[hv:hv + gw, :], preferred_element_type=F32)
         + jnp.dot(do_ref[...], w_ref[hv + gw:, :], preferred_element_type=F32))
    x1 = x + mod[2:3] * y
    x1_ref[...] = x1
    h2 = _rms(x1, nw_ref[...]) * (1.0 + mod[4:5]) + mod[3:4]
    h2_ref[...] = h2.astype(h2_ref.dtype)
    if with_router:
        route_ref[...] = _route(h2, wr_ref)


def _out_projection(x, ctx, modsel, o_f, o_b, gla_r, gqa_o, diff_o, w_out, gmat, gla_norm_w, norm_ffn_w,
                    router_t, ctx_out):
    bsz, s, d = x.shape
    tm = TOKEN_BLOCK
    if ctx is not None:
        s += ctx.shape[1]
    first = 0 if ctx_out else 1
    s_out = s - first * tm
    x_specs, x_args = _token_specs(x, ctx, tm, first)
    arow = lambda w: pl.BlockSpec((None, tm, w), lambda b, j: (b, j + first, 0))
    orow = lambda w: pl.BlockSpec((None, tm, w), lambda b, j: (b, j, 0))
    const = lambda shape: pl.BlockSpec(shape, lambda b, j: (0,) * len(shape))
    with_router = router_t is not None
    in_specs = x_specs + [
                pl.BlockSpec((None, None, 8, d), lambda b, j: (b, jnp.minimum(j + first, 1), 0, 0)),
                arow(256), arow(256), arow(256), orow(512), orow(256),
                const((d, d)), const((256, 256)), const((1, 256)), const((1, d))]
    args = x_args + [modsel, o_f, o_b, gla_r, gqa_o, diff_o, w_out, gmat, gla_norm_w, norm_ffn_w]
    out_specs = [orow(d), orow(d)]
    out_shape = [jax.ShapeDtypeStruct((bsz, s_out, d), F32),
                 jax.ShapeDtypeStruct((bsz, s_out, d), F32 if with_router else BF16)]
    if with_router:
        in_specs.append(const((N_EXPERTS, d)))
        args.append(router_t)
        out_specs.append(orow(LANES))
        out_shape.append(jax.ShapeDtypeStruct((bsz, s_out, LANES), F32))
    return pl.pallas_call(
        functools.partial(_out_kernel, with_router=with_router, split_ctx=ctx is not None),
        grid=(bsz, s_out // tm),
        in_specs=in_specs, out_specs=out_specs, out_shape=out_shape,
        compiler_params=_cparams("parallel", "parallel"),
        name="out_projection",
    )(*args)


def _ffn_kernel(h_ref, x_ref, mod_ref, wg_ref, wu_ref, wd_ref, o_ref, acc_ref, *, n_ctx):
    f = pl.program_id(2)

    @pl.when(f == 0)
    def _():
        acc_ref[...] = jnp.zeros_like(acc_ref)

    h = h_ref[...]
    g = jnp.dot(h, wg_ref[...], preferred_element_type=F32)
    u = jnp.dot(h, wu_ref[...], preferred_element_type=F32)
    acc_ref[...] += jnp.dot((_silu(g) * u).astype(BF16), wd_ref[...], preferred_element_type=F32)

    @pl.when(f == pl.num_programs(2) - 1)
    def _():
        tm = h_ref.shape[0]
        row = pl.program_id(1) * tm + lax.broadcasted_iota(jnp.int32, (tm, 1), 0)
        gate = jnp.where(row < n_ctx, mod_ref[0, 5:6, :], mod_ref[1, 5:6, :])
        o_ref[...] = x_ref[...] + gate * acc_ref[...]


def _dense_ffn(h2, x1, modsel, wg, wu, wd, n_ctx):
    bsz, s, d = x1.shape
    ff = wg.shape[1]
    tm = s // 8
    tf = ff // 2
    row = lambda: pl.BlockSpec((None, tm, d), lambda b, t, f: (b, t, 0))
    return pl.pallas_call(
        functools.partial(_ffn_kernel, n_ctx=n_ctx),
        grid=(bsz, s // tm, ff // tf),
        in_specs=[row(), row(),
                  pl.BlockSpec((None, 2, 8, d), lambda b, t, f: (b, 0, 0, 0)),
                  pl.BlockSpec((d, tf), lambda b, t, f: (0, f)),
                  pl.BlockSpec((d, tf), lambda b, t, f: (0, f)),
                  pl.BlockSpec((tf, d), lambda b, t, f: (f, 0))],
        out_specs=row(),
        out_shape=jax.ShapeDtypeStruct((bsz, s, d), F32),
        scratch_shapes=[pltpu.VMEM((tm, d), F32)],
        compiler_params=_cparams("parallel", "parallel", "arbitrary"),
        name="dense_ffn",
    )(h2, x1, modsel, wg, wu, wd)


MOE_TILE = 1024
MOE_SUB = 256
MOE_FF_BLOCK = 512


def _moe_kernel(te_ref, nv_ref, vr_ref, xs_ref, wg_ref, wu_ref, wd_ref, o_ref, xb_ref):
    f = pl.program_id(1)
    last_f = pl.num_programs(1) - 1
    valid = vr_ref[pl.program_id(0)]

    @pl.when(valid > 0)
    def _():
        wg = wg_ref[...].astype(BF16)
        wu = wu_ref[...].astype(BF16)
        wd = wd_ref[...].astype(BF16)
        for sb in range(MOE_TILE // MOE_SUB):
            rows = pl.ds(sb * MOE_SUB, MOE_SUB)

            @pl.when(sb * MOE_SUB < valid)
            def _():
                @pl.when(f == 0)
                def _():
                    o_ref[rows, :] = jnp.zeros((MOE_SUB, o_ref.shape[1]), F32)
                    xb_ref[rows, :] = xs_ref[rows, :].astype(BF16)

                h = xb_ref[rows, :]
                g = jnp.dot(h, wg, preferred_element_type=F32)
                u = jnp.dot(h, wu, preferred_element_type=F32)
                o_ref[rows, :] += jnp.dot((_silu(g) * u).astype(BF16), wd, preferred_element_type=F32)

            @pl.when(jnp.logical_and(sb * MOE_SUB >= valid, f == last_f))
            def _():
                o_ref[rows, :] = jnp.zeros((MOE_SUB, o_ref.shape[1]), F32)

    @pl.when(jnp.logical_and(valid == 0, f == last_f))
    def _():
        o_ref[...] = jnp.zeros_like(o_ref)


def _moe_experts(tile_expert, n_valid, valid_rows, xs, wg, wu, wd):
    rows, d = xs.shape
    ff = wg.shape[2]
    tm = MOE_TILE
    tf = MOE_FF_BLOCK
    nf = ff // tf

    def tile(i, nv):
        return jnp.minimum(i, nv[0] - 1)

    def fidx(i, f, nv):
        return jnp.where(i < nv[0], f, nf - 1)

    return pl.pallas_call(
        _moe_kernel,
        grid_spec=pltpu.PrefetchScalarGridSpec(
            num_scalar_prefetch=3,
            grid=(rows // tm, nf),
            in_specs=[pl.BlockSpec((tm, d), lambda i, f, te, nv, vr: (tile(i, nv), 0)),
                      pl.BlockSpec((None, d, tf), lambda i, f, te, nv, vr: (te[tile(i, nv)], 0, fidx(i, f, nv))),
                      pl.BlockSpec((None, d, tf), lambda i, f, te, nv, vr: (te[tile(i, nv)], 0, fidx(i, f, nv))),
                      pl.BlockSpec((None, tf, d), lambda i, f, te, nv, vr: (te[tile(i, nv)], fidx(i, f, nv), 0))],
            out_specs=pl.BlockSpec((tm, d), lambda i, f, te, nv, vr: (i, 0)),
            scratch_shapes=[pltpu.VMEM((tm, d), BF16)]),
        out_shape=jax.ShapeDtypeStruct((rows, d), F32),
        compiler_params=_cparams("arbitrary", "arbitrary"),
        name="moe_experts",
    )(tile_expert, n_valid, valid_rows, xs, wg, wu, wd)


def _dispatch_kernel(pos_ref, h_ref, init_ref, xs_ref, sem):
    del init_ref
    tm = h_ref.shape[0]
    base = pl.program_id(0) * (2 * tm)
    for r in range(tm):
        for k in range(2):
            pltpu.make_async_copy(h_ref.at[pl.ds(r, 1), :],
                                  xs_ref.at[pl.ds(pos_ref[base + 2 * r + k], 1), :], sem).start()
    for k in range(2):
        pltpu.make_async_copy(h_ref, xs_ref.at[pl.ds(0, tm), :], sem).wait()


def _moe_dispatch(pos_flat, h2, rows):
    n, d = h2.shape
    tm = TOKEN_BLOCK
    return pl.pallas_call(
        _dispatch_kernel,
        grid_spec=pltpu.PrefetchScalarGridSpec(
            num_scalar_prefetch=1,
            grid=(n // tm,),
            in_specs=[pl.BlockSpec((tm, d), lambda i, pos: (i, 0)),
                      pl.BlockSpec(memory_space=pl.ANY)],
            out_specs=pl.BlockSpec(memory_space=pl.ANY),
            scratch_shapes=[pltpu.SemaphoreType.DMA(())]),
        out_shape=jax.ShapeDtypeStruct((rows, d), F32),
        input_output_aliases={2: 0},
        compiler_params=_cparams("arbitrary"),
        name="moe_dispatch",
    )(pos_flat, h2, jnp.zeros((rows, d), F32))


def _combine_kernel(x_ref, mod_ref, route_ref, ya_ref, yb_ref, nf_ref, o_ref):
    route = route_ref[...]
    moe = route[:, 0:1] * ya_ref[...] + route[:, 1:2] * yb_ref[...]
    x2 = x_ref[...] + mod_ref[5:6, :] * moe
    o_ref[...] = _rms(x2, nf_ref[...])


def _moe_combine(x1, modsel, route, ya, yb, norm_f_w):
    bsz, s, d = x1.shape
    tm = TOKEN_BLOCK
    row = lambda w: pl.BlockSpec((None, tm, w), lambda b, j: (b, j, 0))
    return pl.pallas_call(
        _combine_kernel,
        grid=(bsz, s // tm),
        in_specs=[row(d), pl.BlockSpec((None, None, 8, d), lambda b, j: (b, 1, 0, 0)),
                  row(LANES), row(d), row(d), pl.BlockSpec((1, d), lambda b, j: (0, 0))],
        out_specs=row(d),
        out_shape=jax.ShapeDtypeStruct((bsz, s, d), F32),
        compiler_params=_cparams("parallel", "parallel"),
        name="moe_combine",
    )(x1, modsel, route, ya, yb, norm_f_w)


def _final_norm_kernel(x_ref, nf_ref, o_ref):
    o_ref[...] = _rms(x_ref[...], nf_ref[...])


def _final_norm(x, norm_f_w, first_row):
    bsz, s, d = x.shape
    tm = TOKEN_BLOCK
    first = first_row // tm
    return pl.pallas_call(
        _final_norm_kernel,
        grid=(bsz, (s - first_row) // tm),
        in_specs=[pl.BlockSpec((None, tm, d), lambda b, j: (b, j + first, 0)),
                  pl.BlockSpec((1, d), lambda b, j: (0, 0))],
        out_specs=pl.BlockSpec((None, tm, d), lambda b, j: (b, j, 0)),
        out_shape=jax.ShapeDtypeStruct((bsz, s - first_row, d), F32),
        compiler_params=_cparams("parallel", "parallel"),
        name="final_norm",
    )(x, norm_f_w)


def _moe_plan(route, tm):
    n = route.shape[0]
    e_flat = route[:, 2:4].astype(jnp.int32).reshape(-1)
    onehot = (e_flat[:, None] == jnp.arange(N_EXPERTS, dtype=jnp.int32)[None, :]).astype(jnp.int32)
    csum = jnp.cumsum(onehot, axis=0)
    counts = csum[-1]
    padded = ((counts + tm - 1) // tm) * tm
    ends = jnp.cumsum(padded)
    pos = jnp.sum(onehot * (csum - 1 + (ends - padded)[None, :]), axis=1)
    rows = 2 * n + N_EXPERTS * tm
    n_tiles = rows // tm
    starts = jnp.arange(n_tiles, dtype=jnp.int32) * tm
    tile_expert = jnp.minimum(jnp.sum((ends[None, :] <= starts[:, None]).astype(jnp.int32), axis=1),
                              N_EXPERTS - 1)
    n_valid = (ends[-1] // tm).astype(jnp.int32).reshape(1)
    token_end = (ends - padded + counts)[tile_expert]
    valid_rows = jnp.clip(token_end - starts, 0, tm).astype(jnp.int32)
    return rows, pos.astype(jnp.int32), tile_expert, n_valid, valid_rows


def _rope_tables(n_ctx, seq, head_dim, copies):
    half = head_dim // 4
    t = np.arange(seq)
    freqs = ROPE_THETA ** (-jnp.arange(half, dtype=F32) / half)
    rows = jnp.asarray(t // GRID_W, F32)[:, None] * freqs[None, :]
    cols = jnp.asarray(t % GRID_W, F32)[:, None] * freqs[None, :]
    cos = jnp.concatenate([jnp.cos(rows)] * 2 + [jnp.cos(cols)] * 2, axis=1)
    sin = jnp.concatenate([-jnp.sin(rows), jnp.sin(rows), -jnp.sin(cols), jnp.sin(cols)], axis=1)
    cos = jnp.concatenate([jnp.ones((n_ctx, head_dim), F32), cos], axis=0)
    sin = jnp.concatenate([jnp.zeros((n_ctx, head_dim), F32), sin], axis=0)
    return jnp.tile(cos, (1, copies)), jnp.tile(sin, (1, copies))


def _group_mean_matrix(width, group):
    g = np.kron(np.eye(width // group, dtype=np.float32), np.ones((group, group), np.float32)) / group
    return jnp.asarray(g, BF16)


def kernel(x, c, ctx, c_ctx, w_mod, b_mod, norm_mix, norm_ffn, w_in, w_out, gla_a2_f, gla_ab_f, gla_a2_b,
           gla_ab_b, gla_norm, gqa_q_norm, gqa_k_norm, diff_lam_q1, diff_lam_k1, diff_lam_q2, diff_lam_k2,
           diff_norm, ffn_gate, ffn_up, ffn_down, moe_router, moe_gate, moe_up, moe_down, norm_f):
    bsz, seq, d = x.shape
    n_ctx = ctx.shape[1]
    depth = w_mod.shape[0]

    c_rows = jnp.concatenate([c, c_ctx[None, :], jnp.zeros((8 - bsz - 1, d), F32)], axis=0)
    mod = _modulation(c_rows, w_mod, b_mod).reshape(depth, 8, 6, d)
    mod = jnp.pad(mod, ((0, 0), (0, 0), (0, 2), (0, 0)))
    modsel = jnp.stack([jnp.broadcast_to(mod[:, bsz][:, None], (depth, bsz, 8, d)), mod[:, :bsz]], axis=2)

    tables = (_rope_tables(n_ctx, seq, GQA_HD, LANES // GQA_HD) + _rope_tables(n_ctx, seq, DIFF_QK, LANES // DIFF_QK))
    gmat = _group_mean_matrix(GQA_HEADS * GQA_HD, GQA_HD)
    gmat_v = _group_mean_matrix(GLA_HEADS * GLA_DV, GLA_DV)

    w_in_p = jnp.concatenate([w_in[:, :, :_W_IN_LR_END],
                              jnp.zeros((depth, d, _W_GQA_Q - _W_IN_LR_END), F32),
                              w_in[:, :, _W_IN_LR_END:]], axis=2).astype(BF16)
    w_out_b = w_out.astype(BF16)
    a2 = jnp.zeros((depth, 2, LANES, LANES), F32)
    a2 = a2.at[:, 0, 0:GLA_GATE_RANK].set(gla_a2_f).at[:, 1, GLA_GATE_RANK:2 * GLA_GATE_RANK].set(gla_a2_b)
    a2 = a2.astype(BF16)
    ab = jnp.stack([gla_ab_f, gla_ab_b], axis=1)[:, :, None, :]

    x_all, x_ctx = x, ctx
    out = None
    for l in range(depth):
        ctx_out = l < depth - 1
        lambda_init = 0.8 - 0.6 * math.exp(-0.3 * l)
        gla_qk, gla_v, gla_r, gla_lr, gq, gk, gv, dq, dk, dv = _in_projection(
            x_all, x_ctx, modsel[l], norm_mix[l][None, :], w_in_p[l], gmat,
            jnp.tile(gqa_q_norm[l], GQA_HEADS)[None, :], jnp.tile(gqa_k_norm[l], GQA_KV_HEADS)[None, :], tables)
        o_f, o_b = _gla(gla_qk, gla_v, gla_lr, a2[l], ab[l], n_ctx // GLA_CHUNK)
        lam_rows = jnp.stack([diff_lam_q1[l], diff_lam_k1[l], diff_lam_q2[l], diff_lam_k2[l]], axis=0)
        gqa_o, diff_o = _attention(gq, gk, gv, dq, dk, dv, lam_rows, diff_norm[l][None, :], n_ctx, lambda_init,
                                   ctx_out)
        moe_layer = l % 2 == 1
        i = l // 2
        res = _out_projection(x_all, x_ctx, modsel[l], o_f, o_b, gla_r, gqa_o, diff_o, w_out_b[l], gmat_v,
                              jnp.tile(gla_norm[l], GLA_HEADS)[None, :], norm_ffn[l][None, :],
                              moe_router[i].T if moe_layer else None, ctx_out)
        if not moe_layer:
            x1, h2 = res
            if not ctx_out:
                raise NotImplementedError("dense feed-forward on the last layer")
            x_all = _dense_ffn(h2, x1, modsel[l], ffn_gate[i].astype(BF16), ffn_up[i].astype(BF16),
                               ffn_down[i].astype(BF16), n_ctx)
            x_ctx = None
            if l == depth - 1:
                out = _final_norm(x_all, norm_f[None, :], n_ctx)
        else:
            x1, h2, route = res
            if ctx_out or l != depth - 1:
                raise NotImplementedError("expert feed-forward before the last layer")
            n = bsz * seq
            rows, pos, tile_expert, n_valid, valid_rows = _moe_plan(route.reshape(n, LANES), MOE_TILE)
            xs = _moe_dispatch(pos, h2.reshape(n, d), rows)
            ys = _moe_experts(tile_expert, n_valid, valid_rows, xs, moe_gate[i], moe_up[i], moe_down[i])
            ya = ys.at[pos[0::2]].get(mode="promise_in_bounds").reshape(bsz, seq, d)
            yb = ys.at[pos[1::2]].get(mode="promise_in_bounds").reshape(bsz, seq, d)
            out = _moe_combine(x1, modsel[l], route, ya, yb, norm_f[None, :])
    return out
```

```python
import functools
import math

import jax
import jax.numpy as jnp
import numpy as np
from jax import lax
from jax.experimental import pallas as pl
from jax.experimental.pallas import tpu as pltpu

F32 = jnp.float32
BF16 = jnp.bfloat16

GRID_W = 64
GLA_HEADS, GLA_DK, GLA_DV = 4, 32, 64
GLA_GATE_RANK = 16
GLA_GATE_NORM = 16.0
GLA_CHUNK = 64
GLA_STEP_CHUNKS = 2
GQA_HEADS, GQA_KV_HEADS, GQA_HD = 8, 2, 64
GQA_GROUP = GQA_HEADS // GQA_KV_HEADS
DIFF_HEADS, DIFF_QK, DIFF_V = 4, 32, 64
ROPE_THETA = 10000.0
NORM_EPS = 1e-6
N_EXPERTS = 8

LANES = 128
TOKEN_BLOCK = 256
VMEM_LIMIT = 56 * 1024 * 1024

_W_GLA_QK = 0
_W_GLA_V = 256
_W_GLA_R = 512
_W_GLA_LR = 768
_W_GQA_Q = 896
_W_GQA_K = 1408
_W_GQA_V = 1536
_W_DIFF_Q = 1664
_W_DIFF_K = 1920
_W_DIFF_V = 2176
_W_IN_PAD = 2432
_W_IN_LR_END = 800


def _cparams(*sem):
    return pltpu.CompilerParams(dimension_semantics=sem, vmem_limit_bytes=VMEM_LIMIT)


def _rms(x, w):
    ms = jnp.mean(x * x, axis=-1, keepdims=True)
    return x * lax.rsqrt(ms + NORM_EPS) * w


def _silu(x):
    return x * (1.0 / (1.0 + jnp.exp(-x)))


def _nt_dot(a, b):
    return lax.dot_general(a, b, (((1,), (1,)), ((), ())), preferred_element_type=F32)


def _tn_dot(a, b):
    return lax.dot_general(a, b, (((0,), (0,)), ((), ())), preferred_element_type=F32)


def _mod_kernel(c_ref, w_ref, b_ref, o_ref):
    s = _silu(c_ref[...])
    o_ref[...] = jnp.dot(s.astype(BF16), w_ref[...].astype(BF16),
                         preferred_element_type=F32) + b_ref[...]


def _modulation(c_rows, w_mod, b_mod):
    depth, d, n = w_mod.shape
    tn = 1536
    return pl.pallas_call(
        _mod_kernel,
        grid=(depth, n // tn),
        in_specs=[pl.BlockSpec((8, d), lambda l, j: (0, 0)),
                  pl.BlockSpec((None, d, tn), lambda l, j: (l, 0, j)),
                  pl.BlockSpec((None, 1, tn), lambda l, j: (l, 0, j))],
        out_specs=pl.BlockSpec((None, 8, tn), lambda l, j: (l, 0, j)),
        out_shape=jax.ShapeDtypeStruct((depth, 8, n), F32),
        compiler_params=_cparams("parallel", "parallel"),
        name="modulation",
    )(c_rows, w_mod, b_mod.reshape(depth, 1, n))


def _rope(x, cos, sin_signed, half):
    width = x.shape[1]
    lane = lax.broadcasted_iota(jnp.int32, x.shape, 1)
    first = (lane % (2 * half)) < half
    rot = jnp.where(first, pltpu.roll(x, width - half, 1), pltpu.roll(x, half, 1))
    return x * cos + rot * sin_signed


def _token_rows(refs, split_ctx):
    if not split_ctx:
        return refs[0][...], refs[1:]
    return jnp.where(pl.program_id(1) == 0, refs[0][...], refs[1][...]), refs[2:]


def _token_specs(x, ctx, tm, first):
    d = x.shape[-1]
    if ctx is None:
        return [pl.BlockSpec((None, tm, d), lambda b, j: (b, j + first, 0))], [x]
    assert ctx.shape[1] == tm and first == 0
    return ([pl.BlockSpec((None, tm, d), lambda b, j: (b, 0, 0)),
             pl.BlockSpec((None, tm, d), lambda b, j: (b, jnp.maximum(j - 1, 0), 0))], [ctx, x])


def _values_and_ones(v, head_dim):
    ones = jnp.ones((v.shape[0], head_dim), v.dtype)
    parts = []
    for h in range(v.shape[1] // head_dim):
        parts += [v[:, h * head_dim:(h + 1) * head_dim], ones]
    return jnp.concatenate(parts, axis=1)


def _in_kernel(*refs, split_ctx):
    x, refs = _token_rows(refs, split_ctx)
    (mod_ref, nw_ref, w_ref, g_ref, qg_ref, kg_ref, cg_ref, sg_ref, cd_ref, sd_ref,
     glaqk_ref, glav_ref, glar_ref, glalr_ref, gq_ref, gk_ref, gv_ref, dq_ref, dk_ref, dv_ref) = refs
    mod = mod_ref[...]
    h = _rms(x, nw_ref[...]) * (1.0 + mod[1:2]) + mod[0:1]
    hb = h.astype(BF16)

    def proj(a, b):
        return jnp.dot(hb, w_ref[:, a:b], preferred_element_type=F32)

    glaqk_ref[...] = proj(_W_GLA_QK, _W_GLA_V)
    glav_ref[...] = proj(_W_GLA_V, _W_GLA_R).astype(BF16)
    glar_ref[...] = proj(_W_GLA_R, _W_GLA_LR)
    glalr_ref[...] = proj(_W_GLA_LR, _W_GQA_Q)

    cos_k = cg_ref[...]
    sin_k = sg_ref[...]
    reps = GQA_HEADS * GQA_HD // LANES
    q = proj(_W_GQA_Q, _W_GQA_K)
    ms = jnp.dot((q * q).astype(BF16), g_ref[...], preferred_element_type=F32)
    qn = q * lax.rsqrt(ms + NORM_EPS) * qg_ref[...]
    gq_ref[...] = (_rope(qn, jnp.concatenate([cos_k] * reps, axis=1), jnp.concatenate([sin_k] * reps, axis=1),
                         GQA_HD // 4) * GQA_HD ** -0.5).astype(BF16)

    kw = GQA_KV_HEADS * GQA_HD
    k = proj(_W_GQA_K, _W_GQA_V)
    ms = jnp.dot((k * k).astype(BF16), g_ref[0:kw, 0:kw], preferred_element_type=F32)
    kn = k * lax.rsqrt(ms + NORM_EPS) * kg_ref[...]
    gk_ref[...] = _rope(kn, cos_k, sin_k, GQA_HD // 4).astype(BF16)
    gv_ref[...] = _values_and_ones(proj(_W_GQA_V, _W_DIFF_Q), GQA_HD).astype(BF16)

    reps = 2 * DIFF_HEADS * DIFF_QK // LANES
    cos_d = jnp.concatenate([cd_ref[...]] * reps, axis=1)
    sin_d = jnp.concatenate([sd_ref[...]] * reps, axis=1)
    dq = proj(_W_DIFF_Q, _W_DIFF_K)
    dq_ref[...] = (_rope(dq, cos_d, sin_d, DIFF_QK // 4) * DIFF_QK ** -0.5).astype(BF16)
    dk = proj(_W_DIFF_K, _W_DIFF_V)
    dk_ref[...] = _rope(dk, cos_d, sin_d, DIFF_QK // 4).astype(BF16)
    dv_ref[...] = _values_and_ones(proj(_W_DIFF_V, _W_IN_PAD), DIFF_V).astype(BF16)


def _in_projection(x, ctx, modsel, norm_w, w_in_p, gmat, q_gain, k_gain, tables):
    bsz, s, d = x.shape
    tm = TOKEN_BLOCK
    if ctx is not None:
        s += ctx.shape[1]
    x_specs, x_args = _token_specs(x, ctx, tm, 0)
    cos_g, sin_g, cos_d, sin_d = tables
    row = lambda w: pl.BlockSpec((None, tm, w), lambda b, j: (b, j, 0))
    const = lambda shape: pl.BlockSpec(shape, lambda b, j: (0,) * len(shape))
    tab = lambda w: pl.BlockSpec((tm, w), lambda b, j: (j, 0))
    widths = [(256, F32), (256, BF16), (256, F32), (128, F32), (512, BF16), (128, BF16), (256, BF16),
              (256, BF16), (256, BF16), (512, BF16)]
    return pl.pallas_call(
        functools.partial(_in_kernel, split_ctx=ctx is not None),
        grid=(bsz, s // tm),
        in_specs=x_specs + [
                  pl.BlockSpec((None, None, 8, d), lambda b, j: (b, jnp.minimum(j, 1), 0, 0)),
                  const((1, d)), const((d, _W_IN_PAD)), const((512, 512)),
                  const((1, 512)), const((1, 128)),
                  tab(LANES), tab(LANES), tab(LANES), tab(LANES)],
        out_specs=[row(w) for w, _ in widths],
        out_shape=[jax.ShapeDtypeStruct((bsz, s, w), dt) for w, dt in widths],
        compiler_params=_cparams("parallel", "parallel"),
        name="in_projection",
    )(*x_args, modsel, norm_w, w_in_p, gmat, q_gain, k_gain, cos_g, sin_g, cos_d, sin_d)


def _split3(g):
    g1 = g.astype(BF16)
    r1 = g - g1.astype(F32)
    g2 = r1.astype(BF16)
    g3 = (r1 - g2.astype(F32)).astype(BF16)
    return jnp.concatenate([g1, g2, g3], axis=1)


def _gla_kernel(qkf_ref, vf_ref, lrf_ref, qkb_ref, vb_ref, lrb_ref, a2_ref, ab_ref,
                of_ref, ob_ref, st_ref):
    @pl.when(pl.program_id(0) == 0)
    def _():
        st_ref[...] = jnp.zeros_like(st_ref)

    n_b = qkf_ref.shape[0]
    c = GLA_CHUNK
    hk = GLA_HEADS * GLA_DK
    hv = GLA_HEADS * GLA_DV
    r_i = lax.broadcasted_iota(jnp.int32, (c, c), 0)
    c_i = lax.broadcasted_iota(jnp.int32, (c, c), 1)
    head_of_k = lax.broadcasted_iota(jnp.int32, (c, hk), 1) // GLA_DK
    head_of_v = lax.broadcasted_iota(jnp.int32, (c, hv), 1) // GLA_DV
    dirs = ((qkf_ref, vf_ref, lrf_ref, of_ref, c_i <= r_i, c - 1),
            (qkb_ref, vb_ref, lrb_ref, ob_ref, c_i >= r_i, 0))
    n_sub = qkf_ref.shape[1] // c
    for d, (qk_ref, v_ref, lr_ref, o_ref, tri, last) in enumerate(dirs):
        tri_b = jnp.where(tri, 1.0, 0.0).astype(BF16)
        tri_heads = jnp.concatenate([tri] * GLA_HEADS, axis=0)
        lr = jnp.concatenate([lr_ref[b] for b in range(n_b)], axis=0)
        pre = jnp.dot(lr.astype(BF16), a2_ref[d], preferred_element_type=F32) + ab_ref[d]
        log_a = (jnp.minimum(pre, 0.0) - jnp.log(1.0 + jnp.exp(-jnp.abs(pre)))) * (1.0 / GLA_GATE_NORM)
        parts = _split3(log_a)
        cs = jnp.dot(tri_b, jnp.concatenate([parts[u * c:(u + 1) * c] for u in range(n_b * n_sub)], axis=1),
                     preferred_element_type=F32)
        for b in range(n_b):
            st = st_ref[d, b]
            for sub in (range(n_sub) if d == 0 else reversed(range(n_sub))):
                rows = slice(sub * c, (sub + 1) * c)
                base = (b * n_sub + sub) * 3 * hk
                bc = cs[:, base:base + hk] + cs[:, base + hk:base + 2 * hk] + cs[:, base + 2 * hk:base + 3 * hk]
                tot = bc[last:last + 1, :]
                qk = qk_ref[b, rows, :]
                q = qk[:, 0:hk] * GLA_DK ** -0.5
                k = qk[:, hk:2 * hk]
                qd = q * jnp.exp(bc)
                kd = (k * jnp.exp(-bc)).astype(BF16)
                kl = (k * jnp.exp(tot - bc)).astype(BF16)
                v = v_ref[b, rows, :]
                q_heads = jnp.concatenate([jnp.where(head_of_k == h, qd, 0.0) for h in range(GLA_HEADS)],
                                          axis=0).astype(BF16)
                a = jnp.where(tri_heads, _nt_dot(q_heads, kd), 0.0).astype(BF16)
                o_all = jnp.dot(a, v, preferred_element_type=F32) + _nt_dot(q_heads, st.astype(BF16))
                o = jnp.where(head_of_v == 0, o_all[0:c], 0.0)
                for h in range(1, GLA_HEADS):
                    o = o + jnp.where(head_of_v == h, o_all[h * c:(h + 1) * c], 0.0)
                o_ref[b, rows, :] = o
                st = st * jnp.exp(tot) + _tn_dot(v, kl)
            st_ref[d, b] = st


def _gla(gla_qk, gla_v, gla_lr, a2, ab, n_ctx_chunks):
    bsz, s, _ = gla_qk.shape
    c = GLA_STEP_CHUNKS * GLA_CHUNK
    n = s // c
    n_ctx_steps = n_ctx_chunks // GLA_STEP_CHUNKS
    assert n * c == s and n_ctx_steps * GLA_STEP_CHUNKS == n_ctx_chunks
    fwd = lambda i: (0, i, 0)
    bwd = lambda i: (0, jnp.where(i < n_ctx_steps, n_ctx_steps - 1 - i, n + n_ctx_steps - 1 - i), 0)
    blk = lambda w, m: pl.BlockSpec((bsz, c, w), m)
    hv = GLA_HEADS * GLA_DV
    return pl.pallas_call(
        _gla_kernel,
        grid=(n,),
        in_specs=[blk(256, fwd), blk(hv, fwd), blk(LANES, fwd),
                  blk(256, bwd), blk(hv, bwd), blk(LANES, bwd),
                  pl.BlockSpec((2, LANES, LANES), lambda i: (0, 0, 0)),
                  pl.BlockSpec((2, 1, LANES), lambda i: (0, 0, 0))],
        out_specs=[blk(hv, fwd), blk(hv, bwd)],
        out_shape=[jax.ShapeDtypeStruct((bsz, s, hv), F32)] * 2,
        scratch_shapes=[pltpu.VMEM((2, bsz, hv, GLA_HEADS * GLA_DK), F32)],
        compiler_params=_cparams("arbitrary"),
        name="gla_scan",
    )(gla_qk, gla_v, gla_lr, gla_qk, gla_v, gla_lr, a2, ab)


def _softmax_attend(q, k, v_ones, k_transposed=False):
    s = jnp.dot(q, k, preferred_element_type=F32) if k_transposed else _nt_dot(q, k)
    p = jnp.exp((s - jnp.max(s, axis=-1, keepdims=True)).astype(BF16))
    o = jnp.dot(p, v_ones, preferred_element_type=F32)
    hd = v_ones.shape[1] // 2
    return o[:, 0:hd] * (1.0 / o[:, hd:hd + 1])


def _attn_kernel(gq_ref, gk_ref, gv_ref, dq_ref, dkt_ref, dv_ref, lam_ref, dnw_ref, go_ref, do_ref,
                 *, n_ctx, out_scale, first_block):
    lam = (jnp.exp(jnp.sum(lam_ref[0:1, :] * lam_ref[1:2, :], axis=-1, keepdims=True))
           - jnp.exp(jnp.sum(lam_ref[2:3, :] * lam_ref[3:4, :], axis=-1, keepdims=True))
           + (1.0 - out_scale))
    tq = gq_ref.shape[0]

    def attend(nk):
        for kh in range(GQA_KV_HEADS):
            heads = [slice((kh * GQA_GROUP + g) * GQA_HD, (kh * GQA_GROUP + g + 1) * GQA_HD)
                     for g in range(GQA_GROUP)]
            q = jnp.concatenate([gq_ref[:, hs] for hs in heads], axis=0)
            o = _softmax_attend(q, gk_ref[0:nk, kh * GQA_HD:(kh + 1) * GQA_HD],
                                gv_ref[0:nk, 2 * kh * GQA_HD:2 * (kh + 1) * GQA_HD])
            for g, hs in enumerate(heads):
                go_ref[:, hs] = o[g * tq:(g + 1) * tq].astype(BF16)
        for h in range(DIFF_HEADS):
            base = h * 2 * DIFF_QK
            v = dv_ref[0:nk, 2 * h * DIFF_V:2 * (h + 1) * DIFF_V]
            o = (_softmax_attend(dq_ref[:, base:base + DIFF_QK], dkt_ref[base:base + DIFF_QK, 0:nk], v, True)
                 - lam * _softmax_attend(dq_ref[:, base + DIFF_QK:base + 2 * DIFF_QK],
                                         dkt_ref[base + DIFF_QK:base + 2 * DIFF_QK, 0:nk], v, True))
            do_ref[:, h * DIFF_V:(h + 1) * DIFF_V] = (_rms(o, dnw_ref[...]) * out_scale).astype(BF16)

    if first_block == 0:
        @pl.when(pl.program_id(1) == 0)
        def _():
            attend(n_ctx)

        @pl.when(pl.program_id(1) > 0)
        def _():
            attend(gk_ref.shape[0])
    else:
        attend(gk_ref.shape[0])


def _attention(gq, gk, gv, dq, dk, dv, lam_rows, diff_norm_w, n_ctx, lambda_init, ctx_out):
    bsz, s, _ = gq.shape
    tq = TOKEN_BLOCK
    assert n_ctx == tq
    first = 0 if ctx_out else 1
    s_out = s - first * tq
    kw = GQA_KV_HEADS * GQA_HD
    qrow = lambda w: pl.BlockSpec((None, tq, w), lambda b, j: (b, j + first, 0))
    orow = lambda w: pl.BlockSpec((None, tq, w), lambda b, j: (b, j, 0))
    full = lambda w: pl.BlockSpec((None, s, w), lambda b, j: (b, 0, 0))
    return pl.pallas_call(
        functools.partial(_attn_kernel, n_ctx=n_ctx, out_scale=1.0 - lambda_init, first_block=first),
        grid=(bsz, s_out // tq),
        in_specs=[qrow(512), full(kw), full(2 * kw), qrow(256),
                  pl.BlockSpec((None, dk.shape[2], s), lambda b, j: (b, 0, 0)), full(512),
                  pl.BlockSpec((4, DIFF_QK), lambda b, j: (0, 0)),
                  pl.BlockSpec((1, DIFF_V), lambda b, j: (0, 0))],
        out_specs=[orow(512), orow(256)],
        out_shape=[jax.ShapeDtypeStruct((bsz, s_out, 512), BF16),
                   jax.ShapeDtypeStruct((bsz, s_out, 256), BF16)],
        compiler_params=_cparams("parallel", "parallel"),
        name="attention",
    )(gq, gk, gv, dq, jnp.swapaxes(dk, 1, 2), dv, lam_rows, diff_norm_w)


def _route(h, wr_ref):
    logits = [jnp.sum(h * wr_ref[e:e + 1, :], axis=-1, keepdims=True) for e in range(N_EXPERTS)]

    def top(ls):
        m = functools.reduce(jnp.maximum, ls)
        idx = jnp.full_like(m, float(N_EXPERTS))
        for e in reversed(range(N_EXPERTS)):
            idx = jnp.where(ls[e] == m, float(e), idx)
        return m, idx

    m1, i1 = top(logits)
    m2, i2 = top([jnp.where(i1 == float(e), -jnp.inf, logits[e]) for e in range(N_EXPERTS)])
    e2 = jnp.exp(m2 - m1)
    w1 = 1.0 / (1.0 + e2)
    w2 = e2 * w1
    lane = lax.broadcasted_iota(jnp.int32, (h.shape[0], LANES), 1)
    return jnp.where(lane == 0, w1, jnp.where(lane == 1, w2, jnp.where(lane == 2, i1,
                     jnp.where(lane == 3, i2, 0.0))))


def _out_kernel(*refs, with_router, split_ctx):
    x, refs = _token_rows(refs, split_ctx)
    if with_router:
        (mod_ref, of_ref, ob_ref, r_ref, go_ref, do_ref, w_ref, g_ref, gnw_ref, nw_ref, wr_ref,
         x1_ref, h2_ref, route_ref) = refs
    else:
        (mod_ref, of_ref, ob_ref, r_ref, go_ref, do_ref, w_ref, g_ref, gnw_ref, nw_ref,
         x1_ref, h2_ref) = refs
    mod = mod_ref[...]
    hv = GLA_HEADS * GLA_DV
    gw = GQA_HEADS * GQA_HD
    o = of_ref[...] + ob_ref[...]
    ms = jnp.dot((o * o).astype(BF16), g_ref[...], preferred_element_type=F32)
    gla = o * lax.rsqrt(ms + NORM_EPS) * gnw_ref[...] * _silu(r_ref[...])
    y = (jnp.dot(gla.astype(BF16), w_ref[0:hv, :], preferred_element_type=F32)
         + jnp.dot(go_ref[...], w_ref[hv:hv + gw, :], preferred_element_type=F32)
         + jnp.dot(do_ref[...], w_ref[hv + gw:, :], preferred_element_type=F32))
    x1 = x + mod[2:3] * y
    x1_ref[...] = x1
    h2 = _rms(x1, nw_ref[...]) * (1.0 + mod[4:5]) + mod[3:4]
    h2_ref[...] = h2.astype(h2_ref.dtype)
    if with_router:
        route_ref[...] = _route(h2, wr_ref)


def _out_projection(x, ctx, modsel, o_f, o_b, gla_r, gqa_o, diff_o, w_out, gmat, gla_norm_w, norm_ffn_w,
                    router_t, ctx_out):
    bsz, s, d = x.shape
    tm = TOKEN_BLOCK
    if ctx is not None:
        s += ctx.shape[1]
    first = 0 if ctx_out else 1
    s_out = s - first * tm
    x_specs, x_args = _token_specs(x, ctx, tm, first)
    arow = lambda w: pl.BlockSpec((None, tm, w), lambda b, j: (b, j + first, 0))
    orow = lambda w: pl.BlockSpec((None, tm, w), lambda b, j: (b, j, 0))
    const = lambda shape: pl.BlockSpec(shape, lambda b, j: (0,) * len(shape))
    with_router = router_t is not None
    in_specs = x_specs + [
                pl.BlockSpec((None, None, 8, d), lambda b, j: (b, jnp.minimum(j + first, 1), 0, 0)),
                arow(256), arow(256), arow(256), orow(512), orow(256),
                const((d, d)), const((256, 256)), const((1, 256)), const((1, d))]
    args = x_args + [modsel, o_f, o_b, gla_r, gqa_o, diff_o, w_out, gmat, gla_norm_w, norm_ffn_w]
    out_specs = [orow(d), orow(d)]
    out_shape = [jax.ShapeDtypeStruct((bsz, s_out, d), F32),
                 jax.ShapeDtypeStruct((bsz, s_out, d), F32 if with_router else BF16)]
    if with_router:
        in_specs.append(const((N_EXPERTS, d)))
        args.append(router_t)
        out_specs.append(orow(LANES))
        out_shape.append(jax.ShapeDtypeStruct((bsz, s_out, LANES), F32))
    return pl.pallas_call(
        functools.partial(_out_kernel, with_router=with_router, split_ctx=ctx is not None),
        grid=(bsz, s_out // tm),
        in_specs=in_specs, out_specs=out_specs, out_shape=out_shape,
        compiler_params=_cparams("parallel", "parallel"),
        name="out_projection",
    )(*args)


def _ffn_kernel(h_ref, x_ref, mod_ref, wg_ref, wu_ref, wd_ref, o_ref, acc_ref, *, n_ctx):
    f = pl.program_id(2)

    @pl.when(f == 0)
    def _():
        acc_ref[...] = jnp.zeros_like(acc_ref)

    h = h_ref[...]
    g = jnp.dot(h, wg_ref[...], preferred_element_type=F32)
    u = jnp.dot(h, wu_ref[...], preferred_element_type=F32)
    acc_ref[...] += jnp.dot((_silu(g) * u).astype(BF16), wd_ref[...], preferred_element_type=F32)

    @pl.when(f == pl.num_programs(2) - 1)
    def _():
        tm = h_ref.shape[0]
        row = pl.program_id(1) * tm + lax.broadcasted_iota(jnp.int32, (tm, 1), 0)
        gate = jnp.where(row < n_ctx, mod_ref[0, 5:6, :], mod_ref[1, 5:6, :])
        o_ref[...] = x_ref[...] + gate * acc_ref[...]


def _dense_ffn(h2, x1, modsel, wg, wu, wd, n_ctx):
    bsz, s, d = x1.shape
    ff = wg.shape[1]
    tm = s // 8
    tf = ff // 2
    row = lambda: pl.BlockSpec((None, tm, d), lambda b, t, f: (b, t, 0))
    return pl.pallas_call(
        functools.partial(_ffn_kernel, n_ctx=n_ctx),
        grid=(bsz, s // tm, ff // tf),
        in_specs=[row(), row(),
                  pl.BlockSpec((None, 2, 8, d), lambda b, t, f: (b, 0, 0, 0)),
                  pl.BlockSpec((d, tf), lambda b, t, f: (0, f)),
                  pl.BlockSpec((d, tf), lambda b, t, f: (0, f)),
                  pl.BlockSpec((tf, d), lambda b, t, f: (f, 0))],
        out_specs=row(),
        out_shape=jax.ShapeDtypeStruct((bsz, s, d), F32),
        scratch_shapes=[pltpu.VMEM((tm, d), F32)],
        compiler_params=_cparams("parallel", "parallel", "arbitrary"),
        name="dense_ffn",
    )(h2, x1, modsel, wg, wu, wd)


MOE_TILE = 1024
MOE_SUB = 256
MOE_FF_BLOCK = 896


def _moe_kernel(te_ref, nv_ref, vr_ref, xs_ref, wg_ref, wu_ref, wd_ref, o_ref, xb_ref):
    f = pl.program_id(1)
    last_f = pl.num_programs(1) - 1
    valid = vr_ref[pl.program_id(0)]

    @pl.when(valid > 0)
    def _():
        wg = wg_ref[...].astype(BF16)
        wu = wu_ref[...].astype(BF16)
        wd = wd_ref[...].astype(BF16)
        for sb in range(MOE_TILE // MOE_SUB):
            rows = pl.ds(sb * MOE_SUB, MOE_SUB)

            @pl.when(sb * MOE_SUB < valid)
            def _():
                @pl.when(f == 0)
                def _():
                    o_ref[rows, :] = jnp.zeros((MOE_SUB, o_ref.shape[1]), F32)
                    xb_ref[rows, :] = xs_ref[rows, :].astype(BF16)

                h = xb_ref[rows, :]
                g = jnp.dot(h, wg, preferred_element_type=F32)
                u = jnp.dot(h, wu, preferred_element_type=F32)
                o_ref[rows, :] += jnp.dot((_silu(g) * u).astype(BF16), wd, preferred_element_type=F32)

            @pl.when(jnp.logical_and(sb * MOE_SUB >= valid, f == last_f))
            def _():
                o_ref[rows, :] = jnp.zeros((MOE_SUB, o_ref.shape[1]), F32)

    @pl.when(jnp.logical_and(valid == 0, f == last_f))
    def _():
        o_ref[...] = jnp.zeros_like(o_ref)


def _moe_experts(tile_expert, n_valid, valid_rows, xs, wg, wu, wd):
    rows, d = xs.shape
    ff = wg.shape[2]
    tm = MOE_TILE
    tf = MOE_FF_BLOCK
    nf = ff // tf

    def tile(i, nv):
        return jnp.minimum(i, nv[0] - 1)

    def fidx(i, f, nv):
        return jnp.where(i < nv[0], f, nf - 1)

    return pl.pallas_call(
        _moe_kernel,
        grid_spec=pltpu.PrefetchScalarGridSpec(
            num_scalar_prefetch=3,
            grid=(rows // tm, nf),
            in_specs=[pl.BlockSpec((tm, d), lambda i, f, te, nv, vr: (tile(i, nv), 0)),
                      pl.BlockSpec((None, d, tf), lambda i, f, te, nv, vr: (te[tile(i, nv)], 0, fidx(i, f, nv))),
                      pl.BlockSpec((None, d, tf), lambda i, f, te, nv, vr: (te[tile(i, nv)], 0, fidx(i, f, nv))),
                      pl.BlockSpec((None, tf, d), lambda i, f, te, nv, vr: (te[tile(i, nv)], fidx(i, f, nv), 0))],
            out_specs=pl.BlockSpec((tm, d), lambda i, f, te, nv, vr: (i, 0)),
            scratch_shapes=[pltpu.VMEM((tm, d), BF16)]),
        out_shape=jax.ShapeDtypeStruct((rows, d), F32),
        compiler_params=_cparams("arbitrary", "arbitrary"),
        name="moe_experts",
    )(tile_expert, n_valid, valid_rows, xs, wg, wu, wd)


def _dispatch_kernel(pos_ref, h_ref, init_ref, xs_ref, sem):
    del init_ref
    tm = h_ref.shape[0]
    base = pl.program_id(0) * (2 * tm)
    for r in range(tm):
        for k in range(2):
            pltpu.make_async_copy(h_ref.at[pl.ds(r, 1), :],
                                  xs_ref.at[pl.ds(pos_ref[base + 2 * r + k], 1), :], sem).start()
    for k in range(2):
        pltpu.make_async_copy(h_ref, xs_ref.at[pl.ds(0, tm), :], sem).wait()


def _moe_dispatch(pos_flat, h2, rows):
    n, d = h2.shape
    tm = TOKEN_BLOCK
    return pl.pallas_call(
        _dispatch_kernel,
        grid_spec=pltpu.PrefetchScalarGridSpec(
            num_scalar_prefetch=1,
            grid=(n // tm,),
            in_specs=[pl.BlockSpec((tm, d), lambda i, pos: (i, 0)),
                      pl.BlockSpec(memory_space=pl.ANY)],
            out_specs=pl.BlockSpec(memory_space=pl.ANY),
            scratch_shapes=[pltpu.SemaphoreType.DMA(())]),
        out_shape=jax.ShapeDtypeStruct((rows, d), F32),
        input_output_aliases={2: 0},
        compiler_params=_cparams("arbitrary"),
        name="moe_dispatch",
    )(pos_flat, h2, jnp.zeros((rows, d), F32))


def _combine_kernel(x_ref, mod_ref, route_ref, ya_ref, yb_ref, nf_ref, o_ref):
    route = route_ref[...]
    moe = route[:, 0:1] * ya_ref[...] + route[:, 1:2] * yb_ref[...]
    x2 = x_ref[...] + mod_ref[5:6, :] * moe
    o_ref[...] = _rms(x2, nf_ref[...])


def _moe_combine(x1, modsel, route, ya, yb, norm_f_w):
    bsz, s, d = x1.shape
    tm = TOKEN_BLOCK
    row = lambda w: pl.BlockSpec((None, tm, w), lambda b, j: (b, j, 0))
    return pl.pallas_call(
        _combine_kernel,
        grid=(bsz, s // tm),
        in_specs=[row(d), pl.BlockSpec((None, None, 8, d), lambda b, j: (b, 1, 0, 0)),
                  row(LANES), row(d), row(d), pl.BlockSpec((1, d), lambda b, j: (0, 0))],
        out_specs=row(d),
        out_shape=jax.ShapeDtypeStruct((bsz, s, d), F32),
        compiler_params=_cparams("parallel", "parallel"),
        name="moe_combine",
    )(x1, modsel, route, ya, yb, norm_f_w)


def _final_norm_kernel(x_ref, nf_ref, o_ref):
    o_ref[...] = _rms(x_ref[...], nf_ref[...])


def _final_norm(x, norm_f_w, first_row):
    bsz, s, d = x.shape
    tm = TOKEN_BLOCK
    first = first_row // tm
    return pl.pallas_call(
        _final_norm_kernel,
        grid=(bsz, (s - first_row) // tm),
        in_specs=[pl.BlockSpec((None, tm, d), lambda b, j: (b, j + first, 0)),
                  pl.BlockSpec((1, d), lambda b, j: (0, 0))],
        out_specs=pl.BlockSpec((None, tm, d), lambda b, j: (b, j, 0)),
        out_shape=jax.ShapeDtypeStruct((bsz, s - first_row, d), F32),
        compiler_params=_cparams("parallel", "parallel"),
        name="final_norm",
    )(x, norm_f_w)


def _moe_plan(route, tm):
    n = route.shape[0]
    e_flat = route[:, 2:4].astype(jnp.int32).reshape(-1)
    onehot = (e_flat[:, None] == jnp.arange(N_EXPERTS, dtype=jnp.int32)[None, :]).astype(jnp.int32)
    csum = jnp.cumsum(onehot, axis=0)
    counts = csum[-1]
    padded = ((counts + tm - 1) // tm) * tm
    ends = jnp.cumsum(padded)
    pos = jnp.sum(onehot * (csum - 1 + (ends - padded)[None, :]), axis=1)
    rows = 2 * n + N_EXPERTS * tm
    n_tiles = rows // tm
    starts = jnp.arange(n_tiles, dtype=jnp.int32) * tm
    tile_expert = jnp.minimum(jnp.sum((ends[None, :] <= starts[:, None]).astype(jnp.int32), axis=1),
                              N_EXPERTS - 1)
    n_valid = (ends[-1] // tm).astype(jnp.int32).reshape(1)
    token_end = (ends - padded + counts)[tile_expert]
    valid_rows = jnp.clip(token_end - starts, 0, tm).astype(jnp.int32)
    return rows, pos.astype(jnp.int32), tile_expert, n_valid, valid_rows


def _rope_tables(n_ctx, seq, head_dim, copies):
    half = head_dim // 4
    t = np.arange(seq)
    freqs = ROPE_THETA ** (-jnp.arange(half, dtype=F32) / half)
    rows = jnp.asarray(t // GRID_W, F32)[:, None] * freqs[None, :]
    cols = jnp.asarray(t % GRID_W, F32)[:, None] * freqs[None, :]
    cos = jnp.concatenate([jnp.cos(rows)] * 2 + [jnp.cos(cols)] * 2, axis=1)
    sin = jnp.concatenate([-jnp.sin(rows), jnp.sin(rows), -jnp.sin(cols), jnp.sin(cols)], axis=1)
    cos = jnp.concatenate([jnp.ones((n_ctx, head_dim), F32), cos], axis=0)
    sin = jnp.concatenate([jnp.zeros((n_ctx, head_dim), F32), sin], axis=0)
    return jnp.tile(cos, (1, copies)), jnp.tile(sin, (1, copies))


def _group_mean_matrix(width, group):
    g = np.kron(np.eye(width // group, dtype=np.float32), np.ones((group, group), np.float32)) / group
    return jnp.asarray(g, BF16)


def kernel(x, c, ctx, c_ctx, w_mod, b_mod, norm_mix, norm_ffn, w_in, w_out, gla_a2_f, gla_ab_f, gla_a2_b,
           gla_ab_b, gla_norm, gqa_q_norm, gqa_k_norm, diff_lam_q1, diff_lam_k1, diff_lam_q2, diff_lam_k2,
           diff_norm, ffn_gate, ffn_up, ffn_down, moe_router, moe_gate, moe_up, moe_down, norm_f):
    bsz, seq, d = x.shape
    n_ctx = ctx.shape[1]
    depth = w_mod.shape[0]

    c_rows = jnp.concatenate([c, c_ctx[None, :], jnp.zeros((8 - bsz - 1, d), F32)], axis=0)
    mod = _modulation(c_rows, w_mod, b_mod).reshape(depth, 8, 6, d)
    mod = jnp.pad(mod, ((0, 0), (0, 0), (0, 2), (0, 0)))
    modsel = jnp.stack([jnp.broadcast_to(mod[:, bsz][:, None], (depth, bsz, 8, d)), mod[:, :bsz]], axis=2)

    tables = (_rope_tables(n_ctx, seq, GQA_HD, LANES // GQA_HD) + _rope_tables(n_ctx, seq, DIFF_QK, LANES // DIFF_QK))
    gmat = _group_mean_matrix(GQA_HEADS * GQA_HD, GQA_HD)
    gmat_v = _group_mean_matrix(GLA_HEADS * GLA_DV, GLA_DV)

    w_in_p = jnp.concatenate([w_in[:, :, :_W_IN_LR_END],
                              jnp.zeros((depth, d, _W_GQA_Q - _W_IN_LR_END), F32),
                              w_in[:, :, _W_IN_LR_END:]], axis=2).astype(BF16)
    w_out_b = w_out.astype(BF16)
    a2 = jnp.zeros((depth, 2, LANES, LANES), F32)
    a2 = a2.at[:, 0, 0:GLA_GATE_RANK].set(gla_a2_f).at[:, 1, GLA_GATE_RANK:2 * GLA_GATE_RANK].set(gla_a2_b)
    a2 = a2.astype(BF16)
    ab = jnp.stack([gla_ab_f, gla_ab_b], axis=1)[:, :, None, :]

    x_all, x_ctx = x, ctx
    out = None
    for l in range(depth):
        ctx_out = l < depth - 1
        lambda_init = 0.8 - 0.6 * math.exp(-0.3 * l)
        gla_qk, gla_v, gla_r, gla_lr, gq, gk, gv, dq, dk, dv = _in_projection(
            x_all, x_ctx, modsel[l], norm_mix[l][None, :], w_in_p[l], gmat,
            jnp.tile(gqa_q_norm[l], GQA_HEADS)[None, :], jnp.tile(gqa_k_norm[l], GQA_KV_HEADS)[None, :], tables)
        o_f, o_b = _gla(gla_qk, gla_v, gla_lr, a2[l], ab[l], n_ctx // GLA_CHUNK)
        lam_rows = jnp.stack([diff_lam_q1[l], diff_lam_k1[l], diff_lam_q2[l], diff_lam_k2[l]], axis=0)
        gqa_o, diff_o = _attention(gq, gk, gv, dq, dk, dv, lam_rows, diff_norm[l][None, :], n_ctx, lambda_init,
                                   ctx_out)
        moe_layer = l % 2 == 1
        i = l // 2
        res = _out_projection(x_all, x_ctx, modsel[l], o_f, o_b, gla_r, gqa_o, diff_o, w_out_b[l], gmat_v,
                              jnp.tile(gla_norm[l], GLA_HEADS)[None, :], norm_ffn[l][None, :],
                              moe_router[i].T if moe_layer else None, ctx_out)
        if not moe_layer:
            x1, h2 = res
            if not ctx_out:
                raise NotImplementedError("dense feed-forward on the last layer")
            x_all = _dense_ffn(h2, x1, modsel[l], ffn_gate[i].astype(BF16), ffn_up[i].astype(BF16),
                               ffn_down[i].astype(BF16), n_ctx)
            x_ctx = None
            if l == depth - 1:
                out = _final_norm(x_all, norm_f[None, :], n_ctx)
        else:
            x1, h2, route = res
            if ctx_out or l != depth - 1:
                raise NotImplementedError("expert feed-forward before the last layer")
            n = bsz * seq
            rows, pos, tile_expert, n_valid, valid_rows = _moe_plan(route.reshape(n, LANES), MOE_TILE)
            xs = _moe_dispatch(pos, h2.reshape(n, d), rows)
            ys = _moe_experts(tile_expert, n_valid, valid_rows, xs, moe_gate[i], moe_up[i], moe_down[i])
            ya = ys.at[pos[0::2]].get(mode="promise_in_bounds").reshape(bsz, seq, d)
            yb = ys.at[pos[1::2]].get(mode="promise_in_bounds").reshape(bsz, seq, d)
            out = _moe_combine(x1, modsel[l], route, ya, yb, norm_f[None, :])
    return out
```

```python
import functools
import math

import jax
import jax.numpy as jnp
import numpy as np
from jax import lax
from jax.experimental import pallas as pl
from jax.experimental.pallas import tpu as pltpu

F32 = jnp.float32
BF16 = jnp.bfloat16

GRID_W = 64
GLA_HEADS, GLA_DK, GLA_DV = 4, 32, 64
GLA_GATE_RANK = 16
GLA_GATE_NORM = 16.0
GLA_CHUNK = 64
GLA_STEP_CHUNKS = 4
GQA_HEADS, GQA_KV_HEADS, GQA_HD = 8, 2, 64
GQA_GROUP = GQA_HEADS // GQA_KV_HEADS
DIFF_HEADS, DIFF_QK, DIFF_V = 4, 32, 64
ROPE_THETA = 10000.0
NORM_EPS = 1e-6
N_EXPERTS = 8

LANES = 128
TOKEN_BLOCK = 256
VMEM_LIMIT = 56 * 1024 * 1024

_W_GLA_QK = 0
_W_GLA_V = 256
_W_GLA_R = 512
_W_GLA_LR = 768
_W_GQA_Q = 896
_W_GQA_K = 1408
_W_GQA_V = 1536
_W_DIFF_Q = 1664
_W_DIFF_K = 1920
_W_DIFF_V = 2176
_W_IN_PAD = 2432
_W_IN_LR_END = 800


def _cparams(*sem):
    return pltpu.CompilerParams(dimension_semantics=sem, vmem_limit_bytes=VMEM_LIMIT)


def _rms(x, w):
    ms = jnp.mean(x * x, axis=-1, keepdims=True)
    return x * lax.rsqrt(ms + NORM_EPS) * w


def _silu(x):
    return x * (1.0 / (1.0 + jnp.exp(-x)))


def _nt_dot(a, b):
    return lax.dot_general(a, b, (((1,), (1,)), ((), ())), preferred_element_type=F32)


def _tn_dot(a, b):
    return lax.dot_general(a, b, (((0,), (0,)), ((), ())), preferred_element_type=F32)


def _mod_kernel(c_ref, w_ref, b_ref, o_ref):
    s = _silu(c_ref[...])
    o_ref[...] = jnp.dot(s.astype(BF16), w_ref[...].astype(BF16),
                         preferred_element_type=F32) + b_ref[...]


def _modulation(c_rows, w_mod, b_mod):
    depth, d, n = w_mod.shape
    tn = 1536
    return pl.pallas_call(
        _mod_kernel,
        grid=(depth, n // tn),
        in_specs=[pl.BlockSpec((8, d), lambda l, j: (0, 0)),
                  pl.BlockSpec((None, d, tn), lambda l, j: (l, 0, j)),
                  pl.BlockSpec((None, 1, tn), lambda l, j: (l, 0, j))],
        out_specs=pl.BlockSpec((None, 8, tn), lambda l, j: (l, 0, j)),
        out_shape=jax.ShapeDtypeStruct((depth, 8, n), F32),
        compiler_params=_cparams("parallel", "parallel"),
        name="modulation",
    )(c_rows, w_mod, b_mod.reshape(depth, 1, n))


def _rope(x, cos, sin_signed, half):
    width = x.shape[1]
    lane = lax.broadcasted_iota(jnp.int32, x.shape, 1)
    first = (lane % (2 * half)) < half
    rot = jnp.where(first, pltpu.roll(x, width - half, 1), pltpu.roll(x, half, 1))
    return x * cos + rot * sin_signed


def _token_rows(refs, split_ctx):
    if not split_ctx:
        return refs[0][...], refs[1:]
    return jnp.where(pl.program_id(1) == 0, refs[0][...], refs[1][...]), refs[2:]


def _token_specs(x, ctx, tm, first):
    d = x.shape[-1]
    if ctx is None:
        return [pl.BlockSpec((None, tm, d), lambda b, j: (b, j + first, 0))], [x]
    assert ctx.shape[1] == tm and first == 0
    return ([pl.BlockSpec((None, tm, d), lambda b, j: (b, 0, 0)),
             pl.BlockSpec((None, tm, d), lambda b, j: (b, jnp.maximum(j - 1, 0), 0))], [ctx, x])


def _values_and_ones(v, head_dim):
    ones = jnp.ones((v.shape[0], head_dim), v.dtype)
    parts = []
    for h in range(v.shape[1] // head_dim):
        parts += [v[:, h * head_dim:(h + 1) * head_dim], ones]
    return jnp.concatenate(parts, axis=1)


def _in_kernel(*refs, split_ctx):
    x, refs = _token_rows(refs, split_ctx)
    (mod_ref, nw_ref, w_ref, g_ref, qg_ref, kg_ref, cg_ref, sg_ref, cd_ref, sd_ref,
     glaqk_ref, glav_ref, glar_ref, glalr_ref, gq_ref, gk_ref, gv_ref, dq_ref, dk_ref, dv_ref) = refs
    mod = mod_ref[...]
    h = _rms(x, nw_ref[...]) * (1.0 + mod[1:2]) + mod[0:1]
    hb = h.astype(BF16)

    def proj(a, b):
        return jnp.dot(hb, w_ref[:, a:b], preferred_element_type=F32)

    glaqk_ref[...] = proj(_W_GLA_QK, _W_GLA_V)
    glav_ref[...] = proj(_W_GLA_V, _W_GLA_R).astype(BF16)
    glar_ref[...] = proj(_W_GLA_R, _W_GLA_LR)
    glalr_ref[...] = proj(_W_GLA_LR, _W_GQA_Q)

    cos_k = cg_ref[...]
    sin_k = sg_ref[...]
    reps = GQA_HEADS * GQA_HD // LANES
    q = proj(_W_GQA_Q, _W_GQA_K)
    ms = jnp.dot((q * q).astype(BF16), g_ref[...], preferred_element_type=F32)
    qn = q * lax.rsqrt(ms + NORM_EPS) * qg_ref[...]
    gq_ref[...] = (_rope(qn, jnp.concatenate([cos_k] * reps, axis=1), jnp.concatenate([sin_k] * reps, axis=1),
                         GQA_HD // 4) * GQA_HD ** -0.5).astype(BF16)

    kw = GQA_KV_HEADS * GQA_HD
    k = proj(_W_GQA_K, _W_GQA_V)
    ms = jnp.dot((k * k).astype(BF16), g_ref[0:kw, 0:kw], preferred_element_type=F32)
    kn = k * lax.rsqrt(ms + NORM_EPS) * kg_ref[...]
    gk_ref[...] = _rope(kn, cos_k, sin_k, GQA_HD // 4).astype(BF16)
    gv_ref[...] = _values_and_ones(proj(_W_GQA_V, _W_DIFF_Q), GQA_HD).astype(BF16)

    reps = 2 * DIFF_HEADS * DIFF_QK // LANES
    cos_d = jnp.concatenate([cd_ref[...]] * reps, axis=1)
    sin_d = jnp.concatenate([sd_ref[...]] * reps, axis=1)
    dq = proj(_W_DIFF_Q, _W_DIFF_K)
    dq_ref[...] = (_rope(dq, cos_d, sin_d, DIFF_QK // 4) * DIFF_QK ** -0.5).astype(BF16)
    dk = proj(_W_DIFF_K, _W_DIFF_V)
    dk_ref[...] = _rope(dk, cos_d, sin_d, DIFF_QK // 4).astype(BF16)
    dv_ref[...] = _values_and_ones(proj(_W_DIFF_V, _W_IN_PAD), DIFF_V).astype(BF16)


def _in_projection(x, ctx, modsel, norm_w, w_in_p, gmat, q_gain, k_gain, tables):
    bsz, s, d = x.shape
    tm = TOKEN_BLOCK
    if ctx is not None:
        s += ctx.shape[1]
    x_specs, x_args = _token_specs(x, ctx, tm, 0)
    cos_g, sin_g, cos_d, sin_d = tables
    row = lambda w: pl.BlockSpec((None, tm, w), lambda b, j: (b, j, 0))
    const = lambda shape: pl.BlockSpec(shape, lambda b, j: (0,) * len(shape))
    tab = lambda w: pl.BlockSpec((tm, w), lambda b, j: (j, 0))
    widths = [(256, F32), (256, BF16), (256, F32), (128, F32), (512, BF16), (128, BF16), (256, BF16),
              (256, BF16), (256, BF16), (512, BF16)]
    return pl.pallas_call(
        functools.partial(_in_kernel, split_ctx=ctx is not None),
        grid=(bsz, s // tm),
        in_specs=x_specs + [
                  pl.BlockSpec((None, None, 8, d), lambda b, j: (b, jnp.minimum(j, 1), 0, 0)),
                  const((1, d)), const((d, _W_IN_PAD)), const((512, 512)),
                  const((1, 512)), const((1, 128)),
                  tab(LANES), tab(LANES), tab(LANES), tab(LANES)],
        out_specs=[row(w) for w, _ in widths],
        out_shape=[jax.ShapeDtypeStruct((bsz, s, w), dt) for w, dt in widths],
        compiler_params=_cparams("parallel", "parallel"),
        name="in_projection",
    )(*x_args, modsel, norm_w, w_in_p, gmat, q_gain, k_gain, cos_g, sin_g, cos_d, sin_d)


def _split3(g):
    g1 = g.astype(BF16)
    r1 = g - g1.astype(F32)
    g2 = r1.astype(BF16)
    g3 = (r1 - g2.astype(F32)).astype(BF16)
    return jnp.concatenate([g1, g2, g3], axis=1)


def _gla_kernel(qkf_ref, vf_ref, lrf_ref, qkb_ref, vb_ref, lrb_ref, a2_ref, ab_ref,
                of_ref, ob_ref, st_ref):
    @pl.when(pl.program_id(0) == 0)
    def _():
        st_ref[...] = jnp.zeros_like(st_ref)

    n_b = qkf_ref.shape[0]
    c = GLA_CHUNK
    hk = GLA_HEADS * GLA_DK
    hv = GLA_HEADS * GLA_DV
    r_i = lax.broadcasted_iota(jnp.int32, (c, c), 0)
    c_i = lax.broadcasted_iota(jnp.int32, (c, c), 1)
    head_of_k = lax.broadcasted_iota(jnp.int32, (c, hk), 1) // GLA_DK
    head_of_v = lax.broadcasted_iota(jnp.int32, (c, hv), 1) // GLA_DV
    dirs = ((qkf_ref, vf_ref, lrf_ref, of_ref, c_i <= r_i, c - 1),
            (qkb_ref, vb_ref, lrb_ref, ob_ref, c_i >= r_i, 0))
    n_sub = qkf_ref.shape[1] // c
    for d, (qk_ref, v_ref, lr_ref, o_ref, tri, last) in enumerate(dirs):
        tri_b = jnp.where(tri, 1.0, 0.0).astype(BF16)
        tri_heads = jnp.concatenate([tri] * GLA_HEADS, axis=0)
        lr = jnp.concatenate([lr_ref[b] for b in range(n_b)], axis=0)
        pre = jnp.dot(lr.astype(BF16), a2_ref[d], preferred_element_type=F32) + ab_ref[d]
        log_a = (jnp.minimum(pre, 0.0) - jnp.log(1.0 + jnp.exp(-jnp.abs(pre)))) * (1.0 / GLA_GATE_NORM)
        parts = _split3(log_a)
        cs = jnp.dot(tri_b, jnp.concatenate([parts[u * c:(u + 1) * c] for u in range(n_b * n_sub)], axis=1),
                     preferred_element_type=F32)
        for b in range(n_b):
            st = st_ref[d, b]
            for sub in (range(n_sub) if d == 0 else reversed(range(n_sub))):
                rows = slice(sub * c, (sub + 1) * c)
                base = (b * n_sub + sub) * 3 * hk
                bc = cs[:, base:base + hk] + cs[:, base + hk:base + 2 * hk] + cs[:, base + 2 * hk:base + 3 * hk]
                tot = bc[last:last + 1, :]
                qk = qk_ref[b, rows, :]
                q = qk[:, 0:hk] * GLA_DK ** -0.5
                k = qk[:, hk:2 * hk]
                qd = q * jnp.exp(bc)
                kd = (k * jnp.exp(-bc)).astype(BF16)
                kl = (k * jnp.exp(tot - bc)).astype(BF16)
                v = v_ref[b, rows, :]
                q_heads = jnp.concatenate([jnp.where(head_of_k == h, qd, 0.0) for h in range(GLA_HEADS)],
                                          axis=0).astype(BF16)
                a = jnp.where(tri_heads, _nt_dot(q_heads, kd), 0.0).astype(BF16)
                o_all = jnp.dot(a, v, preferred_element_type=F32) + _nt_dot(q_heads, st.astype(BF16))
                o = jnp.where(head_of_v == 0, o_all[0:c], 0.0)
                for h in range(1, GLA_HEADS):
                    o = o + jnp.where(head_of_v == h, o_all[h * c:(h + 1) * c], 0.0)
                o_ref[b, rows, :] = o
                st = st * jnp.exp(tot) + _tn_dot(v, kl)
            st_ref[d, b] = st


def _gla(gla_qk, gla_v, gla_lr, a2, ab, n_ctx_chunks):
    bsz, s, _ = gla_qk.shape
    c = GLA_STEP_CHUNKS * GLA_CHUNK
    n = s // c
    n_ctx_steps = n_ctx_chunks // GLA_STEP_CHUNKS
    assert n * c == s and n_ctx_steps * GLA_STEP_CHUNKS == n_ctx_chunks
    fwd = lambda i: (0, i, 0)
    bwd = lambda i: (0, jnp.where(i < n_ctx_steps, n_ctx_steps - 1 - i, n + n_ctx_steps - 1 - i), 0)
    blk = lambda w, m: pl.BlockSpec((bsz, c, w), m)
    hv = GLA_HEADS * GLA_DV
    return pl.pallas_call(
        _gla_kernel,
        grid=(n,),
        in_specs=[blk(256, fwd), blk(hv, fwd), blk(LANES, fwd),
                  blk(256, bwd), blk(hv, bwd), blk(LANES, bwd),
                  pl.BlockSpec((2, LANES, LANES), lambda i: (0, 0, 0)),
                  pl.BlockSpec((2, 1, LANES), lambda i: (0, 0, 0))],
        out_specs=[blk(hv, fwd), blk(hv, bwd)],
        out_shape=[jax.ShapeDtypeStruct((bsz, s, hv), F32)] * 2,
        scratch_shapes=[pltpu.VMEM((2, bsz, hv, GLA_HEADS * GLA_DK), F32)],
        compiler_params=_cparams("arbitrary"),
        name="gla_scan",
    )(gla_qk, gla_v, gla_lr, gla_qk, gla_v, gla_lr, a2, ab)


def _softmax_attend(q, k, v_ones, k_transposed=False):
    s = jnp.dot(q, k, preferred_element_type=F32) if k_transposed else _nt_dot(q, k)
    p = jnp.exp((s - jnp.max(s, axis=-1, keepdims=True)).astype(BF16))
    o = jnp.dot(p, v_ones, preferred_element_type=F32)
    hd = v_ones.shape[1] // 2
    return o[:, 0:hd] * (1.0 / o[:, hd:hd + 1])


def _attn_kernel(gq_ref, gk_ref, gv_ref, dq_ref, dkt_ref, dv_ref, lam_ref, dnw_ref, go_ref, do_ref,
                 *, n_ctx, out_scale, first_block):
    lam = (jnp.exp(jnp.sum(lam_ref[0:1, :] * lam_ref[1:2, :], axis=-1, keepdims=True))
           - jnp.exp(jnp.sum(lam_ref[2:3, :] * lam_ref[3:4, :], axis=-1, keepdims=True))
           + (1.0 - out_scale))
    tq = gq_ref.shape[0]

    def attend(nk):
        for kh in range(GQA_KV_HEADS):
            heads = [slice((kh * GQA_GROUP + g) * GQA_HD, (kh * GQA_GROUP + g + 1) * GQA_HD)
                     for g in range(GQA_GROUP)]
            q = jnp.concatenate([gq_ref[:, hs] for hs in heads], axis=0)
            o = _softmax_attend(q, gk_ref[0:nk, kh * GQA_HD:(kh + 1) * GQA_HD],
                                gv_ref[0:nk, 2 * kh * GQA_HD:2 * (kh + 1) * GQA_HD])
            for g, hs in enumerate(heads):
                go_ref[:, hs] = o[g * tq:(g + 1) * tq].astype(BF16)
        for h in range(DIFF_HEADS):
            base = h * 2 * DIFF_QK
            v = dv_ref[0:nk, 2 * h * DIFF_V:2 * (h + 1) * DIFF_V]
            o = (_softmax_attend(dq_ref[:, base:base + DIFF_QK], dkt_ref[base:base + DIFF_QK, 0:nk], v, True)
                 - lam * _softmax_attend(dq_ref[:, base + DIFF_QK:base + 2 * DIFF_QK],
                                         dkt_ref[base + DIFF_QK:base + 2 * DIFF_QK, 0:nk], v, True))
            do_ref[:, h * DIFF_V:(h + 1) * DIFF_V] = (_rms(o, dnw_ref[...]) * out_scale).astype(BF16)

    if first_block == 0:
        @pl.when(pl.program_id(1) == 0)
        def _():
            attend(n_ctx)

        @pl.when(pl.program_id(1) > 0)
        def _():
            attend(gk_ref.shape[0])
    else:
        attend(gk_ref.shape[0])


def _attention(gq, gk, gv, dq, dk, dv, lam_rows, diff_norm_w, n_ctx, lambda_init, ctx_out):
    bsz, s, _ = gq.shape
    tq = TOKEN_BLOCK
    assert n_ctx == tq
    first = 0 if ctx_out else 1
    s_out = s - first * tq
    kw = GQA_KV_HEADS * GQA_HD
    qrow = lambda w: pl.BlockSpec((None, tq, w), lambda b, j: (b, j + first, 0))
    orow = lambda w: pl.BlockSpec((None, tq, w), lambda b, j: (b, j, 0))
    full = lambda w: pl.BlockSpec((None, s, w), lambda b, j: (b, 0, 0))
    return pl.pallas_call(
        functools.partial(_attn_kernel, n_ctx=n_ctx, out_scale=1.0 - lambda_init, first_block=first),
        grid=(bsz, s_out // tq),
        in_specs=[qrow(512), full(kw), full(2 * kw), qrow(256),
                  pl.BlockSpec((None, dk.shape[2], s), lambda b, j: (b, 0, 0)), full(512),
                  pl.BlockSpec((4, DIFF_QK), lambda b, j: (0, 0)),
                  pl.BlockSpec((1, DIFF_V), lambda b, j: (0, 0))],
        out_specs=[orow(512), orow(256)],
        out_shape=[jax.ShapeDtypeStruct((bsz, s_out, 512), BF16),
                   jax.ShapeDtypeStruct((bsz, s_out, 256), BF16)],
        compiler_params=_cparams("parallel", "parallel"),
        name="attention",
    )(gq, gk, gv, dq, jnp.swapaxes(dk, 1, 2), dv, lam_rows, diff_norm_w)


def _route(h, wr_ref):
    logits = [jnp.sum(h * wr_ref[e:e + 1, :], axis=-1, keepdims=True) for e in range(N_EXPERTS)]

    def top(ls):
        m = functools.reduce(jnp.maximum, ls)
        idx = jnp.full_like(m, float(N_EXPERTS))
        for e in reversed(range(N_EXPERTS)):
            idx = jnp.where(ls[e] == m, float(e), idx)
        return m, idx

    m1, i1 = top(logits)
    m2, i2 = top([jnp.where(i1 == float(e), -jnp.inf, logits[e]) for e in range(N_EXPERTS)])
    e2 = jnp.exp(m2 - m1)
    w1 = 1.0 / (1.0 + e2)
    w2 = e2 * w1
    lane = lax.broadcasted_iota(jnp.int32, (h.shape[0], LANES), 1)
    return jnp.where(lane == 0, w1, jnp.where(lane == 1, w2, jnp.where(lane == 2, i1,
                     jnp.where(lane == 3, i2, 0.0))))


def _out_kernel(*refs, with_router, split_ctx):
    x, refs = _token_rows(refs, split_ctx)
    if with_router:
        (mod_ref, of_ref, ob_ref, r_ref, go_ref, do_ref, w_ref, g_ref, gnw_ref, nw_ref, wr_ref,
         x1_ref, h2_ref, route_ref) = refs
    else:
        (mod_ref, of_ref, ob_ref, r_ref, go_ref, do_ref, w_ref, g_ref, gnw_ref, nw_ref,
         x1_ref, h2_ref) = refs
    mod = mod_ref[...]
    hv = GLA_HEADS * GLA_DV
    gw = GQA_HEADS * GQA_HD
    o = of_ref[...] + ob_ref[...]
    ms = jnp.dot((o * o).astype(BF16), g_ref[...], preferred_element_type=F32)
    gla = o * lax.rsqrt(ms + NORM_EPS) * gnw_ref[...] * _silu(r_ref[...])
    y = (jnp.dot(gla.astype(BF16), w_ref[0:hv, :], preferred_element_type=F32)
         + jnp.dot(go_ref[...], w_ref[hv:hv + gw, :], preferred_element_type=F32)
         + jnp.dot(do_ref[...], w_ref[hv + gw:, :], preferred_element_type=F32))
    x1 = x + mod[2:3] * y
    x1_ref[...] = x1
    h2 = _rms(x1, nw_ref[...]) * (1.0 + mod[4:5]) + mod[3:4]
    h2_ref[...] = h2.astype(h2_ref.dtype)
    if with_router:
        route_ref[...] = _route(h2, wr_ref)


def _out_projection(x, ctx, modsel, o_f, o_b, gla_r, gqa_o, diff_o, w_out, gmat, gla_norm_w, norm_ffn_w,
                    router_t, ctx_out):
    bsz, s, d = x.shape
    tm = TOKEN_BLOCK
    if ctx is not None:
        s += ctx.shape[1]
    first = 0 if ctx_out else 1
    s_out = s - first * tm
    x_specs, x_args = _token_specs(x, ctx, tm, first)
    arow = lambda w: pl.BlockSpec((None, tm, w), lambda b, j: (b, j + first, 0))
    orow = lambda w: pl.BlockSpec((None, tm, w), lambda b, j: (b, j, 0))
    const = lambda shape: pl.BlockSpec(shape, lambda b, j: (0,) * len(shape))
    with_router = router_t is not None
    in_specs = x_specs + [
                pl.BlockSpec((None, None, 8, d), lambda b, j: (b, jnp.minimum(j + first, 1), 0, 0)),
                arow(256), arow(256), arow(256), orow(512), orow(256),
                const((d, d)), const((256, 256)), const((1, 256)), const((1, d))]
    args = x_args + [modsel, o_f, o_b, gla_r, gqa_o, diff_o, w_out, gmat, gla_norm_w, norm_ffn_w]
    out_specs = [orow(d), orow(d)]
    out_shape = [jax.ShapeDtypeStruct((bsz, s_out, d), F32),
                 jax.ShapeDtypeStruct((bsz, s_out, d), F32 if with_router else BF16)]
    if with_router:
        in_specs.append(const((N_EXPERTS, d)))
        args.append(router_t)
        out_specs.append(orow(LANES))
        out_shape.append(jax.ShapeDtypeStruct((bsz, s_out, LANES), F32))
    return pl.pallas_call(
        functools.partial(_out_kernel, with_router=with_router, split_ctx=ctx is not None),
        grid=(bsz, s_out // tm),
        in_specs=in_specs, out_specs=out_specs, out_shape=out_shape,
        compiler_params=_cparams("parallel", "parallel"),
        name="out_projection",
    )(*args)


def _ffn_kernel(h_ref, x_ref, mod_ref, wg_ref, wu_ref, wd_ref, o_ref, acc_ref, *, n_ctx):
    f = pl.program_id(2)

    @pl.when(f == 0)
    def _():
        acc_ref[...] = jnp.zeros_like(acc_ref)

    h = h_ref[...]
    g = jnp.dot(h, wg_ref[...], preferred_element_type=F32)
    u = jnp.dot(h, wu_ref[...], preferred_element_type=F32)
    acc_ref[...] += jnp.dot((_silu(g) * u).astype(BF16), wd_ref[...], preferred_element_type=F32)

    @pl.when(f == pl.num_programs(2) - 1)
    def _():
        tm = h_ref.shape[0]
        row = pl.program_id(1) * tm + lax.broadcasted_iota(jnp.int32, (tm, 1), 0)
        gate = jnp.where(row < n_ctx, mod_ref[0, 5:6, :], mod_ref[1, 5:6, :])
        o_ref[...] = x_ref[...] + gate * acc_ref[...]


def _dense_ffn(h2, x1, modsel, wg, wu, wd, n_ctx):
    bsz, s, d = x1.shape
    ff = wg.shape[1]
    tm = s // 8
    tf = ff // 2
    row = lambda: pl.BlockSpec((None, tm, d), lambda b, t, f: (b, t, 0))
    return pl.pallas_call(
        functools.partial(_ffn_kernel, n_ctx=n_ctx),
        grid=(bsz, s // tm, ff // tf),
        in_specs=[row(), row(),
                  pl.BlockSpec((None, 2, 8, d), lambda b, t, f: (b, 0, 0, 0)),
                  pl.BlockSpec((d, tf), lambda b, t, f: (0, f)),
                  pl.BlockSpec((d, tf), lambda b, t, f: (0, f)),
                  pl.BlockSpec((tf, d), lambda b, t, f: (f, 0))],
        out_specs=row(),
        out_shape=jax.ShapeDtypeStruct((bsz, s, d), F32),
        scratch_shapes=[pltpu.VMEM((tm, d), F32)],
        compiler_params=_cparams("parallel", "parallel", "arbitrary"),
        name="dense_ffn",
    )(h2, x1, modsel, wg, wu, wd)


MOE_TILE = 1024
MOE_SUB = 256
MOE_FF_BLOCK = 896


def _moe_kernel(te_ref, nv_ref, vr_ref, xs_ref, wg_ref, wu_ref, wd_ref, o_ref, xb_ref):
    f = pl.program_id(1)
    last_f = pl.num_programs(1) - 1
    valid = vr_ref[pl.program_id(0)]

    @pl.when(valid > 0)
    def _():
        wg = wg_ref[...].astype(BF16)
        wu = wu_ref[...].astype(BF16)
        wd = wd_ref[...].astype(BF16)
        for sb in range(MOE_TILE // MOE_SUB):
            rows = pl.ds(sb * MOE_SUB, MOE_SUB)

            @pl.when(sb * MOE_SUB < valid)
            def _():
                @pl.when(f == 0)
                def _():
                    o_ref[rows, :] = jnp.zeros((MOE_SUB, o_ref.shape[1]), F32)
                    xb_ref[rows, :] = xs_ref[rows, :].astype(BF16)

                h = xb_ref[rows, :]
                g = jnp.dot(h, wg, preferred_element_type=F32)
                u = jnp.dot(h, wu, preferred_element_type=F32)
                o_ref[rows, :] += jnp.dot((_silu(g) * u).astype(BF16), wd, preferred_element_type=F32)

            @pl.when(jnp.logical_and(sb * MOE_SUB >= valid, f == last_f))
            def _():
                o_ref[rows, :] = jnp.zeros((MOE_SUB, o_ref.shape[1]), F32)

    @pl.when(jnp.logical_and(valid == 0, f == last_f))
    def _():
        o_ref[...] = jnp.zeros_like(o_ref)


def _moe_experts(tile_expert, n_valid, valid_rows, xs, wg, wu, wd):
    rows, d = xs.shape
    ff = wg.shape[2]
    tm = MOE_TILE
    tf = MOE_FF_BLOCK
    nf = ff // tf

    def tile(i, nv):
        return jnp.minimum(i, nv[0] - 1)

    def fidx(i, f, nv):
        return jnp.where(i < nv[0], f, nf - 1)

    return pl.pallas_call(
        _moe_kernel,
        grid_spec=pltpu.PrefetchScalarGridSpec(
            num_scalar_prefetch=3,
            grid=(rows // tm, nf),
            in_specs=[pl.BlockSpec((tm, d), lambda i, f, te, nv, vr: (tile(i, nv), 0)),
                      pl.BlockSpec((None, d, tf), lambda i, f, te, nv, vr: (te[tile(i, nv)], 0, fidx(i, f, nv))),
                      pl.BlockSpec((None, d, tf), lambda i, f, te, nv, vr: (te[tile(i, nv)], 0, fidx(i, f, nv))),
                      pl.BlockSpec((None, tf, d), lambda i, f, te, nv, vr: (te[tile(i, nv)], fidx(i, f, nv), 0))],
            out_specs=pl.BlockSpec((tm, d), lambda i, f, te, nv, vr: (i, 0)),
            scratch_shapes=[pltpu.VMEM((tm, d), BF16)]),
        out_shape=jax.ShapeDtypeStruct((rows, d), F32),
        compiler_params=_cparams("arbitrary", "arbitrary"),
        name="moe_experts",
    )(tile_expert, n_valid, valid_rows, xs, wg, wu, wd)


def _dispatch_kernel(pos_ref, h_ref, init_ref, xs_ref, sem):
    del init_ref
    tm = h_ref.shape[0]
    base = pl.program_id(0) * (2 * tm)
    for r in range(tm):
        for k in range(2):
            pltpu.make_async_copy(h_ref.at[pl.ds(r, 1), :],
                                  xs_ref.at[pl.ds(pos_ref[base + 2 * r + k], 1), :], sem).start()
    for k in range(2):
        pltpu.make_async_copy(h_ref, xs_ref.at[pl.ds(0, tm), :], sem).wait()


def _moe_dispatch(pos_flat, h2, rows):
    n, d = h2.shape
    tm = TOKEN_BLOCK
    return pl.pallas_call(
        _dispatch_kernel,
        grid_spec=pltpu.PrefetchScalarGridSpec(
            num_scalar_prefetch=1,
            grid=(n // tm,),
            in_specs=[pl.BlockSpec((tm, d), lambda i, pos: (i, 0)),
                      pl.BlockSpec(memory_space=pl.ANY)],
            out_specs=pl.BlockSpec(memory_space=pl.ANY),
            scratch_shapes=[pltpu.SemaphoreType.DMA(())]),
        out_shape=jax.ShapeDtypeStruct((rows, d), F32),
        input_output_aliases={2: 0},
        compiler_params=_cparams("arbitrary"),
        name="moe_dispatch",
    )(pos_flat, h2, jnp.zeros((rows, d), F32))


def _combine_kernel(x_ref, mod_ref, route_ref, ya_ref, yb_ref, nf_ref, o_ref):
    route = route_ref[...]
    moe = route[:, 0:1] * ya_ref[...] + route[:, 1:2] * yb_ref[...]
    x2 = x_ref[...] + mod_ref[5:6, :] * moe
    o_ref[...] = _rms(x2, nf_ref[...])


def _moe_combine(x1, modsel, route, ya, yb, norm_f_w):
    bsz, s, d = x1.shape
    tm = TOKEN_BLOCK
    row = lambda w: pl.BlockSpec((None, tm, w), lambda b, j: (b, j, 0))
    return pl.pallas_call(
        _combine_kernel,
        grid=(bsz, s // tm),
        in_specs=[row(d), pl.BlockSpec((None, None, 8, d), lambda b, j: (b, 1, 0, 0)),
                  row(LANES), row(d), row(d), pl.BlockSpec((1, d), lambda b, j: (0, 0))],
        out_specs=row(d),
        out_shape=jax.ShapeDtypeStruct((bsz, s, d), F32),
        compiler_params=_cparams("parallel", "parallel"),
        name="moe_combine",
    )(x1, modsel, route, ya, yb, norm_f_w)


def _final_norm_kernel(x_ref, nf_ref, o_ref):
    o_ref[...] = _rms(x_ref[...], nf_ref[...])


def _final_norm(x, norm_f_w, first_row):
    bsz, s, d = x.shape
    tm = TOKEN_BLOCK
    first = first_row // tm
    return pl.pallas_call(
        _final_norm_kernel,
        grid=(bsz, (s - first_row) // tm),
        in_specs=[pl.BlockSpec((None, tm, d), lambda b, j: (b, j + first, 0)),
                  pl.BlockSpec((1, d), lambda b, j: (0, 0))],
        out_specs=pl.BlockSpec((None, tm, d), lambda b, j: (b, j, 0)),
        out_shape=jax.ShapeDtypeStruct((bsz, s - first_row, d), F32),
        compiler_params=_cparams("parallel", "parallel"),
        name="final_norm",
    )(x, norm_f_w)


def _moe_plan(route, tm):
    n = route.shape[0]
    e_flat = route[:, 2:4].astype(jnp.int32).reshape(-1)
    onehot = (e_flat[:, None] == jnp.arange(N_EXPERTS, dtype=jnp.int32)[None, :]).astype(jnp.int32)
    csum = jnp.cumsum(onehot, axis=0)
    counts = csum[-1]
    padded = ((counts + tm - 1) // tm) * tm
    ends = jnp.cumsum(padded)
    pos = jnp.sum(onehot * (csum - 1 + (ends - padded)[None, :]), axis=1)
    rows = 2 * n + N_EXPERTS * tm
    n_tiles = rows // tm
    starts = jnp.arange(n_tiles, dtype=jnp.int32) * tm
    tile_expert = jnp.minimum(jnp.sum((ends[None, :] <= starts[:, None]).astype(jnp.int32), axis=1),
                              N_EXPERTS - 1)
    n_valid = (ends[-1] // tm).astype(jnp.int32).reshape(1)
    token_end = (ends - padded + counts)[tile_expert]
    valid_rows = jnp.clip(token_end - starts, 0, tm).astype(jnp.int32)
    return rows, pos.astype(jnp.int32), tile_expert, n_valid, valid_rows


def _rope_tables(n_ctx, seq, head_dim, copies):
    half = head_dim // 4
    t = np.arange(seq)
    freqs = ROPE_THETA ** (-jnp.arange(half, dtype=F32) / half)
    rows = jnp.asarray(t // GRID_W, F32)[:, None] * freqs[None, :]
    cols = jnp.asarray(t % GRID_W, F32)[:, None] * freqs[None, :]
    cos = jnp.concatenate([jnp.cos(rows)] * 2 + [jnp.cos(cols)] * 2, axis=1)
    sin = jnp.concatenate([-jnp.sin(rows), jnp.sin(rows), -jnp.sin(cols), jnp.sin(cols)], axis=1)
    cos = jnp.concatenate([jnp.ones((n_ctx, head_dim), F32), cos], axis=0)
    sin = jnp.concatenate([jnp.zeros((n_ctx, head_dim), F32), sin], axis=0)
    return jnp.tile(cos, (1, copies)), jnp.tile(sin, (1, copies))


def _group_mean_matrix(width, group):
    g = np.kron(np.eye(width // group, dtype=np.float32), np.ones((group, group), np.float32)) / group
    return jnp.asarray(g, BF16)


def kernel(x, c, ctx, c_ctx, w_mod, b_mod, norm_mix, norm_ffn, w_in, w_out, gla_a2_f, gla_ab_f, gla_a2_b,
           gla_ab_b, gla_norm, gqa_q_norm, gqa_k_norm, diff_lam_q1, diff_lam_k1, diff_lam_q2, diff_lam_k2,
           diff_norm, ffn_gate, ffn_up, ffn_down, moe_router, moe_gate, moe_up, moe_down, norm_f):
    bsz, seq, d = x.shape
    n_ctx = ctx.shape[1]
    depth = w_mod.shape[0]

    c_rows = jnp.concatenate([c, c_ctx[None, :], jnp.zeros((8 - bsz - 1, d), F32)], axis=0)
    mod = _modulation(c_rows, w_mod, b_mod).reshape(depth, 8, 6, d)
    mod = jnp.pad(mod, ((0, 0), (0, 0), (0, 2), (0, 0)))
    modsel = jnp.stack([jnp.broadcast_to(mod[:, bsz][:, None], (depth, bsz, 8, d)), mod[:, :bsz]], axis=2)

    tables = (_rope_tables(n_ctx, seq, GQA_HD, LANES // GQA_HD) + _rope_tables(n_ctx, seq, DIFF_QK, LANES // DIFF_QK))
    gmat = _group_mean_matrix(GQA_HEADS * GQA_HD, GQA_HD)
    gmat_v = _group_mean_matrix(GLA_HEADS * GLA_DV, GLA_DV)

    w_in_p = jnp.concatenate([w_in[:, :, :_W_IN_LR_END],
                              jnp.zeros((depth, d, _W_GQA_Q - _W_IN_LR_END), F32),
                              w_in[:, :, _W_IN_LR_END:]], axis=2).astype(BF16)
    w_out_b = w_out.astype(BF16)
    a2 = jnp.zeros((depth, 2, LANES, LANES), F32)
    a2 = a2.at[:, 0, 0:GLA_GATE_RANK].set(gla_a2_f).at[:, 1, GLA_GATE_RANK:2 * GLA_GATE_RANK].set(gla_a2_b)
    a2 = a2.astype(BF16)
    ab = jnp.stack([gla_ab_f, gla_ab_b], axis=1)[:, :, None, :]

    x_all, x_ctx = x, ctx
    out = None
    for l in range(depth):
        ctx_out = l < depth - 1
        lambda_init = 0.8 - 0.6 * math.exp(-0.3 * l)
        gla_qk, gla_v, gla_r, gla_lr, gq, gk, gv, dq, dk, dv = _in_projection(
            x_all, x_ctx, modsel[l], norm_mix[l][None, :], w_in_p[l], gmat,
            jnp.tile(gqa_q_norm[l], GQA_HEADS)[None, :], jnp.tile(gqa_k_norm[l], GQA_KV_HEADS)[None, :], tables)
        o_f, o_b = _gla(gla_qk, gla_v, gla_lr, a2[l], ab[l], n_ctx // GLA_CHUNK)
        lam_rows = jnp.stack([diff_lam_q1[l], diff_lam_k1[l], diff_lam_q2[l], diff_lam_k2[l]], axis=0)
        gqa_o, diff_o = _attention(gq, gk, gv, dq, dk, dv, lam_rows, diff_norm[l][None, :], n_ctx, lambda_init,
                                   ctx_out)
        moe_layer = l % 2 == 1
        i = l // 2
        res = _out_projection(x_all, x_ctx, modsel[l], o_f, o_b, gla_r, gqa_o, diff_o, w_out_b[l], gmat_v,
                              jnp.tile(gla_norm[l], GLA_HEADS)[None, :], norm_ffn[l][None, :],
                              moe_router[i].T if moe_layer else None, ctx_out)
        if not moe_layer:
            x1, h2 = res
            if not ctx_out:
                raise NotImplementedError("dense feed-forward on the last layer")
            x_all = _dense_ffn(h2, x1, modsel[l], ffn_gate[i].astype(BF16), ffn_up[i].astype(BF16),
                               ffn_down[i].astype(BF16), n_ctx)
            x_ctx = None
            if l == depth - 1:
                out = _final_norm(x_all, norm_f[None, :], n_ctx)
        else:
            x1, h2, route = res
            if ctx_out or l != depth - 1:
                raise NotImplementedError("expert feed-forward before the last layer")
            n = bsz * seq
            rows, pos, tile_expert, n_valid, valid_rows = _moe_plan(route.reshape(n, LANES), MOE_TILE)
            xs = _moe_dispatch(pos, h2.reshape(n, d), rows)
            ys = _moe_experts(tile_expert, n_valid, valid_rows, xs, moe_gate[i], moe_up[i], moe_down[i])
            ya = ys.at[pos[0::2]].get(mode="promise_in_bounds").reshape(bsz, seq, d)
            yb = ys.at[pos[1::2]].get(mode="promise_in_bounds").reshape(bsz, seq, d)
            out = _moe_combine(x1, modsel[l], route, ya, yb, norm_f[None, :])
    return out
```

```python
import functools
import math

import jax
import jax.numpy as jnp
import numpy as np
from jax import lax
from jax.experimental import pallas as pl
from jax.experimental.pallas import tpu as pltpu

F32 = jnp.float32
BF16 = jnp.bfloat16

GRID_W = 64
GLA_HEADS, GLA_DK, GLA_DV = 4, 32, 64
GLA_GATE_RANK = 16
GLA_GATE_NORM = 16.0
GLA_CHUNK = 64
GLA_STEP_CHUNKS = 4
GQA_HEADS, GQA_KV_HEADS, GQA_HD = 8, 2, 64
GQA_GROUP = GQA_HEADS // GQA_KV_HEADS
DIFF_HEADS, DIFF_QK, DIFF_V = 4, 32, 64
ROPE_THETA = 10000.0
NORM_EPS = 1e-6
N_EXPERTS = 8

LANES = 128
TOKEN_BLOCK = 256
VMEM_LIMIT = 56 * 1024 * 1024

_W_GLA_QK = 0
_W_GLA_V = 256
_W_GLA_R = 512
_W_GLA_LR = 768
_W_GQA_Q = 896
_W_GQA_K = 1408
_W_GQA_V = 1536
_W_DIFF_Q = 1664
_W_DIFF_K = 1920
_W_DIFF_V = 2176
_W_IN_PAD = 2432
_W_IN_LR_END = 800


def _cparams(*sem):
    return pltpu.CompilerParams(dimension_semantics=sem, vmem_limit_bytes=VMEM_LIMIT)


def _rms(x, w):
    ms = jnp.mean(x * x, axis=-1, keepdims=True)
    return x * lax.rsqrt(ms + NORM_EPS) * w


def _silu(x):
    return x * (1.0 / (1.0 + jnp.exp(-x)))


def _nt_dot(a, b):
    return lax.dot_general(a, b, (((1,), (1,)), ((), ())), preferred_element_type=F32)


def _tn_dot(a, b):
    return lax.dot_general(a, b, (((0,), (0,)), ((), ())), preferred_element_type=F32)


def _mod_kernel(c_ref, w_ref, b_ref, o_ref):
    s = _silu(c_ref[...])
    o_ref[...] = jnp.dot(s.astype(BF16), w_ref[...].astype(BF16),
                         preferred_element_type=F32) + b_ref[...]


def _modulation(c_rows, w_mod, b_mod):
    depth, d, n = w_mod.shape
    tn = 1536
    return pl.pallas_call(
        _mod_kernel,
        grid=(depth, n // tn),
        in_specs=[pl.BlockSpec((8, d), lambda l, j: (0, 0)),
                  pl.BlockSpec((None, d, tn), lambda l, j: (l, 0, j)),
                  pl.BlockSpec((None, 1, tn), lambda l, j: (l, 0, j))],
        out_specs=pl.BlockSpec((None, 8, tn), lambda l, j: (l, 0, j)),
        out_shape=jax.ShapeDtypeStruct((depth, 8, n), F32),
        compiler_params=_cparams("parallel", "parallel"),
        name="modulation",
    )(c_rows, w_mod, b_mod.reshape(depth, 1, n))


def _rope(x, cos, sin_signed, half):
    width = x.shape[1]
    lane = lax.broadcasted_iota(jnp.int32, x.shape, 1)
    first = (lane % (2 * half)) < half
    rot = jnp.where(first, pltpu.roll(x, width - half, 1), pltpu.roll(x, half, 1))
    return x * cos + rot * sin_signed


def _token_rows(refs, split_ctx):
    if not split_ctx:
        return refs[0][...], refs[1:]
    return jnp.where(pl.program_id(1) == 0, refs[0][...], refs[1][...]), refs[2:]


def _token_specs(x, ctx, tm, first):
    d = x.shape[-1]
    if ctx is None:
        return [pl.BlockSpec((None, tm, d), lambda b, j: (b, j + first, 0))], [x]
    assert ctx.shape[1] == tm and first == 0
    return ([pl.BlockSpec((None, tm, d), lambda b, j: (b, 0, 0)),
             pl.BlockSpec((None, tm, d), lambda b, j: (b, jnp.maximum(j - 1, 0), 0))], [ctx, x])


def _values_and_ones(v, head_dim):
    ones = jnp.ones((v.shape[0], head_dim), v.dtype)
    parts = []
    for h in range(v.shape[1] // head_dim):
        parts += [v[:, h * head_dim:(h + 1) * head_dim], ones]
    return jnp.concatenate(parts, axis=1)


def _in_kernel(*refs, split_ctx):
    x, refs = _token_rows(refs, split_ctx)
    (mod_ref, nw_ref, w_ref, g_ref, qg_ref, kg_ref, cg_ref, sg_ref, cd_ref, sd_ref,
     glaqk_ref, glav_ref, glar_ref, glalr_ref, gq_ref, gk_ref, gv_ref, dq_ref, dk_ref, dv_ref) = refs
    mod = mod_ref[...]
    h = _rms(x, nw_ref[...]) * (1.0 + mod[1:2]) + mod[0:1]
    hb = h.astype(BF16)

    def proj(a, b):
        return jnp.dot(hb, w_ref[:, a:b], preferred_element_type=F32)

    glaqk_ref[...] = proj(_W_GLA_QK, _W_GLA_V)
    glav_ref[...] = proj(_W_GLA_V, _W_GLA_R).astype(BF16)
    glar_ref[...] = proj(_W_GLA_R, _W_GLA_LR)
    glalr_ref[...] = proj(_W_GLA_LR, _W_GQA_Q)

    cos_k = cg_ref[...]
    sin_k = sg_ref[...]
    reps = GQA_HEADS * GQA_HD // LANES
    q = proj(_W_GQA_Q, _W_GQA_K)
    ms = jnp.dot((q * q).astype(BF16), g_ref[...], preferred_element_type=F32)
    qn = q * lax.rsqrt(ms + NORM_EPS) * qg_ref[...]
    gq_ref[...] = (_rope(qn, jnp.concatenate([cos_k] * reps, axis=1), jnp.concatenate([sin_k] * reps, axis=1),
                         GQA_HD // 4) * GQA_HD ** -0.5).astype(BF16)

    kw = GQA_KV_HEADS * GQA_HD
    k = proj(_W_GQA_K, _W_GQA_V)
    ms = jnp.dot((k * k).astype(BF16), g_ref[0:kw, 0:kw], preferred_element_type=F32)
    kn = k * lax.rsqrt(ms + NORM_EPS) * kg_ref[...]
    gk_ref[...] = _rope(kn, cos_k, sin_k, GQA_HD // 4).astype(BF16)
    gv_ref[...] = _values_and_ones(proj(_W_GQA_V, _W_DIFF_Q), GQA_HD).astype(BF16)

    reps = 2 * DIFF_HEADS * DIFF_QK // LANES
    cos_d = jnp.concatenate([cd_ref[...]] * reps, axis=1)
    sin_d = jnp.concatenate([sd_ref[...]] * reps, axis=1)
    dq = proj(_W_DIFF_Q, _W_DIFF_K)
    dq_ref[...] = (_rope(dq, cos_d, sin_d, DIFF_QK // 4) * DIFF_QK ** -0.5).astype(BF16)
    dk = proj(_W_DIFF_K, _W_DIFF_V)
    dk_ref[...] = _rope(dk, cos_d, sin_d, DIFF_QK // 4).astype(BF16)
    dv_ref[...] = _values_and_ones(proj(_W_DIFF_V, _W_IN_PAD), DIFF_V).astype(BF16)


def _in_projection(x, ctx, modsel, norm_w, w_in_p, gmat, q_gain, k_gain, tables):
    bsz, s, d = x.shape
    tm = TOKEN_BLOCK
    if ctx is not None:
        s += ctx.shape[1]
    x_specs, x_args = _token_specs(x, ctx, tm, 0)
    cos_g, sin_g, cos_d, sin_d = tables
    row = lambda w: pl.BlockSpec((None, tm, w), lambda b, j: (b, j, 0))
    const = lambda shape: pl.BlockSpec(shape, lambda b, j: (0,) * len(shape))
    tab = lambda w: pl.BlockSpec((tm, w), lambda b, j: (j, 0))
    widths = [(256, F32), (256, BF16), (256, F32), (128, F32), (512, BF16), (128, BF16), (256, BF16),
              (256, BF16), (256, BF16), (512, BF16)]
    return pl.pallas_call(
        functools.partial(_in_kernel, split_ctx=ctx is not None),
        grid=(bsz, s // tm),
        in_specs=x_specs + [
                  pl.BlockSpec((None, None, 8, d), lambda b, j: (b, jnp.minimum(j, 1), 0, 0)),
                  const((1, d)), const((d, _W_IN_PAD)), const((512, 512)),
                  const((1, 512)), const((1, 128)),
                  tab(LANES), tab(LANES), tab(LANES), tab(LANES)],
        out_specs=[row(w) for w, _ in widths],
        out_shape=[jax.ShapeDtypeStruct((bsz, s, w), dt) for w, dt in widths],
        compiler_params=_cparams("parallel", "parallel"),
        name="in_projection",
    )(*x_args, modsel, norm_w, w_in_p, gmat, q_gain, k_gain, cos_g, sin_g, cos_d, sin_d)


def _split3(g):
    g1 = g.astype(BF16)
    r1 = g - g1.astype(F32)
    g2 = r1.astype(BF16)
    g3 = (r1 - g2.astype(F32)).astype(BF16)
    return jnp.concatenate([g1, g2, g3], axis=1)


def _gla_kernel(qkf_ref, vf_ref, lrf_ref, qkb_ref, vb_ref, lrb_ref, a2_ref, ab_ref,
                of_ref, ob_ref, st_ref):
    @pl.when(pl.program_id(0) == 0)
    def _():
        st_ref[...] = jnp.zeros_like(st_ref)

    n_b = qkf_ref.shape[0]
    c = GLA_CHUNK
    hk = GLA_HEADS * GLA_DK
    hv = GLA_HEADS * GLA_DV
    r_i = lax.broadcasted_iota(jnp.int32, (c, c), 0)
    c_i = lax.broadcasted_iota(jnp.int32, (c, c), 1)
    head_of_k = lax.broadcasted_iota(jnp.int32, (c, hk), 1) // GLA_DK
    head_of_v = lax.broadcasted_iota(jnp.int32, (c, hv), 1) // GLA_DV
    dirs = ((qkf_ref, vf_ref, lrf_ref, of_ref, c_i <= r_i, c - 1),
            (qkb_ref, vb_ref, lrb_ref, ob_ref, c_i >= r_i, 0))
    n_sub = qkf_ref.shape[1] // c
    for d, (qk_ref, v_ref, lr_ref, o_ref, tri, last) in enumerate(dirs):
        tri_b = jnp.where(tri, 1.0, 0.0).astype(BF16)
        tri_heads = jnp.concatenate([tri] * GLA_HEADS, axis=0)
        lr = jnp.concatenate([lr_ref[b] for b in range(n_b)], axis=0)
        pre = jnp.dot(lr.astype(BF16), a2_ref[d], preferred_element_type=F32) + ab_ref[d]
        log_a = (jnp.minimum(pre, 0.0) - jnp.log(1.0 + jnp.exp(-jnp.abs(pre)))) * (1.0 / GLA_GATE_NORM)
        parts = _split3(log_a)
        cs = jnp.dot(tri_b, jnp.concatenate([parts[u * c:(u + 1) * c] for u in range(n_b * n_sub)], axis=1),
                     preferred_element_type=F32)
        for b in range(n_b):
            st = st_ref[d, b]
            for sub in (range(n_sub) if d == 0 else reversed(range(n_sub))):
                rows = slice(sub * c, (sub + 1) * c)
                base = (b * n_sub + sub) * 3 * hk
                bc = cs[:, base:base + hk] + cs[:, base + hk:base + 2 * hk] + cs[:, base + 2 * hk:base + 3 * hk]
                tot = bc[last:last + 1, :]
                qk = qk_ref[b, rows, :]
                q = qk[:, 0:hk] * GLA_DK ** -0.5
                k = qk[:, hk:2 * hk]
                qd = q * jnp.exp(bc)
                kd = (k * jnp.exp(-bc)).astype(BF16)
                kl = (k * jnp.exp(tot - bc)).astype(BF16)
                v = v_ref[b, rows, :]
                q_heads = jnp.concatenate([jnp.where(head_of_k == h, qd, 0.0) for h in range(GLA_HEADS)],
                                          axis=0).astype(BF16)
                a = jnp.where(tri_heads, _nt_dot(q_heads, kd), 0.0).astype(BF16)
                o_all = jnp.dot(a, v, preferred_element_type=F32) + _nt_dot(q_heads, st.astype(BF16))
                o = jnp.where(head_of_v == 0, o_all[0:c], 0.0)
                for h in range(1, GLA_HEADS):
                    o = o + jnp.where(head_of_v == h, o_all[h * c:(h + 1) * c], 0.0)
                o_ref[b, rows, :] = o
                st = st * jnp.exp(tot) + _tn_dot(v, kl)
            st_ref[d, b] = st


def _gla(gla_qk, gla_v, gla_lr, a2, ab, n_ctx_chunks):
    bsz, s, _ = gla_qk.shape
    c = GLA_STEP_CHUNKS * GLA_CHUNK
    n = s // c
    n_ctx_steps = n_ctx_chunks // GLA_STEP_CHUNKS
    assert n * c == s and n_ctx_steps * GLA_STEP_CHUNKS == n_ctx_chunks
    fwd = lambda i: (0, i, 0)
    bwd = lambda i: (0, jnp.where(i < n_ctx_steps, n_ctx_steps - 1 - i, n + n_ctx_steps - 1 - i), 0)
    blk = lambda w, m: pl.BlockSpec((bsz, c, w), m)
    hv = GLA_HEADS * GLA_DV
    return pl.pallas_call(
        _gla_kernel,
        grid=(n,),
        in_specs=[blk(256, fwd), blk(hv, fwd), blk(LANES, fwd),
                  blk(256, bwd), blk(hv, bwd), blk(LANES, bwd),
                  pl.BlockSpec((2, LANES, LANES), lambda i: (0, 0, 0)),
                  pl.BlockSpec((2, 1, LANES), lambda i: (0, 0, 0))],
        out_specs=[blk(hv, fwd), blk(hv, bwd)],
        out_shape=[jax.ShapeDtypeStruct((bsz, s, hv), F32)] * 2,
        scratch_shapes=[pltpu.VMEM((2, bsz, hv, GLA_HEADS * GLA_DK), F32)],
        compiler_params=_cparams("arbitrary"),
        name="gla_scan",
    )(gla_qk, gla_v, gla_lr, gla_qk, gla_v, gla_lr, a2, ab)


def _softmax_attend(q, k, v_ones, k_transposed=False):
    s = jnp.dot(q, k, preferred_element_type=F32) if k_transposed else _nt_dot(q, k)
    p = jnp.exp((s - jnp.max(s, axis=-1, keepdims=True)).astype(BF16))
    o = jnp.dot(p, v_ones, preferred_element_type=F32)
    hd = v_ones.shape[1] // 2
    return o[:, 0:hd] * (1.0 / o[:, hd:hd + 1])


def _attn_kernel(gq_ref, gk_ref, gv_ref, dq_ref, dkt_ref, dv_ref, lam_ref, dnw_ref, go_ref, do_ref,
                 *, n_ctx, out_scale, first_block):
    lam = (jnp.exp(jnp.sum(lam_ref[0:1, :] * lam_ref[1:2, :], axis=-1, keepdims=True))
           - jnp.exp(jnp.sum(lam_ref[2:3, :] * lam_ref[3:4, :], axis=-1, keepdims=True))
           + (1.0 - out_scale))
    tq = gq_ref.shape[0]

    def attend(nk):
        for kh in range(GQA_KV_HEADS):
            heads = [slice((kh * GQA_GROUP + g) * GQA_HD, (kh * GQA_GROUP + g + 1) * GQA_HD)
                     for g in range(GQA_GROUP)]
            q = jnp.concatenate([gq_ref[:, hs] for hs in heads], axis=0)
            o = _softmax_attend(q, gk_ref[0:nk, kh * GQA_HD:(kh + 1) * GQA_HD],
                                gv_ref[0:nk, 2 * kh * GQA_HD:2 * (kh + 1) * GQA_HD])
            for g, hs in enumerate(heads):
                go_ref[:, hs] = o[g * tq:(g + 1) * tq].astype(BF16)
        for h in range(DIFF_HEADS):
            base = h * 2 * DIFF_QK
            v = dv_ref[0:nk, 2 * h * DIFF_V:2 * (h + 1) * DIFF_V]
            o = (_softmax_attend(dq_ref[:, base:base + DIFF_QK], dkt_ref[base:base + DIFF_QK, 0:nk], v, True)
                 - lam * _softmax_attend(dq_ref[:, base + DIFF_QK:base + 2 * DIFF_QK],
                                         dkt_ref[base + DIFF_QK:base + 2 * DIFF_QK, 0:nk], v, True))
            do_ref[:, h * DIFF_V:(h + 1) * DIFF_V] = (_rms(o, dnw_ref[...]) * out_scale).astype(BF16)

    if first_block == 0:
        @pl.when(pl.program_id(1) == 0)
        def _():
            attend(n_ctx)

        @pl.when(pl.program_id(1) > 0)
        def _():
            attend(gk_ref.shape[0])
    else:
        attend(gk_ref.shape[0])


def _attention(gq, gk, gv, dq, dk, dv, lam_rows, diff_norm_w, n_ctx, lambda_init, ctx_out):
    bsz, s, _ = gq.shape
    tq = TOKEN_BLOCK
    assert n_ctx == tq
    first = 0 if ctx_out else 1
    s_out = s - first * tq
    kw = GQA_KV_HEADS * GQA_HD
    qrow = lambda w: pl.BlockSpec((None, tq, w), lambda b, j: (b, j + first, 0))
    orow = lambda w: pl.BlockSpec((None, tq, w), lambda b, j: (b, j, 0))
    full = lambda w: pl.BlockSpec((None, s, w), lambda b, j: (b, 0, 0))
    return pl.pallas_call(
        functools.partial(_attn_kernel, n_ctx=n_ctx, out_scale=1.0 - lambda_init, first_block=first),
        grid=(bsz, s_out // tq),
        in_specs=[qrow(512), full(kw), full(2 * kw), qrow(256),
                  pl.BlockSpec((None, dk.shape[2], s), lambda b, j: (b, 0, 0)), full(512),
                  pl.BlockSpec((4, DIFF_QK), lambda b, j: (0, 0)),
                  pl.BlockSpec((1, DIFF_V), lambda b, j: (0, 0))],
        out_specs=[orow(512), orow(256)],
        out_shape=[jax.ShapeDtypeStruct((bsz, s_out, 512), BF16),
                   jax.ShapeDtypeStruct((bsz, s_out, 256), BF16)],
        compiler_params=_cparams("parallel", "parallel"),
        name="attention",
    )(gq, gk, gv, dq, jnp.swapaxes(dk, 1, 2), dv, lam_rows, diff_norm_w)


def _route(h, wr_ref):
    logits = [jnp.sum(h * wr_ref[e:e + 1, :], axis=-1, keepdims=True) for e in range(N_EXPERTS)]

    def top(ls):
        m = functools.reduce(jnp.maximum, ls)
        idx = jnp.full_like(m, float(N_EXPERTS))
        for e in reversed(range(N_EXPERTS)):
            idx = jnp.where(ls[e] == m, float(e), idx)
        return m, idx

    m1, i1 = top(logits)
    m2, i2 = top([jnp.where(i1 == float(e), -jnp.inf, logits[e]) for e in range(N_EXPERTS)])
    e2 = jnp.exp(m2 - m1)
    w1 = 1.0 / (1.0 + e2)
    w2 = e2 * w1
    lane = lax.broadcasted_iota(jnp.int32, (h.shape[0], LANES), 1)
    return jnp.where(lane == 0, w1, jnp.where(lane == 1, w2, jnp.where(lane == 2, i1,
                     jnp.where(lane == 3, i2, 0.0))))


def _out_kernel(*refs, with_router, split_ctx):
    x, refs = _token_rows(refs, split_ctx)
    if with_router:
        (mod_ref, of_ref, ob_ref, r_ref, go_ref, do_ref, w_ref, g_ref, gnw_ref, nw_ref, wr_ref,
         x1_ref, h2_ref, route_ref) = refs
    else:
        (mod_ref, of_ref, ob_ref, r_ref, go_ref, do_ref, w_ref, g_ref, gnw_ref, nw_ref,
         x1_ref, h2_ref) = refs
    mod = mod_ref[...]
    hv = GLA_HEADS * GLA_DV
    gw = GQA_HEADS * GQA_HD
    o = of_ref[...] + ob_ref[...]
    ms = jnp.dot((o * o).astype(BF16), g_ref[...], preferred_element_type=F32)
    gla = o * lax.rsqrt(ms + NORM_EPS) * gnw_ref[...] * _silu(r_ref[...])
    y = (jnp.dot(gla.astype(BF16), w_ref[0:hv, :], preferred_element_type=F32)
         + jnp.dot(go_ref[...], w_ref[hv:hv + gw, :], preferred_element_type=F32)
         + jnp.dot(do_ref[...], w_ref[hv + gw:, :], preferred_element_type=F32))
    x1 = x + mod[2:3] * y
    x1_ref[...] = x1
    h2 = _rms(x1, nw_ref[...]) * (1.0 + mod[4:5]) + mod[3:4]
    h2_ref[...] = h2.astype(h2_ref.dtype)
    if with_router:
        route_ref[...] = _route(h2, wr_ref)


def _out_projection(x, ctx, modsel, o_f, o_b, gla_r, gqa_o, diff_o, w_out, gmat, gla_norm_w, norm_ffn_w,
                    router_t, ctx_out):
    bsz, s, d = x.shape
    tm = TOKEN_BLOCK
    if ctx is not None:
        s += ctx.shape[1]
    first = 0 if ctx_out else 1
    s_out = s - first * tm
    x_specs, x_args = _token_specs(x, ctx, tm, first)
    arow = lambda w: pl.BlockSpec((None, tm, w), lambda b, j: (b, j + first, 0))
    orow = lambda w: pl.BlockSpec((None, tm, w), lambda b, j: (b, j, 0))
    const = lambda shape: pl.BlockSpec(shape, lambda b, j: (0,) * len(shape))
    with_router = router_t is not None
    in_specs = x_specs + [
                pl.BlockSpec((None, None, 8, d), lambda b, j: (b, jnp.minimum(j + first, 1), 0, 0)),
                arow(256), arow(256), arow(256), orow(512), orow(256),
                const((d, d)), const((256, 256)), const((1, 256)), const((1, d))]
    args = x_args + [modsel, o_f, o_b, gla_r, gqa_o, diff_o, w_out, gmat, gla_norm_w, norm_ffn_w]
    out_specs = [orow(d), orow(d)]
    out_shape = [jax.ShapeDtypeStruct((bsz, s_out, d), F32),
                 jax.ShapeDtypeStruct((bsz, s_out, d), F32 if with_router else BF16)]
    if with_router:
        in_specs.append(const((N_EXPERTS, d)))
        args.append(router_t)
        out_specs.append(orow(LANES))
        out_shape.append(jax.ShapeDtypeStruct((bsz, s_out, LANES), F32))
    return pl.pallas_call(
        functools.partial(_out_kernel, with_router=with_router, split_ctx=ctx is not None),
        grid=(bsz, s_out // tm),
        in_specs=in_specs, out_specs=out_specs, out_shape=out_shape,
        compiler_params=_cparams("parallel", "parallel"),
        name="out_projection",
    )(*args)


def _ffn_kernel(h_ref, x_ref, mod_ref, wg_ref, wu_ref, wd_ref, o_ref, acc_ref, *, n_ctx):
    f = pl.program_id(2)

    @pl.when(f == 0)
    def _():
        acc_ref[...] = jnp.zeros_like(acc_ref)

    h = h_ref[...]
    g = jnp.dot(h, wg_ref[...], preferred_element_type=F32)
    u = jnp.dot(h, wu_ref[...], preferred_element_type=F32)
    acc_ref[...] += jnp.dot((_silu(g) * u).astype(BF16), wd_ref[...], preferred_element_type=F32)

    @pl.when(f == pl.num_programs(2) - 1)
    def _():
        tm = h_ref.shape[0]
        row = pl.program_id(1) * tm + lax.broadcasted_iota(jnp.int32, (tm, 1), 0)
        gate = jnp.where(row < n_ctx, mod_ref[0, 5:6, :], mod_ref[1, 5:6, :])
        o_ref[...] = x_ref[...] + gate * acc_ref[...]


def _dense_ffn(h2, x1, modsel, wg, wu, wd, n_ctx):
    bsz, s, d = x1.shape
    ff = wg.shape[1]
    tm = s // 8
    tf = ff // 2
    row = lambda: pl.BlockSpec((None, tm, d), lambda b, t, f: (b, t, 0))
    return pl.pallas_call(
        functools.partial(_ffn_kernel, n_ctx=n_ctx),
        grid=(bsz, s // tm, ff // tf),
        in_specs=[row(), row(),
                  pl.BlockSpec((None, 2, 8, d), lambda b, t, f: (b, 0, 0, 0)),
                  pl.BlockSpec((d, tf), lambda b, t, f: (0, f)),
                  pl.BlockSpec((d, tf), lambda b, t, f: (0, f)),
                  pl.BlockSpec((tf, d), lambda b, t, f: (f, 0))],
        out_specs=row(),
        out_shape=jax.ShapeDtypeStruct((bsz, s, d), F32),
        scratch_shapes=[pltpu.VMEM((tm, d), F32)],
        compiler_params=_cparams("parallel", "parallel", "arbitrary"),
        name="dense_ffn",
    )(h2, x1, modsel, wg, wu, wd)


MOE_TILE = 1024
MOE_SUB = 512
MOE_FF_BLOCK = 896


def _moe_kernel(te_ref, nv_ref, vr_ref, xs_ref, wg_ref, wu_ref, wd_ref, o_ref, xb_ref):
    f = pl.program_id(1)
    last_f = pl.num_programs(1) - 1
    valid = vr_ref[pl.program_id(0)]

    @pl.when(valid > 0)
    def _():
        wg = wg_ref[...].astype(BF16)
        wu = wu_ref[...].astype(BF16)
        wd = wd_ref[...].astype(BF16)
        for sb in range(MOE_TILE // MOE_SUB):
            rows = pl.ds(sb * MOE_SUB, MOE_SUB)

            @pl.when(sb * MOE_SUB < valid)
            def _():
                @pl.when(f == 0)
                def _():
                    o_ref[rows, :] = jnp.zeros((MOE_SUB, o_ref.shape[1]), F32)
                    xb_ref[rows, :] = xs_ref[rows, :].astype(BF16)

                h = xb_ref[rows, :]
                g = jnp.dot(h, wg, preferred_element_type=F32)
                u = jnp.dot(h, wu, preferred_element_type=F32)
                o_ref[rows, :] += jnp.dot((_silu(g) * u).astype(BF16), wd, preferred_element_type=F32)

            @pl.when(jnp.logical_and(sb * MOE_SUB >= valid, f == last_f))
            def _():
                o_ref[rows, :] = jnp.zeros((MOE_SUB, o_ref.shape[1]), F32)

    @pl.when(jnp.logical_and(valid == 0, f == last_f))
    def _():
        o_ref[...] = jnp.zeros_like(o_ref)


def _moe_experts(tile_expert, n_valid, valid_rows, xs, wg, wu, wd):
    rows, d = xs.shape
    ff = wg.shape[2]
    tm = MOE_TILE
    tf = MOE_FF_BLOCK
    nf = ff // tf

    def tile(i, nv):
        return jnp.minimum(i, nv[0] - 1)

    def fidx(i, f, nv):
        return jnp.where(i < nv[0], f, nf - 1)

    return pl.pallas_call(
        _moe_kernel,
        grid_spec=pltpu.PrefetchScalarGridSpec(
            num_scalar_prefetch=3,
            grid=(rows // tm, nf),
            in_specs=[pl.BlockSpec((tm, d), lambda i, f, te, nv, vr: (tile(i, nv), 0)),
                      pl.BlockSpec((None, d, tf), lambda i, f, te, nv, vr: (te[tile(i, nv)], 0, fidx(i, f, nv))),
                      pl.BlockSpec((None, d, tf), lambda i, f, te, nv, vr: (te[tile(i, nv)], 0, fidx(i, f, nv))),
                      pl.BlockSpec((None, tf, d), lambda i, f, te, nv, vr: (te[tile(i, nv)], fidx(i, f, nv), 0))],
            out_specs=pl.BlockSpec((tm, d), lambda i, f, te, nv, vr: (i, 0)),
            scratch_shapes=[pltpu.VMEM((tm, d), BF16)]),
        out_shape=jax.ShapeDtypeStruct((rows, d), F32),
        compiler_params=_cparams("arbitrary", "arbitrary"),
        name="moe_experts",
    )(tile_expert, n_valid, valid_rows, xs, wg, wu, wd)


def _dispatch_kernel(pos_ref, h_ref, init_ref, xs_ref, sem):
    del init_ref
    tm = h_ref.shape[0]
    base = pl.program_id(0) * (2 * tm)
    for r in range(tm):
        for k in range(2):
            pltpu.make_async_copy(h_ref.at[pl.ds(r, 1), :],
                                  xs_ref.at[pl.ds(pos_ref[base + 2 * r + k], 1), :], sem).start()
    for k in range(2):
        pltpu.make_async_copy(h_ref, xs_ref.at[pl.ds(0, tm), :], sem).wait()


def _moe_dispatch(pos_flat, h2, rows):
    n, d = h2.shape
    tm = TOKEN_BLOCK
    return pl.pallas_call(
        _dispatch_kernel,
        grid_spec=pltpu.PrefetchScalarGridSpec(
            num_scalar_prefetch=1,
            grid=(n // tm,),
            in_specs=[pl.BlockSpec((tm, d), lambda i, pos: (i, 0)),
                      pl.BlockSpec(memory_space=pl.ANY)],
            out_specs=pl.BlockSpec(memory_space=pl.ANY),
            scratch_shapes=[pltpu.SemaphoreType.DMA(())]),
        out_shape=jax.ShapeDtypeStruct((rows, d), F32),
        input_output_aliases={2: 0},
        compiler_params=_cparams("arbitrary"),
        name="moe_dispatch",
    )(pos_flat, h2, jnp.zeros((rows, d), F32))


def _combine_kernel(x_ref, mod_ref, route_ref, ya_ref, yb_ref, nf_ref, o_ref):
    route = route_ref[...]
    moe = route[:, 0:1] * ya_ref[...] + route[:, 1:2] * yb_ref[...]
    x2 = x_ref[...] + mod_ref[5:6, :] * moe
    o_ref[...] = _rms(x2, nf_ref[...])


def _moe_combine(x1, modsel, route, ya, yb, norm_f_w):
    bsz, s, d = x1.shape
    tm = TOKEN_BLOCK
    row = lambda w: pl.BlockSpec((None, tm, w), lambda b, j: (b, j, 0))
    return pl.pallas_call(
        _combine_kernel,
        grid=(bsz, s // tm),
        in_specs=[row(d), pl.BlockSpec((None, None, 8, d), lambda b, j: (b, 1, 0, 0)),
                  row(LANES), row(d), row(d), pl.BlockSpec((1, d), lambda b, j: (0, 0))],
        out_specs=row(d),
        out_shape=jax.ShapeDtypeStruct((bsz, s, d), F32),
        compiler_params=_cparams("parallel", "parallel"),
        name="moe_combine",
    )(x1, modsel, route, ya, yb, norm_f_w)


def _final_norm_kernel(x_ref, nf_ref, o_ref):
    o_ref[...] = _rms(x_ref[...], nf_ref[...])


def _final_norm(x, norm_f_w, first_row):
    bsz, s, d = x.shape
    tm = TOKEN_BLOCK
    first = first_row // tm
    return pl.pallas_call(
        _final_norm_kernel,
        grid=(bsz, (s - first_row) // tm),
        in_specs=[pl.BlockSpec((None, tm, d), lambda b, j: (b, j + first, 0)),
                  pl.BlockSpec((1, d), lambda b, j: (0, 0))],
        out_specs=pl.BlockSpec((None, tm, d), lambda b, j: (b, j, 0)),
        out_shape=jax.ShapeDtypeStruct((bsz, s - first_row, d), F32),
        compiler_params=_cparams("parallel", "parallel"),
        name="final_norm",
    )(x, norm_f_w)


def _moe_plan(route, tm):
    n = route.shape[0]
    e_flat = route[:, 2:4].astype(jnp.int32).reshape(-1)
    onehot = (e_flat[:, None] == jnp.arange(N_EXPERTS, dtype=jnp.int32)[None, :]).astype(jnp.int32)
    csum = jnp.cumsum(onehot, axis=0)
    counts = csum[-1]
    padded = ((counts + tm - 1) // tm) * tm
    ends = jnp.cumsum(padded)
    pos = jnp.sum(onehot * (csum - 1 + (ends - padded)[None, :]), axis=1)
    rows = 2 * n + N_EXPERTS * tm
    n_tiles = rows // tm
    starts = jnp.arange(n_tiles, dtype=jnp.int32) * tm
    tile_expert = jnp.minimum(jnp.sum((ends[None, :] <= starts[:, None]).astype(jnp.int32), axis=1),
                              N_EXPERTS - 1)
    n_valid = (ends[-1] // tm).astype(jnp.int32).reshape(1)
    token_end = (ends - padded + counts)[tile_expert]
    valid_rows = jnp.clip(token_end - starts, 0, tm).astype(jnp.int32)
    return rows, pos.astype(jnp.int32), tile_expert, n_valid, valid_rows


def _rope_tables(n_ctx, seq, head_dim, copies):
    half = head_dim // 4
    t = np.arange(seq)
    freqs = ROPE_THETA ** (-jnp.arange(half, dtype=F32) / half)
    rows = jnp.asarray(t // GRID_W, F32)[:, None] * freqs[None, :]
    cols = jnp.asarray(t % GRID_W, F32)[:, None] * freqs[None, :]
    cos = jnp.concatenate([jnp.cos(rows)] * 2 + [jnp.cos(cols)] * 2, axis=1)
    sin = jnp.concatenate([-jnp.sin(rows), jnp.sin(rows), -jnp.sin(cols), jnp.sin(cols)], axis=1)
    cos = jnp.concatenate([jnp.ones((n_ctx, head_dim), F32), cos], axis=0)
    sin = jnp.concatenate([jnp.zeros((n_ctx, head_dim), F32), sin], axis=0)
    return jnp.tile(cos, (1, copies)), jnp.tile(sin, (1, copies))


def _group_mean_matrix(width, group):
    g = np.kron(np.eye(width // group, dtype=np.float32), np.ones((group, group), np.float32)) / group
    return jnp.asarray(g, BF16)


def kernel(x, c, ctx, c_ctx, w_mod, b_mod, norm_mix, norm_ffn, w_in, w_out, gla_a2_f, gla_ab_f, gla_a2_b,
           gla_ab_b, gla_norm, gqa_q_norm, gqa_k_norm, diff_lam_q1, diff_lam_k1, diff_lam_q2, diff_lam_k2,
           diff_norm, ffn_gate, ffn_up, ffn_down, moe_router, moe_gate, moe_up, moe_down, norm_f):
    bsz, seq, d = x.shape
    n_ctx = ctx.shape[1]
    depth = w_mod.shape[0]

    c_rows = jnp.concatenate([c, c_ctx[None, :], jnp.zeros((8 - bsz - 1, d), F32)], axis=0)
    mod = _modulation(c_rows, w_mod, b_mod).reshape(depth, 8, 6, d)
    mod = jnp.pad(mod, ((0, 0), (0, 0), (0, 2), (0, 0)))
    modsel = jnp.stack([jnp.broadcast_to(mod[:, bsz][:, None], (depth, bsz, 8, d)), mod[:, :bsz]], axis=2)

    tables = (_rope_tables(n_ctx, seq, GQA_HD, LANES // GQA_HD) + _rope_tables(n_ctx, seq, DIFF_QK, LANES // DIFF_QK))
    gmat = _group_mean_matrix(GQA_HEADS * GQA_HD, GQA_HD)
    gmat_v = _group_mean_matrix(GLA_HEADS * GLA_DV, GLA_DV)

    w_in_p = jnp.concatenate([w_in[:, :, :_W_IN_LR_END],
                              jnp.zeros((depth, d, _W_GQA_Q - _W_IN_LR_END), F32),
                              w_in[:, :, _W_IN_LR_END:]], axis=2).astype(BF16)
    w_out_b = w_out.astype(BF16)
    a2 = jnp.zeros((depth, 2, LANES, LANES), F32)
    a2 = a2.at[:, 0, 0:GLA_GATE_RANK].set(gla_a2_f).at[:, 1, GLA_GATE_RANK:2 * GLA_GATE_RANK].set(gla_a2_b)
    a2 = a2.astype(BF16)
    ab = jnp.stack([gla_ab_f, gla_ab_b], axis=1)[:, :, None, :]

    x_all, x_ctx = x, ctx
    out = None
    for l in range(depth):
        ctx_out = l < depth - 1
        lambda_init = 0.8 - 0.6 * math.exp(-0.3 * l)
        gla_qk, gla_v, gla_r, gla_lr, gq, gk, gv, dq, dk, dv = _in_projection(
            x_all, x_ctx, modsel[l], norm_mix[l][None, :], w_in_p[l], gmat,
            jnp.tile(gqa_q_norm[l], GQA_HEADS)[None, :], jnp.tile(gqa_k_norm[l], GQA_KV_HEADS)[None, :], tables)
        o_f, o_b = _gla(gla_qk, gla_v, gla_lr, a2[l], ab[l], n_ctx // GLA_CHUNK)
        lam_rows = jnp.stack([diff_lam_q1[l], diff_lam_k1[l], diff_lam_q2[l], diff_lam_k2[l]], axis=0)
        gqa_o, diff_o = _attention(gq, gk, gv, dq, dk, dv, lam_rows, diff_norm[l][None, :], n_ctx, lambda_init,
                                   ctx_out)
        moe_layer = l % 2 == 1
        i = l // 2
        res = _out_projection(x_all, x_ctx, modsel[l], o_f, o_b, gla_r, gqa_o, diff_o, w_out_b[l], gmat_v,
                              jnp.tile(gla_norm[l], GLA_HEADS)[None, :], norm_ffn[l][None, :],
                              moe_router[i].T if moe_layer else None, ctx_out)
        if not moe_layer:
            x1, h2 = res
            if not ctx_out:
                raise NotImplementedError("dense feed-forward on the last layer")
            x_all = _dense_ffn(h2, x1, modsel[l], ffn_gate[i].astype(BF16), ffn_up[i].astype(BF16),
                               ffn_down[i].astype(BF16), n_ctx)
            x_ctx = None
            if l == depth - 1:
                out = _final_norm(x_all, norm_f[None, :], n_ctx)
        else:
            x1, h2, route = res
            if ctx_out or l != depth - 1:
                raise NotImplementedError("expert feed-forward before the last layer")
            n = bsz * seq
            rows, pos, tile_expert, n_valid, valid_rows = _moe_plan(route.reshape(n, LANES), MOE_TILE)
            xs = _moe_dispatch(pos, h2.reshape(n, d), rows)
            ys = _moe_experts(tile_expert, n_valid, valid_rows, xs, moe_gate[i], moe_up[i], moe_down[i])
            ya = ys.at[pos[0::2]].get(mode="promise_in_bounds").reshape(bsz, seq, d)
            yb = ys.at[pos[1::2]].get(mode="promise_in_bounds").reshape(bsz, seq, d)
            out = _moe_combine(x1, modsel[l], route, ya, yb, norm_f[None, :])
    return out
```
